```python
import math
import jax, jax.numpy as jnp
from jax import lax
import numpy as np

D_MODEL = 1024
BATCH = 4
SEQ = 8192
DEPTH = 2

N_HEADS = 8
HEAD_DIM = D_MODEL // N_HEADS
Q_LORA = 256
KV_LORA = 128
IDX_HEADS = 8
IDX_DIM = 64
IDX_TOPK_MAX = 256
N_KV_GROUPS = 2
HEADS_PER_GROUP = N_HEADS // N_KV_GROUPS
CMP_LEN = 32
CMP_STRIDE = 16
CMP_HID = 256
SLC_BLK = 64
N_SLC_MAX = 16
WIN = 512
D_FF = 4 * D_MODEL
REL_BUCKETS = 32
REL_MAX_DIST = 4096
Q_BLK = 128
EPS = 1e-6
NEG = -1e30
N_A_LAYERS = DEPTH // 2
N_B_LAYERS = DEPTH - N_A_LAYERS

kernel_name = 'yoco_dsa_nsa_hybrid'


def rmsnorm(x, g):
    xf = x.astype(jnp.float32)
    y = xf * lax.rsqrt(jnp.mean(xf * xf, axis=-1, keepdims=True) + EPS)
    return (y * g.astype(jnp.float32)).astype(x.dtype)


def masked_softmax(logits, mask):
    logits = jnp.where(mask, logits, NEG)
    m = jnp.max(logits, axis=-1, keepdims=True)
    e = jnp.where(mask, jnp.exp(logits - m), 0.0)
    return e / jnp.maximum(jnp.sum(e, axis=-1, keepdims=True), 1e-30)


def rel_bucket(dist):
    dist = jnp.maximum(dist, 0)
    exact = REL_BUCKETS // 2
    log_ratio = jnp.log(jnp.maximum(dist, 1).astype(jnp.float32) / exact) / math.log(REL_MAX_DIST / exact)
    large = exact + (log_ratio * (REL_BUCKETS - exact)).astype(jnp.int32)
    return jnp.where(dist < exact, dist, jnp.minimum(large, REL_BUCKETS - 1))


def gather_rows(table, idx):
    return table[idx]


def to_blocks(a):
    return jnp.moveaxis(a.reshape(a.shape[0], a.shape[1] // Q_BLK, Q_BLK, *a.shape[2:]), 1, 0)


def from_blocks(a):
    a = jnp.moveaxis(a, 0, 1)
    return a.reshape(a.shape[0], a.shape[1] * a.shape[2], *a.shape[3:])


def sq_relu_mlp(h, w_up, w_down):
    return jnp.square(jax.nn.relu(h @ w_up)) @ w_down


def dsa_mixer(h, rel_bias, w_in, g_q_lat, g_kv_lat, g_k_idx, w_uq, w_q_idx, w_uk, w_uv, w_o):
    B, T, _ = h.shape
    top_k = min(IDX_TOPK_MAX, T // 4)
    proj = h @ w_in
    c_q, c_kv, k_idx, w_idx = jnp.split(proj, [Q_LORA, Q_LORA + KV_LORA, Q_LORA + KV_LORA + IDX_DIM], axis=-1)
    c_q = rmsnorm(c_q, g_q_lat)
    c_kv = rmsnorm(c_kv, g_kv_lat)
    k_idx = rmsnorm(k_idx, g_k_idx)
    q = jnp.einsum('btr,rhd->bthd', c_q, w_uq)
    q_abs = jnp.einsum('bthd,chd->bthc', q, w_uk) * HEAD_DIM ** -0.5
    q_idx = jnp.einsum('btr,rhd->bthd', c_q, w_q_idx)
    w_idx = w_idx * (IDX_HEADS ** -0.5 * IDX_DIM ** -0.5)
    key_pos = jnp.arange(T, dtype=jnp.int32)

    def block(args):
        qa, qi, wi, t0 = args
        t = t0 + jnp.arange(Q_BLK, dtype=jnp.int32)
        rel = jax.nn.relu(jnp.einsum('bqhd,bsd->bqhs', qi, k_idx).astype(jnp.float32))
        score = jnp.einsum('bqh,bqhs->bqs', wi.astype(jnp.float32), rel)
        score = jnp.where(key_pos[None, None, :] <= t[None, :, None], score, NEG)
        _, sel = lax.top_k(score, top_k)
        c_sel = jax.vmap(gather_rows)(c_kv, sel)
        dist = t[None, :, None] - sel
        logits = jnp.einsum('bqhc,bqkc->bqhk', qa, c_sel).astype(jnp.float32)
        logits = logits + jnp.transpose(rel_bias[rel_bucket(dist)], (0, 1, 3, 2))
        p = masked_softmax(logits, (dist >= 0)[:, :, None, :])
        return jnp.einsum('bqhk,bqkc->bqhc', p.astype(c_sel.dtype), c_sel)

    t0s = jnp.arange(T // Q_BLK, dtype=jnp.int32) * Q_BLK
    o_lat = from_blocks(lax.map(block, (to_blocks(q_abs), to_blocks(q_idx), to_blocks(w_idx), t0s)))
    o = jnp.einsum('bthc,chd->bthd', o_lat, w_uv).reshape(B, T, N_HEADS * HEAD_DIM)
    return o @ w_o


def compress_blocks(raw, n_cmp, pos, w1, w2):
    B = raw.shape[0]
    idx = CMP_STRIDE * np.arange(n_cmp)[:, None] + np.arange(CMP_LEN)[None, :]
    blocks = raw[:, idx] + pos[None, None, :, None, :]
    flat = jnp.transpose(blocks, (0, 1, 3, 2, 4)).reshape(B, n_cmp, N_KV_GROUPS, CMP_LEN * HEAD_DIM)
    return jax.nn.gelu(flat @ w1) @ w2


def selection_map(n_cmp, n_slc):
    c0 = CMP_STRIDE * np.arange(n_cmp)[:, None]
    s0 = SLC_BLK * np.arange(n_slc)[None, :]
    ov = np.clip(np.minimum(c0 + CMP_LEN, s0 + SLC_BLK) - np.maximum(c0, s0), 0, None)
    return (ov / CMP_LEN).astype(np.float32)


def shared_kv(x, g_kv, w_kv, pos_k, pos_v, w1_k, w2_k, w1_v, w2_v):
    B, T, _ = x.shape
    hs = rmsnorm(x, g_kv)
    kv = (hs @ w_kv).reshape(B, T, 6, N_KV_GROUPS, HEAD_DIM)
    k_cmp, v_cmp, k_slc, v_slc, k_win, v_win = [kv[:, :, i] for i in range(6)]
    n_cmp = (T - CMP_LEN) // CMP_STRIDE + 1
    kc = compress_blocks(k_cmp, n_cmp, pos_k, w1_k, w2_k)
    vc = compress_blocks(v_cmp, n_cmp, pos_v, w1_v, w2_v)
    n_slc = T // SLC_BLK

    def to_sel(a):
        return jnp.transpose(a.reshape(B, n_slc, SLC_BLK, N_KV_GROUPS, HEAD_DIM), (0, 3, 1, 2, 4))

    pad = ((0, 0), (WIN, 0), (0, 0), (0, 0))
    return (kc, vc, to_sel(k_slc), to_sel(v_slc), jnp.pad(k_win, pad), jnp.pad(v_win, pad))


def nsa_mixer(h, rel_bias, kv_shared, w_in, w_o):
    kc, vc, ks_blk, vs_blk, kw_pad, vw_pad = kv_shared
    B, T, _ = h.shape
    G, R = N_KV_GROUPS, HEADS_PER_GROUP
    n_cmp = kc.shape[1]
    n_slc = ks_blk.shape[2]
    n_sel = min(N_SLC_MAX, n_slc)
    cmp_end = jnp.asarray(CMP_STRIDE * np.arange(n_cmp) + CMP_LEN - 1, dtype=jnp.int32)
    sel_map = jnp.asarray(selection_map(n_cmp, n_slc))
    blk_ids = jnp.arange(n_slc, dtype=jnp.int32)
    bias_gr = rel_bias.reshape(REL_BUCKETS, G, R)
    g_ids = jnp.arange(G, dtype=jnp.int32)[None, None, :, None]
    gather_groups = jax.vmap(jax.vmap(gather_rows, in_axes=(0, 1), out_axes=1))

    proj = h @ w_in
    q = proj[..., :N_HEADS * HEAD_DIM].reshape(B, T, G, R, HEAD_DIM) * HEAD_DIM ** -0.5
    gates = jax.nn.sigmoid(proj[..., N_HEADS * HEAD_DIM:].astype(jnp.float32)).astype(h.dtype).reshape(B, T, G, R, 3)

    def block(args):
        qb, gb, t0 = args
        t = t0 + jnp.arange(Q_BLK, dtype=jnp.int32)
        lc = jnp.einsum('bqgrd,bngd->bqgrn', qb, kc).astype(jnp.float32)
        pc = masked_softmax(lc, (cmp_end[None, :] <= t[:, None])[None, :, None, None, :])
        o_cmp = jnp.einsum('bqgrn,bngd->bqgrd', pc.astype(vc.dtype), vc)
        imp = jnp.einsum('bqgrn,nj->bqgj', pc, sel_map)
        cur = t[:, None] // SLC_BLK
        forced = (blk_ids[None, :] == 0) | (blk_ids[None, :] == cur) | (blk_ids[None, :] == cur - 1)
        future = blk_ids[None, :] * SLC_BLK > t[:, None]
        imp = jnp.where(forced[None, :, None, :], 1e9, imp)
        imp = jnp.where(future[None, :, None, :], NEG, imp)
        _, sel = lax.top_k(imp, n_sel)
        k_sel = gather_groups(ks_blk, sel).reshape(B, Q_BLK, G, n_sel * SLC_BLK, HEAD_DIM)
        v_sel = gather_groups(vs_blk, sel).reshape(B, Q_BLK, G, n_sel * SLC_BLK, HEAD_DIM)
        pos = (sel[..., None] * SLC_BLK + jnp.arange(SLC_BLK, dtype=jnp.int32)).reshape(B, Q_BLK, G, n_sel * SLC_BLK)
        dist_s = t[None, :, None, None] - pos
        bias_s = jnp.moveaxis(bias_gr[rel_bucket(dist_s), g_ids], -1, 3)
        ls = jnp.einsum('bqgrd,bqgkd->bqgrk', qb, k_sel).astype(jnp.float32) + bias_s
        ps = masked_softmax(ls, (dist_s >= 0)[:, :, :, None, :])
        o_slc = jnp.einsum('bqgrk,bqgkd->bqgrd', ps.astype(v_sel.dtype), v_sel)
        kw = lax.dynamic_slice_in_dim(kw_pad, t0, WIN + Q_BLK, axis=1)
        vw = lax.dynamic_slice_in_dim(vw_pad, t0, WIN + Q_BLK, axis=1)
        s_w = t0 - WIN + jnp.arange(WIN + Q_BLK, dtype=jnp.int32)
        dist_w = t[:, None] - s_w[None, :]
        wmask = (dist_w >= 0) & (dist_w < WIN) & (s_w[None, :] >= 0)
        bias_w = jnp.transpose(rel_bias[rel_bucket(dist_w)].reshape(Q_BLK, WIN + Q_BLK, G, R), (0, 2, 3, 1))
        lw = jnp.einsum('bqgrd,bsgd->bqgrs', qb, kw).astype(jnp.float32) + bias_w[None]
        pw = masked_softmax(lw, wmask[None, :, None, None, :])
        o_win = jnp.einsum('bqgrs,bsgd->bqgrd', pw.astype(vw.dtype), vw)
        return gb[..., 0:1] * o_cmp + gb[..., 1:2] * o_slc + gb[..., 2:3] * o_win

    t0s = jnp.arange(T // Q_BLK, dtype=jnp.int32) * Q_BLK
    o = from_blocks(lax.map(block, (to_blocks(q), to_blocks(gates), t0s)))
    return o.reshape(B, T, N_HEADS * HEAD_DIM) @ w_o


def setup_inputs(seed: int = 0) -> dict:
    key = jax.random.key(seed)
    ks = jax.random.split(key, 27)

    def nrm(k, shape, scale):
        return jax.random.normal(k, shape, jnp.float32) * scale

    def gain(k, shape):
        return 1.0 + 0.01 * jax.random.normal(k, shape, jnp.float32)

    hd = N_HEADS * HEAD_DIM
    return {
        'x': nrm(ks[0], (BATCH, SEQ, D_MODEL), 1.0),
        'g_attn': gain(ks[1], (DEPTH, D_MODEL)),
        'g_mlp': gain(ks[2], (DEPTH, D_MODEL)),
        'w_up': nrm(ks[3], (DEPTH, D_MODEL, D_FF), D_MODEL ** -0.5),
        'w_down': nrm(ks[4], (DEPTH, D_FF, D_MODEL), D_FF ** -0.5),
        'rel_bias': nrm(ks[5], (REL_BUCKETS, N_HEADS), 0.5),
        'a_w_in': nrm(ks[6], (N_A_LAYERS, D_MODEL, Q_LORA + KV_LORA + IDX_DIM + IDX_HEADS), D_MODEL ** -0.5),
        'a_g_q_lat': gain(ks[7], (N_A_LAYERS, Q_LORA)),
        'a_g_kv_lat': gain(ks[8], (N_A_LAYERS, KV_LORA)),
        'a_g_k_idx': gain(ks[9], (N_A_LAYERS, IDX_DIM)),
        'a_w_uq': nrm(ks[10], (N_A_LAYERS, Q_LORA, N_HEADS, HEAD_DIM), Q_LORA ** -0.5),
        'a_w_q_idx': nrm(ks[11], (N_A_LAYERS, Q_LORA, IDX_HEADS, IDX_DIM), Q_LORA ** -0.5),
        'a_w_uk': nrm(ks[12], (N_A_LAYERS, KV_LORA, N_HEADS, HEAD_DIM), KV_LORA ** -0.5),
        'a_w_uv': nrm(ks[13], (N_A_LAYERS, KV_LORA, N_HEADS, HEAD_DIM), KV_LORA ** -0.5),
        'a_w_o': nrm(ks[14], (N_A_LAYERS, hd, D_MODEL), hd ** -0.5),
        'g_kv_shared': gain(ks[15], (D_MODEL,)),
        'w_kv_shared': nrm(ks[16], (D_MODEL, 6 * N_KV_GROUPS * HEAD_DIM), D_MODEL ** -0.5),
        'cmp_pos_k': nrm(ks[17], (CMP_LEN, HEAD_DIM), 0.5),
        'cmp_pos_v': nrm(ks[18], (CMP_LEN, HEAD_DIM), 0.5),
        'cmp_w1_k': nrm(ks[19], (CMP_LEN * HEAD_DIM, CMP_HID), (CMP_LEN * HEAD_DIM) ** -0.5),
        'cmp_w2_k': nrm(ks[20], (CMP_HID, HEAD_DIM), CMP_HID ** -0.5),
        'cmp_w1_v': nrm(ks[21], (CMP_LEN * HEAD_DIM, CMP_HID), (CMP_LEN * HEAD_DIM) ** -0.5),
        'cmp_w2_v': nrm(ks[22], (CMP_HID, HEAD_DIM), CMP_HID ** -0.5),
        'b_w_in': nrm(ks[23], (N_B_LAYERS, D_MODEL, hd + 3 * N_HEADS), D_MODEL ** -0.5),
        'b_w_o': nrm(ks[24], (N_B_LAYERS, hd, D_MODEL), hd ** -0.5),
        'g_final': gain(ks[25], (D_MODEL,)),
    }


def reference(x, g_attn, g_mlp, w_up, w_down, rel_bias, a_w_in, a_g_q_lat, a_g_kv_lat, a_g_k_idx,
              a_w_uq, a_w_q_idx, a_w_uk, a_w_uv, a_w_o, g_kv_shared, w_kv_shared, cmp_pos_k, cmp_pos_v,
              cmp_w1_k, cmp_w2_k, cmp_w1_v, cmp_w2_v, b_w_in, b_w_o, g_final):
    kv_shared = None
    for l in range(DEPTH):
        h = rmsnorm(x, g_attn[l])
        if l < N_A_LAYERS:
            x = x + dsa_mixer(h, rel_bias, a_w_in[l], a_g_q_lat[l], a_g_kv_lat[l], a_g_k_idx[l],
                              a_w_uq[l], a_w_q_idx[l], a_w_uk[l], a_w_uv[l], a_w_o[l])
        else:
            j = l - N_A_LAYERS
            x = x + nsa_mixer(h, rel_bias, kv_shared, b_w_in[j], b_w_o[j])
        x = x + sq_relu_mlp(rmsnorm(x, g_mlp[l]), w_up[l], w_down[l])
        if l == N_A_LAYERS - 1:
            kv_shared = shared_kv(x, g_kv_shared, w_kv_shared, cmp_pos_k, cmp_pos_v,
                                  cmp_w1_k, cmp_w2_k, cmp_w1_v, cmp_w2_v)
    return rmsnorm(x, g_final)
```

```python
import functools
import math

import numpy as np
import jax
import jax.numpy as jnp
from jax import lax
from jax.experimental import pallas as pl
from jax.experimental.pallas import tpu as pltpu

N_HEADS = 8
HEAD_DIM = 128
Q_LORA = 256
KV_LORA = 128
IDX_HEADS = 8
IDX_DIM = 64
IDX_TOPK_MAX = 256
N_KV_GROUPS = 2
HEADS_PER_GROUP = N_HEADS // N_KV_GROUPS
CMP_LEN = 32
CMP_STRIDE = 16
CMP_HID = 256
SLC_BLK = 64
N_SLC_MAX = 16
WIN = 512
REL_BUCKETS = 32
REL_MAX_DIST = 4096
EPS = 1e-6
NEG = -1e30

LANES = 128
INT_MIN = np.int32(-2 ** 31)
VMEM_LIMIT = 56 * 1024 * 1024

F32 = jnp.float32
BF16 = jnp.bfloat16
NT_DIMS = (((1,), (1,)), ((), ()))


def _dot(a, b):
    return jnp.dot(a, b, preferred_element_type=F32)


def _dot_nt(a, b):
    return lax.dot_general(a, b, NT_DIMS, preferred_element_type=F32)


def _rms(x, g):
    return x * lax.rsqrt(jnp.mean(x * x, axis=-1, keepdims=True) + EPS) * g


def _const_spec(shape):
    nd = len(shape)
    return pl.BlockSpec(shape, lambda *_: (0,) * nd, pipeline_mode=pl.Buffered(1))


def _params(sem):
    return pltpu.CompilerParams(dimension_semantics=sem, vmem_limit_bytes=VMEM_LIMIT)


def _rel_bucket(dist):
    dist = jnp.maximum(dist, 0)
    exact = REL_BUCKETS // 2
    log_ratio = jnp.log(jnp.maximum(dist, 1).astype(F32) / exact) / math.log(REL_MAX_DIST / exact)
    large = exact + (log_ratio * (REL_BUCKETS - exact)).astype(jnp.int32)
    return jnp.where(dist < exact, dist, jnp.minimum(large, REL_BUCKETS - 1))


def _num_bias_tiles(T):
    exact = REL_BUCKETS // 2
    switch = exact * (REL_MAX_DIST / exact) ** ((REL_BUCKETS - exact - 1) / (REL_BUCKETS - exact))
    far = int(math.ceil(switch)) + 32
    return min(T // LANES, -(-(far + LANES - 1) // LANES) + 1)


def _bias_tiles(rel_bias, T):
    nd = _num_bias_tiles(T)
    d = jnp.arange(nd, dtype=jnp.int32)[:, None, None] * LANES
    i = jnp.arange(LANES, dtype=jnp.int32)[None, :, None]
    j = jnp.arange(LANES, dtype=jnp.int32)[None, None, :]
    tiles = rel_bias[_rel_bucket(d + i - j)]
    return jnp.transpose(tiles, (0, 3, 1, 2)).astype(F32)


def _dsa_proj_kernel(x_ref, g_ref, w_in_ref, gq_ref, gkv_ref, gk_ref, w_uq_ref, w_ukt_ref, w_qi_ref,
                     qa_ref, qi_ref, ckv_ref, kidx_ref, wi_ref):
    h = _rms(x_ref[0], g_ref[...]).astype(BF16)
    proj = _dot(h, w_in_ref[...])
    c_q = _rms(proj[:, :Q_LORA], gq_ref[...]).astype(BF16)
    ckv_ref[0] = _rms(proj[:, Q_LORA:Q_LORA + KV_LORA], gkv_ref[...]).astype(BF16)
    o_k = Q_LORA + KV_LORA
    kidx_ref[0] = _rms(proj[:, o_k:o_k + IDX_DIM], gk_ref[...]).astype(BF16)
    o_w = o_k + LANES
    wi_ref[0] = proj[:, o_w:o_w + IDX_HEADS] * (IDX_HEADS ** -0.5 * IDX_DIM ** -0.5)
    q = _dot(c_q, w_uq_ref[...])
    for hh in range(N_HEADS):
        qh = q[:, hh * HEAD_DIM:(hh + 1) * HEAD_DIM].astype(BF16)
        qa_ref[0, hh] = (_dot(qh, w_ukt_ref[hh]) * HEAD_DIM ** -0.5).astype(BF16)
        qi_ref[0, hh] = _dot(c_q, w_qi_ref[hh]).astype(BF16)


def _dsa_proj(x, g, w_in, gq, gkv, gk, w_uq, w_uk, w_q_idx, tm=256):
    B, T, D = x.shape
    n_in = Q_LORA + KV_LORA + 2 * LANES
    w_in_p = jnp.zeros((D, n_in), F32)
    o_k = Q_LORA + KV_LORA
    w_in_p = w_in_p.at[:, :o_k + IDX_DIM].set(w_in[:, :o_k + IDX_DIM])
    w_in_p = w_in_p.at[:, o_k + LANES:o_k + LANES + IDX_HEADS].set(w_in[:, o_k + IDX_DIM:])
    w_ukt = jnp.transpose(w_uk, (1, 2, 0)).astype(BF16)
    w_qi = jnp.transpose(w_q_idx, (1, 0, 2)).astype(BF16)
    tile = lambda b, i: (b, i, 0)
    htile = lambda b, i: (b, 0, i, 0)
    return pl.pallas_call(
        _dsa_proj_kernel,
        grid=(B, T // tm),
        in_specs=[
            pl.BlockSpec((1, tm, D), tile),
            _const_spec((1, D)),
            _const_spec((D, n_in)),
            _const_spec((1, Q_LORA)), _const_spec((1, KV_LORA)), _const_spec((1, IDX_DIM)),
            _const_spec((Q_LORA, N_HEADS * HEAD_DIM)),
            _const_spec((N_HEADS, HEAD_DIM, KV_LORA)),
            _const_spec((IDX_HEADS, Q_LORA, IDX_DIM)),
        ],
        out_specs=[
            pl.BlockSpec((1, N_HEADS, tm, KV_LORA), htile),
            pl.BlockSpec((1, IDX_HEADS, tm, IDX_DIM), htile),
            pl.BlockSpec((1, tm, KV_LORA), tile),
            pl.BlockSpec((1, tm, IDX_DIM), tile),
            pl.BlockSpec((1, tm, IDX_HEADS), tile),
        ],
        out_shape=[
            jax.ShapeDtypeStruct((B, N_HEADS, T, KV_LORA), BF16),
            jax.ShapeDtypeStruct((B, IDX_HEADS, T, IDX_DIM), BF16),
            jax.ShapeDtypeStruct((B, T, KV_LORA), BF16),
            jax.ShapeDtypeStruct((B, T, IDX_DIM), BF16),
            jax.ShapeDtypeStruct((B, T, IDX_HEADS), F32),
        ],
        compiler_params=_params(("parallel", "parallel")),
        name="dsa_proj",
    )(x, g.reshape(1, D), w_in_p.astype(BF16), gq.reshape(1, -1), gkv.reshape(1, -1), gk.reshape(1, -1),
      w_uq.reshape(Q_LORA, N_HEADS * HEAD_DIM).astype(BF16), w_ukt, w_qi)


def _bias_row(toe_ref, head, d0, n_sub, nd):
    parts = [toe_ref[jnp.clip(d0 - k, 0, nd - 1), head] for k in range(n_sub)]
    return parts[0] if n_sub == 1 else jnp.concatenate(parts, axis=1)


def _tile_lanes(a, n):
    return a if n == 1 else jnp.concatenate([a] * n, axis=1)


def _online_softmax_head(lg, row, tq, m_scr, l_scr, acc_scr, p_scr, head, prow):
    n_rep = lg.shape[1] // LANES
    m_old = m_scr[head]
    m_new = jnp.maximum(m_old, jnp.max(lg, axis=1, keepdims=True))
    alpha = jnp.exp(m_old - m_new)
    p = jnp.exp(lg - _tile_lanes(m_new, n_rep))
    l_scr[head] = alpha * l_scr[head] + jnp.sum(p, axis=1, keepdims=True)
    m_scr[head] = m_new
    p_scr[prow * tq:(prow + 1) * tq, :] = p.astype(BF16)
    acc_scr[row * tq:(row + 1) * tq, :] = acc_scr[row * tq:(row + 1) * tq, :] * alpha


def _dsa_attn_kernel(qi_ref, wi_ref, kidx_ref, qa_ref, ckv_ref, toe_ref, o_ref,
                     s_scr, p_scr, m_scr, l_scr, acc_scr, *, tq, tk, top_k, nd, idx_bits):
    t0 = pl.program_id(1) * tq
    nch = (t0 + tq + tk - 1) // tk
    n_sub = tk // LANES
    row_t = t0 + lax.broadcasted_iota(jnp.int32, (tq, tk), 0)
    col = lax.broadcasted_iota(jnp.int32, (tq, tk), 1)
    lane = lax.broadcasted_iota(jnp.int32, (tq, LANES), 1)

    wi = wi_ref[0]

    def score_chunk(c, carry):
        s0 = pl.multiple_of(c * tk, tk)
        kc = kidx_ref[0, pl.ds(s0, tk), :]
        acc = jnp.zeros((tq, tk), F32)
        for hh in range(IDX_HEADS):
            rel = _dot_nt(qi_ref[0, hh], kc)
            acc = acc + wi[:, hh:hh + 1] * jnp.maximum(rel, 0.0)
        bits = lax.bitcast_convert_type(acc + 0.0, jnp.int32)
        key = bits ^ ((bits >> 31) & jnp.int32(0x7FFFFFFF))
        s_scr[c] = jnp.where(col + s0 <= row_t, key, INT_MIN)
        return carry

    lax.fori_loop(0, nch, score_chunk, 0)

    def count(pred):
        def body(c, cnt):
            for j in range(n_sub):
                blk = s_scr[c, :, j * LANES:(j + 1) * LANES]
                cnt = cnt + jnp.where(pred(blk, lane + (c * tk + j * LANES)), 1.0, 0.0)
            return cnt
        cnt = lax.fori_loop(0, nch, body, jnp.zeros((tq, LANES), F32))
        return jnp.sum(cnt, axis=1, keepdims=True)

    kf = jnp.float32(top_k)

    def thr_bit(bi, carry):
        thr, n_ge = carry
        cand = thr ^ lax.shift_left(jnp.int32(1), 31 - bi)
        cnt = count(lambda blk, idx: blk >= cand)
        ok = cnt >= kf
        return jnp.where(ok, cand, thr), jnp.where(ok, cnt, n_ge)

    thr, n_ge = lax.fori_loop(0, 32, thr_bit,
                              (jnp.full((tq, LANES), INT_MIN, jnp.int32), jnp.zeros((tq, 1), F32)))
    thr = jnp.maximum(thr, INT_MIN + 1)

    @pl.when(jnp.max(n_ge) > kf)
    def _():
        need = kf - count(lambda blk, idx: blk > thr)

        def idx_bit(bi, last):
            cand = last | lax.shift_left(jnp.int32(1), idx_bits - 1 - bi)
            cnt = count(lambda blk, idx: (blk == thr) & (idx < cand))
            return jnp.where(cnt < need, cand, last)

        last = lax.fori_loop(0, idx_bits, idx_bit, jnp.zeros((tq, LANES), jnp.int32))

        def drop(c, carry):
            for j in range(n_sub):
                sl = slice(j * LANES, (j + 1) * LANES)
                blk = s_scr[c, :, sl]
                idx = lane + (c * tk + j * LANES)
                s_scr[c, :, sl] = jnp.where((blk == thr) & (idx > last), INT_MIN, blk)
            return carry

        lax.fori_loop(0, nch, drop, 0)

    m_scr[...] = jnp.full(m_scr.shape, NEG, F32)
    l_scr[...] = jnp.zeros(l_scr.shape, F32)
    acc_scr[...] = jnp.zeros(acc_scr.shape, F32)
    thr_full = _tile_lanes(thr, n_sub)
    qa = qa_ref[0].reshape(N_HEADS * tq, KV_LORA)

    def attn_chunk(c, carry):
        s0 = pl.multiple_of(c * tk, tk)
        kv = ckv_ref[0, pl.ds(s0, tk), :]
        mb = jnp.where(s_scr[c] >= thr_full, 0.0, NEG)
        lg_all = _dot_nt(qa, kv)
        d0 = (t0 - s0) // LANES
        for hh in range(N_HEADS):
            lg = lg_all[hh * tq:(hh + 1) * tq] + _bias_row(toe_ref, hh, d0, n_sub, nd) + mb
            _online_softmax_head(lg, hh, tq, m_scr, l_scr, acc_scr, p_scr, hh, hh)
        acc_scr[...] = acc_scr[...] + _dot(p_scr[...], kv)
        return carry

    lax.fori_loop(0, nch, attn_chunk, 0)
    for hh in range(N_HEADS):
        o = acc_scr[hh * tq:(hh + 1) * tq, :] / l_scr[hh]
        o_ref[0, :, hh * KV_LORA:(hh + 1) * KV_LORA] = o.astype(o_ref.dtype)


def _dsa_attn(qi, wi, kidx, qa, ckv, toe, tq=128, tk=512):
    B, _, T, _ = qa.shape
    tk = min(tk, T)
    nd = toe.shape[0]
    top_k = min(IDX_TOPK_MAX, T // 4)
    kern = functools.partial(_dsa_attn_kernel, tq=tq, tk=tk, top_k=top_k, nd=nd,
                             idx_bits=max(1, (T - 1).bit_length()))
    return pl.pallas_call(
        kern,
        grid=(B, T // tq),
        in_specs=[
            pl.BlockSpec((1, IDX_HEADS, tq, IDX_DIM), lambda b, i: (b, 0, i, 0)),
            pl.BlockSpec((1, tq, IDX_HEADS), lambda b, i: (b, i, 0)),
            pl.BlockSpec((1, T, IDX_DIM), lambda b, i: (b, 0, 0), pipeline_mode=pl.Buffered(1)),
            pl.BlockSpec((1, N_HEADS, tq, KV_LORA), lambda b, i: (b, 0, i, 0)),
            pl.BlockSpec((1, T, KV_LORA), lambda b, i: (b, 0, 0), pipeline_mode=pl.Buffered(1)),
            _const_spec(toe.shape),
        ],
        out_specs=pl.BlockSpec((1, tq, N_HEADS * KV_LORA), lambda b, i: (b, i, 0)),
        out_shape=jax.ShapeDtypeStruct((B, T, N_HEADS * KV_LORA), BF16),
        scratch_shapes=[
            pltpu.VMEM((T // tk, tq, tk), jnp.int32),
            pltpu.VMEM((N_HEADS * tq, tk), BF16),
            pltpu.VMEM((N_HEADS, tq, LANES), F32),
            pltpu.VMEM((N_HEADS, tq, LANES), F32),
            pltpu.VMEM((N_HEADS * tq, KV_LORA), F32),
        ],
        compiler_params=_params(("parallel", "arbitrary")),
        name="dsa_attn",
    )(qi, wi, kidx, qa, ckv, toe)


def _dsa_out_kernel(o_ref, x_ref, w_uv_ref, w_o_ref, y_ref):
    o_lat = o_ref[0]
    parts = [_dot(o_lat[:, hh * KV_LORA:(hh + 1) * KV_LORA], w_uv_ref[hh]).astype(BF16)
             for hh in range(N_HEADS)]
    y_ref[0] = x_ref[0] + _dot(jnp.concatenate(parts, axis=1), w_o_ref[...])


def _dsa_out(o_lat, x, w_uv, w_o, tm=512):
    B, T, D = x.shape
    tile = lambda b, i: (b, i, 0)
    return pl.pallas_call(
        _dsa_out_kernel,
        grid=(B, T // tm),
        in_specs=[pl.BlockSpec((1, tm, N_HEADS * KV_LORA), tile), pl.BlockSpec((1, tm, D), tile),
                  _const_spec((N_HEADS, KV_LORA, HEAD_DIM)), _const_spec((N_HEADS * HEAD_DIM, D))],
        out_specs=pl.BlockSpec((1, tm, D), tile),
        out_shape=jax.ShapeDtypeStruct((B, T, D), F32),
        compiler_params=_params(("parallel", "parallel")),
        name="dsa_out",
    )(o_lat, x, jnp.transpose(w_uv, (1, 0, 2)).astype(BF16), w_o.astype(BF16))


def _out_proj_kernel(o_ref, x_ref, w_o_ref, y_ref):
    y_ref[0] = x_ref[0] + _dot(o_ref[0], w_o_ref[...])


def _out_proj(o, x, w_o, tm=512):
    B, T, D = x.shape
    tile = lambda b, i: (b, i, 0)
    return pl.pallas_call(
        _out_proj_kernel,
        grid=(B, T // tm),
        in_specs=[pl.BlockSpec((1, tm, o.shape[-1]), tile), pl.BlockSpec((1, tm, D), tile),
                  _const_spec(w_o.shape)],
        out_specs=pl.BlockSpec((1, tm, D), tile),
        out_shape=jax.ShapeDtypeStruct((B, T, D), F32),
        compiler_params=_params(("parallel", "parallel")),
        name="out_proj",
    )(o, x, w_o.astype(BF16))


def _mlp_kernel(x_ref, g_ref, w_up_ref, w_down_ref, *rest, tf, final):
    y_ref = rest[-1]
    x = x_ref[0]
    h = _rms(x, g_ref[...]).astype(BF16)
    acc = x
    for f0 in range(0, w_up_ref.shape[1], tf):
        u = jnp.maximum(_dot(h, w_up_ref[:, f0:f0 + tf]), 0.0)
        acc = acc + _dot((u * u).astype(BF16), w_down_ref[f0:f0 + tf, :])
    y_ref[0] = _rms(acc, rest[0][...]) if final else acc


def _mlp(x, g, w_up, w_down, g_final=None, tm=512, tf=512):
    B, T, D = x.shape
    F = w_up.shape[1]
    tile = lambda b, i: (b, i, 0)
    final = g_final is not None
    in_specs = [pl.BlockSpec((1, tm, D), tile), _const_spec((1, D)), _const_spec((D, F)), _const_spec((F, D))]
    args = [x, g.reshape(1, D), w_up.astype(BF16), w_down.astype(BF16)]
    if final:
        in_specs.append(_const_spec((1, D)))
        args.append(g_final.reshape(1, D))
    return pl.pallas_call(
        functools.partial(_mlp_kernel, tf=tf, final=final),
        grid=(B, T // tm),
        in_specs=in_specs,
        out_specs=pl.BlockSpec((1, tm, D), tile),
        out_shape=jax.ShapeDtypeStruct((B, T, D), F32),
        compiler_params=_params(("parallel", "parallel")),
        name="mlp_final" if final else "mlp",
    )(*args)


def _kv_proj_kernel(x_ref, g_ref, w_ref, kc_ref, vc_ref, ks_ref, vs_ref, kw_ref, vw_ref):
    h = _rms(x_ref[0], g_ref[...]).astype(BF16)
    kv = _dot(h, w_ref[...])
    for part, ref in enumerate((kc_ref, vc_ref, ks_ref, vs_ref, kw_ref, vw_ref)):
        for g in range(N_KV_GROUPS):
            o = (part * N_KV_GROUPS + g) * HEAD_DIM
            ref[0, g] = kv[:, o:o + HEAD_DIM].astype(ref.dtype)


def _kv_proj(x, g, w_kv, tm=512):
    B, T, D = x.shape
    gtile = lambda b, i: (b, 0, i, 0)
    spec = pl.BlockSpec((1, N_KV_GROUPS, tm, HEAD_DIM), gtile)
    shp = lambda dt: jax.ShapeDtypeStruct((B, N_KV_GROUPS, T, HEAD_DIM), dt)
    return pl.pallas_call(
        _kv_proj_kernel,
        grid=(B, T // tm),
        in_specs=[pl.BlockSpec((1, tm, D), lambda b, i: (b, i, 0)), _const_spec((1, D)),
                  _const_spec(w_kv.shape)],
        out_specs=[spec] * 6,
        out_shape=[shp(F32), shp(F32), shp(BF16), shp(BF16), shp(BF16), shp(BF16)],
        compiler_params=_params(("parallel", "parallel")),
        name="kv_proj",
    )(x, g.reshape(1, D), w_kv.astype(BF16))


def _compress_kernel(raw_ref, pos_ref, w1_ref, w2_ref, o_ref):
    rows = raw_ref[0, 0]
    half = rows.shape[1]
    a = _dot((rows + pos_ref[:, :half]).astype(BF16), w1_ref[:half, :])
    b = _dot((rows + pos_ref[:, half:]).astype(BF16), w1_ref[half:, :])
    pre = a + pltpu.roll(b, rows.shape[0] - 1, 0)
    act = 0.5 * pre * (1.0 + jnp.tanh(math.sqrt(2.0 / math.pi) * (pre + 0.044715 * (pre * pre * pre))))
    o_ref[0, 0] = _dot(act.astype(BF16), w2_ref[...]).astype(o_ref.dtype)


def _compress(raw, pos, w1, w2):
    B, G, T, Dh = raw.shape
    nr = T // CMP_STRIDE
    rows = raw.reshape(B, G, nr, CMP_STRIDE * Dh)
    return pl.pallas_call(
        _compress_kernel,
        grid=(B, G),
        in_specs=[pl.BlockSpec((1, 1, nr, CMP_STRIDE * Dh), lambda b, g: (b, g, 0, 0)),
                  _const_spec((1, CMP_LEN * Dh)), _const_spec(w1.shape), _const_spec(w2.shape)],
        out_specs=pl.BlockSpec((1, 1, nr, Dh), lambda b, g: (b, g, 0, 0)),
        out_shape=jax.ShapeDtypeStruct((B, G, nr, Dh), BF16),
        compiler_params=_params(("parallel", "parallel")),
        name="compress",
    )(rows, pos.reshape(1, CMP_LEN * Dh), w1.astype(BF16), w2.astype(BF16))


def _nsa_proj_kernel(x_ref, g_ref, wq_ref, wg_ref, q_ref, gate_ref):
    h = _rms(x_ref[0], g_ref[...]).astype(BF16)
    q = _dot(h, wq_ref[...]) * HEAD_DIM ** -0.5
    for hh in range(N_HEADS):
        q_ref[0, hh] = q[:, hh * HEAD_DIM:(hh + 1) * HEAD_DIM].astype(BF16)
    gate_ref[0] = jax.nn.sigmoid(_dot(h, wg_ref[...]))[:, :3 * N_HEADS]


def _nsa_proj(x, g, w_in, tm=512):
    B, T, D = x.shape
    hd = N_HEADS * HEAD_DIM
    wq = w_in[:, :hd].astype(BF16)
    wg = jnp.zeros((D, LANES), F32).at[:, :3 * N_HEADS].set(w_in[:, hd:]).astype(BF16)
    return pl.pallas_call(
        _nsa_proj_kernel,
        grid=(B, T // tm),
        in_specs=[pl.BlockSpec((1, tm, D), lambda b, i: (b, i, 0)), _const_spec((1, D)),
                  _const_spec((D, hd)), _const_spec((D, LANES))],
        out_specs=[pl.BlockSpec((1, N_HEADS, tm, HEAD_DIM), lambda b, i: (b, 0, i, 0)),
                   pl.BlockSpec((1, tm, 3 * N_HEADS), lambda b, i: (b, i, 0))],
        out_shape=[jax.ShapeDtypeStruct((B, N_HEADS, T, HEAD_DIM), BF16),
                   jax.ShapeDtypeStruct((B, T, 3 * N_HEADS), F32)],
        compiler_params=_params(("parallel", "parallel")),
        name="nsa_proj",
    )(x, g.reshape(1, D), wq, wg)


def _nsa_attn_kernel(q_ref, gate_ref, kc_ref, vc_ref, ks_ref, vs_ref, kw_ref, vw_ref, selmap_ref, toe_ref,
                     o_ref, sel_scr, p_scr, m_scr, l_scr, acc_scr, ocw_scr, *, tq, tk, nd, n_slc, n_sel):
    G, R = N_KV_GROUPS, HEADS_PER_GROUP
    t0 = pl.program_id(1) * tq
    n_cmp = kc_ref.shape[2]
    n_sub = tk // LANES
    gates = gate_ref[0]
    t_col = t0 + lax.broadcasted_iota(jnp.int32, (tq, 1), 0)

    cmp_end = CMP_STRIDE * lax.broadcasted_iota(jnp.int32, (tq, n_cmp), 1) + (CMP_LEN - 1)
    cmask = cmp_end <= t_col
    blk = lax.broadcasted_iota(jnp.int32, (tq, LANES), 1)
    cur = t_col // SLC_BLK
    forced = (blk == 0) | (blk == cur) | (blk == cur - 1)
    future = blk * SLC_BLK > t_col
    for g in range(G):
        qg = q_ref[0, g * R:(g + 1) * R].reshape(R * tq, HEAD_DIM)
        lc_all = _dot_nt(qg, kc_ref[0, g])
        pc_sum = jnp.zeros((tq, n_cmp), F32)
        for r in range(R):
            hh = g * R + r
            lc = jnp.where(cmask, lc_all[r * tq:(r + 1) * tq], NEG)
            e = jnp.where(cmask, jnp.exp(lc - jnp.max(lc, axis=1, keepdims=True)), 0.0)
            pc = e / jnp.maximum(jnp.sum(e, axis=1, keepdims=True), 1e-30)
            pc_sum = pc_sum + pc
            ocw_scr[hh * tq:(hh + 1) * tq, :] = gates[:, 3 * hh:3 * hh + 1] * _dot(pc.astype(BF16), vc_ref[0, g])
        hi = pc_sum.astype(BF16)
        lo = (pc_sum - hi.astype(F32)).astype(BF16)
        imp = _dot(hi, selmap_ref[...]) + _dot(lo, selmap_ref[...])
        imp = jnp.where(forced, 1e9, imp)
        imp = jnp.where(future, NEG, imp)
        imp = jnp.where(blk < n_slc, imp, -jnp.inf)
        sel = jnp.zeros((tq, LANES), F32)
        for _ in range(n_sel):
            best = jnp.max(imp, axis=1, keepdims=True)
            first = jnp.min(jnp.where(imp == best, blk, LANES), axis=1, keepdims=True)
            hit = blk == first
            sel = jnp.where(hit, 1.0, sel)
            imp = jnp.where(hit, -jnp.inf, imp)
        sel_scr[g] = sel

    m_scr[...] = jnp.full(m_scr.shape, NEG, F32)
    l_scr[...] = jnp.zeros(l_scr.shape, F32)
    acc_scr[...] = jnp.zeros(acc_scr.shape, F32)
    nch = (t0 + tq + tk - 1) // tk
    row_t = t0 + lax.broadcasted_iota(jnp.int32, (tq, tk), 0)
    col = lax.broadcasted_iota(jnp.int32, (tq, tk), 1)
    e_row = lax.broadcasted_iota(jnp.int32, (LANES, tk), 0)
    e_col = lax.broadcasted_iota(jnp.int32, (LANES, tk), 1) // SLC_BLK

    def slc_chunk(c, carry):
        s0 = pl.multiple_of(c * tk, tk)
        expand = jnp.where(e_row == e_col + s0 // SLC_BLK, 1.0, 0.0).astype(BF16)
        causal = col + s0 <= row_t
        d0 = (t0 - s0) // LANES
        for g in range(G):
            kch = ks_ref[0, g, pl.ds(s0, tk), :]
            picked = _dot(sel_scr[g].astype(BF16), expand)
            mb = jnp.where((picked > 0.5) & causal, 0.0, NEG)
            qg = q_ref[0, g * R:(g + 1) * R].reshape(R * tq, HEAD_DIM)
            lg_all = _dot_nt(qg, kch)
            for r in range(R):
                hh = g * R + r
                lg = lg_all[r * tq:(r + 1) * tq] + _bias_row(toe_ref, hh, d0, n_sub, nd) + mb
                _online_softmax_head(lg, hh, tq, m_scr, l_scr, acc_scr, p_scr, hh, r)
            rows = slice(g * R * tq, (g + 1) * R * tq)
            acc_scr[rows, :] = acc_scr[rows, :] + _dot(p_scr[...], vs_ref[0, g, pl.ds(s0, tk), :])
        return carry

    lax.fori_loop(0, nch, slc_chunk, 0)

    wlen = WIN + tq
    start = pl.multiple_of(jnp.maximum(t0 - WIN, 0), LANES)
    s_w = start + lax.broadcasted_iota(jnp.int32, (tq, wlen), 1)
    dist = t_col - s_w
    wmask = (dist >= 0) & (dist < WIN)
    dw = (t0 - start) // LANES
    for g in range(G):
        kwc = kw_ref[0, g, pl.ds(start, wlen), :]
        vwc = vw_ref[0, g, pl.ds(start, wlen), :]
        qg = q_ref[0, g * R:(g + 1) * R].reshape(R * tq, HEAD_DIM)
        lw_all = _dot_nt(qg, kwc)
        for r in range(R):
            hh = g * R + r
            lw = lw_all[r * tq:(r + 1) * tq] + _bias_row(toe_ref, hh, dw, wlen // LANES, nd)
            lw = jnp.where(wmask, lw, NEG)
            e = jnp.where(wmask, jnp.exp(lw - jnp.max(lw, axis=1, keepdims=True)), 0.0)
            pw = e / jnp.maximum(jnp.sum(e, axis=1, keepdims=True), 1e-30)
            o_win = _dot(pw.astype(BF16), vwc)
            rows = slice(hh * tq, (hh + 1) * tq)
            o_slc = acc_scr[rows, :] / jnp.maximum(l_scr[hh], 1e-30)
            o = (ocw_scr[rows, :] + gates[:, 3 * hh + 1:3 * hh + 2] * o_slc
                 + gates[:, 3 * hh + 2:3 * hh + 3] * o_win)
            o_ref[0, :, hh * HEAD_DIM:(hh + 1) * HEAD_DIM] = o.astype(o_ref.dtype)


def _selection_map(n_cmp, n_slc):
    c0 = CMP_STRIDE * np.arange(n_cmp)[:, None]
    s0 = SLC_BLK * np.arange(n_slc)[None, :]
    ov = np.clip(np.minimum(c0 + CMP_LEN, s0 + SLC_BLK) - np.maximum(c0, s0), 0, None)
    return (ov / CMP_LEN).astype(np.float32)


def _nsa_attn(q, gates, kc, vc, ks, vs, kw, vw, toe, tq=128, tk=512):
    B, _, T, _ = q.shape
    tk = min(tk, T)
    nd = toe.shape[0]
    n_cmp_pad = kc.shape[2]
    n_cmp = (T - CMP_LEN) // CMP_STRIDE + 1
    n_slc = T // SLC_BLK
    assert n_slc <= LANES and T >= WIN + tq
    selmap = np.zeros((n_cmp_pad, LANES), np.float32)
    selmap[:n_cmp, :n_slc] = _selection_map(n_cmp, n_slc)
    kern = functools.partial(_nsa_attn_kernel, tq=tq, tk=tk, nd=nd, n_slc=n_slc, n_sel=min(N_SLC_MAX, n_slc))
    G = N_KV_GROUPS
    res = lambda n: pl.BlockSpec((1, G, n, HEAD_DIM), lambda b, i: (b, 0, 0, 0), pipeline_mode=pl.Buffered(1))
    return pl.pallas_call(
        kern,
        grid=(B, T // tq),
        in_specs=[
            pl.BlockSpec((1, N_HEADS, tq, HEAD_DIM), lambda b, i: (b, 0, i, 0)),
            pl.BlockSpec((1, tq, 3 * N_HEADS), lambda b, i: (b, i, 0)),
            res(n_cmp_pad), res(n_cmp_pad), res(T), res(T), res(T), res(T),
            _const_spec(selmap.shape),
            _const_spec(toe.shape),
        ],
        out_specs=pl.BlockSpec((1, tq, N_HEADS * HEAD_DIM), lambda b, i: (b, i, 0)),
        out_shape=jax.ShapeDtypeStruct((B, T, N_HEADS * HEAD_DIM), BF16),
        scratch_shapes=[
            pltpu.VMEM((G, tq, LANES), F32),
            pltpu.VMEM((HEADS_PER_GROUP * tq, tk), BF16),
            pltpu.VMEM((N_HEADS, tq, LANES), F32),
            pltpu.VMEM((N_HEADS, tq, LANES), F32),
            pltpu.VMEM((N_HEADS * tq, HEAD_DIM), F32),
            pltpu.VMEM((N_HEADS * tq, HEAD_DIM), F32),
        ],
        compiler_params=_params(("parallel", "arbitrary")),
        name="nsa_attn",
    )(q, gates, kc, vc, ks, vs, kw, vw, jnp.asarray(selmap, BF16), toe)


def _dsa_layer(x, toe, g_attn, w_in, gq, gkv, gk, w_uq, w_q_idx, w_uk, w_uv, w_o):
    qa, qi, ckv, kidx, wi = _dsa_proj(x, g_attn, w_in, gq, gkv, gk, w_uq, w_uk, w_q_idx)
    o_lat = _dsa_attn(qi, wi, kidx, qa, ckv, toe)
    return _dsa_out(o_lat, x, w_uv, w_o)


def _shared_kv(x, g_kv, w_kv, pos_k, pos_v, w1_k, w2_k, w1_v, w2_v):
    kc_raw, vc_raw, ks, vs, kw, vw = _kv_proj(x, g_kv, w_kv)
    return _compress(kc_raw, pos_k, w1_k, w2_k), _compress(vc_raw, pos_v, w1_v, w2_v), ks, vs, kw, vw


def _nsa_layer(x, toe, kv_shared, g_attn, w_in, w_o):
    q, gates = _nsa_proj(x, g_attn, w_in)
    return _out_proj(_nsa_attn(q, gates, *kv_shared, toe), x, w_o)


def kernel(x, g_attn, g_mlp, w_up, w_down, rel_bias, a_w_in, a_g_q_lat, a_g_kv_lat, a_g_k_idx, a_w_uq, a_w_q_idx, a_w_uk, a_w_uv, a_w_o, g_kv_shared, w_kv_shared, cmp_pos_k, cmp_pos_v, cmp_w1_k, cmp_w2_k, cmp_w1_v, cmp_w2_v, b_w_in, b_w_o, g_final):
    depth = g_attn.shape[0]
    n_a = a_w_in.shape[0]
    toe = _bias_tiles(rel_bias, x.shape[1])
    kv_shared = None
    for l in range(depth):
        if l < n_a:
            x = _dsa_layer(x, toe, g_attn[l], a_w_in[l], a_g_q_lat[l], a_g_kv_lat[l], a_g_k_idx[l],
                           a_w_uq[l], a_w_q_idx[l], a_w_uk[l], a_w_uv[l], a_w_o[l])
        else:
            j = l - n_a
            x = _nsa_layer(x, toe, kv_shared, g_attn[l], b_w_in[j], b_w_o[j])
        x = _mlp(x, g_mlp[l], w_up[l], w_down[l], g_final if l == depth - 1 else None)
        if l == n_a - 1:
            kv_shared = _shared_kv(x, g_kv_shared, w_kv_shared, cmp_pos_k, cmp_pos_v,
                                   cmp_w1_k, cmp_w2_k, cmp_w1_v, cmp_w2_v)
    return x
```

```python
import functools
import math

import numpy as np
import jax
import jax.numpy as jnp
from jax import lax
from jax.experimental import pallas as pl
from jax.experimental.pallas import tpu as pltpu

N_HEADS = 8
HEAD_DIM = 128
Q_LORA = 256
KV_LORA = 128
IDX_HEADS = 8
IDX_DIM = 64
IDX_TOPK_MAX = 256
N_KV_GROUPS = 2
HEADS_PER_GROUP = N_HEADS // N_KV_GROUPS
CMP_LEN = 32
CMP_STRIDE = 16
CMP_HID = 256
SLC_BLK = 64
N_SLC_MAX = 16
WIN = 512
REL_BUCKETS = 32
REL_MAX_DIST = 4096
EPS = 1e-6
NEG = -1e30

LANES = 128
INT_MIN = np.int32(-2 ** 31)
VALUE_STEPS = 16
VMEM_LIMIT = 56 * 1024 * 1024

F32 = jnp.float32
BF16 = jnp.bfloat16
NT_DIMS = (((1,), (1,)), ((), ()))


def _dot(a, b):
    return jnp.dot(a, b, preferred_element_type=F32)


def _dot_nt(a, b):
    return lax.dot_general(a, b, NT_DIMS, preferred_element_type=F32)


def _rms(x, g):
    return x * lax.rsqrt(jnp.mean(x * x, axis=-1, keepdims=True) + EPS) * g


def _to_key(v):
    bits = lax.bitcast_convert_type(v, jnp.int32)
    return bits ^ ((bits >> 31) & jnp.int32(0x7FFFFFFF))


def _from_key(k):
    return lax.bitcast_convert_type(k ^ ((k >> 31) & jnp.int32(0x7FFFFFFF)), F32)


def _const_spec(shape):
    nd = len(shape)
    return pl.BlockSpec(shape, lambda *_: (0,) * nd, pipeline_mode=pl.Buffered(1))


def _params(sem):
    return pltpu.CompilerParams(dimension_semantics=sem, vmem_limit_bytes=VMEM_LIMIT)


def _rel_bucket(dist):
    dist = jnp.maximum(dist, 0)
    exact = REL_BUCKETS // 2
    log_ratio = jnp.log(jnp.maximum(dist, 1).astype(F32) / exact) / math.log(REL_MAX_DIST / exact)
    large = exact + (log_ratio * (REL_BUCKETS - exact)).astype(jnp.int32)
    return jnp.where(dist < exact, dist, jnp.minimum(large, REL_BUCKETS - 1))


def _num_bias_tiles(T):
    exact = REL_BUCKETS // 2
    switch = exact * (REL_MAX_DIST / exact) ** ((REL_BUCKETS - exact - 1) / (REL_BUCKETS - exact))
    far = int(math.ceil(switch)) + 32
    return min(T // LANES, -(-(far + LANES - 1) // LANES) + 1)


def _bias_tiles(rel_bias, T):
    nd = _num_bias_tiles(T)
    c = LANES * (nd - 1)
    bd = rel_bias[_rel_bucket(jnp.arange(c + LANES, dtype=jnp.int32))].T.astype(F32)
    p = c + 2 * LANES
    w = jnp.concatenate([bd[:, c::-1], jnp.broadcast_to(bd[:, :1], (N_HEADS, LANES)), bd[:, :c:-1]], axis=1)
    band = jnp.tile(w, (1, LANES))[:, :LANES * (p - 1)].reshape(N_HEADS, LANES, p - 1)[:, :, :c + LANES]
    tiles = band.reshape(N_HEADS, LANES, nd, LANES)[:, :, ::-1]
    return jnp.transpose(tiles, (2, 0, 1, 3))


def _dsa_proj_kernel(x_ref, g_ref, w_in_ref, gq_ref, gkv_ref, gk_ref, w_uq_ref, w_ukt_ref, w_qi_ref,
                     qa_ref, qi_ref, ckv_ref, kidx_ref, wi_ref):
    h = _rms(x_ref[0], g_ref[...]).astype(BF16)
    proj = _dot(h, w_in_ref[...])
    c_q = _rms(proj[:, :Q_LORA], gq_ref[...]).astype(BF16)
    ckv_ref[0] = _rms(proj[:, Q_LORA:Q_LORA + KV_LORA], gkv_ref[...]).astype(BF16)
    o_k = Q_LORA + KV_LORA
    kidx_ref[0] = _rms(proj[:, o_k:o_k + IDX_DIM], gk_ref[...]).astype(BF16)
    o_w = o_k + LANES
    wi_ref[0] = proj[:, o_w:o_w + IDX_HEADS] * (IDX_HEADS ** -0.5 * IDX_DIM ** -0.5)
    q = _dot(c_q, w_uq_ref[...])
    for hh in range(N_HEADS):
        qh = q[:, hh * HEAD_DIM:(hh + 1) * HEAD_DIM].astype(BF16)
        qa_ref[0, hh] = (_dot(qh, w_ukt_ref[hh]) * HEAD_DIM ** -0.5).astype(BF16)
        qi_ref[0, hh] = _dot(c_q, w_qi_ref[hh]).astype(BF16)


def _dsa_proj(x, g, w_in, gq, gkv, gk, w_uq, w_uk, w_q_idx, tm=256):
    B, T, D = x.shape
    n_in = Q_LORA + KV_LORA + 2 * LANES
    w_in_p = jnp.zeros((D, n_in), F32)
    o_k = Q_LORA + KV_LORA
    w_in_p = w_in_p.at[:, :o_k + IDX_DIM].set(w_in[:, :o_k + IDX_DIM])
    w_in_p = w_in_p.at[:, o_k + LANES:o_k + LANES + IDX_HEADS].set(w_in[:, o_k + IDX_DIM:])
    w_ukt = jnp.transpose(w_uk, (1, 2, 0)).astype(BF16)
    w_qi = jnp.transpose(w_q_idx, (1, 0, 2)).astype(BF16)
    tile = lambda b, i: (b, i, 0)
    htile = lambda b, i: (b, 0, i, 0)
    return pl.pallas_call(
        _dsa_proj_kernel,
        grid=(B, T // tm),
        in_specs=[
            pl.BlockSpec((1, tm, D), tile),
            _const_spec((1, D)),
            _const_spec((D, n_in)),
            _const_spec((1, Q_LORA)), _const_spec((1, KV_LORA)), _const_spec((1, IDX_DIM)),
            _const_spec((Q_LORA, N_HEADS * HEAD_DIM)),
            _const_spec((N_HEADS, HEAD_DIM, KV_LORA)),
            _const_spec((IDX_HEADS, Q_LORA, IDX_DIM)),
        ],
        out_specs=[
            pl.BlockSpec((1, N_HEADS, tm, KV_LORA), htile),
            pl.BlockSpec((1, IDX_HEADS, tm, IDX_DIM), htile),
            pl.BlockSpec((1, tm, KV_LORA), tile),
            pl.BlockSpec((1, tm, IDX_DIM), tile),
            pl.BlockSpec((1, tm, IDX_HEADS), tile),
        ],
        out_shape=[
            jax.ShapeDtypeStruct((B, N_HEADS, T, KV_LORA), BF16),
            jax.ShapeDtypeStruct((B, IDX_HEADS, T, IDX_DIM), BF16),
            jax.ShapeDtypeStruct((B, T, KV_LORA), BF16),
            jax.ShapeDtypeStruct((B, T, IDX_DIM), BF16),
            jax.ShapeDtypeStruct((B, T, IDX_HEADS), F32),
        ],
        compiler_params=_params(("parallel", "parallel")),
        name="dsa_proj",
    )(x, g.reshape(1, D), w_in_p.astype(BF16), gq.reshape(1, -1), gkv.reshape(1, -1), gk.reshape(1, -1),
      w_uq.reshape(Q_LORA, N_HEADS * HEAD_DIM).astype(BF16), w_ukt, w_qi)


def _bias_row(toe_ref, head, d0, n_sub, nd):
    parts = [toe_ref[jnp.clip(d0 - k, 0, nd - 1), head] for k in range(n_sub)]
    return parts[0] if n_sub == 1 else jnp.concatenate(parts, axis=1)


def _tile_lanes(a, n):
    return a if n == 1 else jnp.concatenate([a] * n, axis=1)


def _online_softmax_head(lg, row, tq, m_scr, l_scr, acc_scr, p_scr, head, prow):
    n_rep = lg.shape[1] // LANES
    m_old = m_scr[head]
    m_new = jnp.maximum(m_old, jnp.max(lg, axis=1, keepdims=True))
    alpha = jnp.exp(m_old - m_new)
    p = jnp.exp(lg - _tile_lanes(m_new, n_rep))
    l_scr[head] = alpha * l_scr[head] + jnp.sum(p, axis=1, keepdims=True)
    m_scr[head] = m_new
    p_scr[prow * tq:(prow + 1) * tq, :] = p.astype(BF16)
    acc_scr[row * tq:(row + 1) * tq, :] = acc_scr[row * tq:(row + 1) * tq, :] * alpha


def _dsa_attn_kernel(qi_ref, wi_ref, kidx_ref, qa_ref, ckv_ref, toe_ref, o_ref,
                     s_scr, p_scr, m_scr, l_scr, acc_scr, *, tq, tk, top_k, nd, idx_bits):
    t0 = pl.program_id(1) * tq
    nch = (t0 + tq + tk - 1) // tk
    n_sub = tk // LANES
    row_t = t0 + lax.broadcasted_iota(jnp.int32, (tq, tk), 0)
    col = lax.broadcasted_iota(jnp.int32, (tq, tk), 1)
    lane = lax.broadcasted_iota(jnp.int32, (tq, LANES), 1)

    wi = wi_ref[0]

    def score_chunk(c, carry):
        smax, smin = carry
        s0 = pl.multiple_of(c * tk, tk)
        kc = kidx_ref[0, pl.ds(s0, tk), :]
        acc = jnp.zeros((tq, tk), F32)
        for hh in range(IDX_HEADS):
            rel = _dot_nt(qi_ref[0, hh], kc)
            acc = acc + wi[:, hh:hh + 1] * jnp.maximum(rel, 0.0)
        acc = acc + 0.0
        valid = col + s0 <= row_t
        s_scr[c] = jnp.where(valid, _to_key(acc), INT_MIN)
        top = jnp.where(valid, acc, -jnp.inf)
        bot = jnp.where(valid, acc, jnp.inf)
        for j in range(n_sub):
            smax = jnp.maximum(smax, top[:, j * LANES:(j + 1) * LANES])
            smin = jnp.minimum(smin, bot[:, j * LANES:(j + 1) * LANES])
        return smax, smin

    smax, smin = lax.fori_loop(0, nch, score_chunk, (jnp.full((tq, LANES), -jnp.inf, F32),
                                                     jnp.full((tq, LANES), jnp.inf, F32)))
    smax = jnp.max(smax, axis=1, keepdims=True)
    smin = jnp.min(smin, axis=1, keepdims=True)

    def count(pred):
        def body(c, cnt):
            for j in range(n_sub):
                blk = s_scr[c, :, j * LANES:(j + 1) * LANES]
                cnt = cnt + jnp.where(pred(blk, lane + (c * tk + j * LANES)), 1.0, 0.0)
            return cnt
        cnt = lax.fori_loop(0, nch, body, jnp.zeros((tq, LANES), F32))
        return jnp.sum(cnt, axis=1, keepdims=True)

    kf = jnp.float32(top_k)

    def remaining(lo, hi, n_lo):
        return jnp.sum(jnp.where((n_lo <= kf) | (hi - 1 <= lo), 0.0, 1.0))

    def bisect_cond(st):
        return (st[0] < VALUE_STEPS + 33) & (st[1] > 0.0)

    def bisect_step(st):
        it, _, lo, hi, n_lo = st
        done = (n_lo <= kf) | (hi - 1 <= lo)
        lo_v, hi_v = _from_key(lo), _from_key(hi)
        mid_v = _to_key(lo_v + (hi_v - lo_v) * 0.5 + 0.0)
        mid_k = (lo >> 1) + (hi >> 1) + (lo & hi & 1)
        cand = jnp.where((mid_v <= lo) | (mid_v >= hi) | (it >= VALUE_STEPS), mid_k, mid_v)
        cand = jnp.where(done, lo, cand)
        cnt = count(lambda blk, idx: blk >= cand)
        up = (cnt >= kf) & ~done
        lo = jnp.where(up, cand, lo)
        hi = jnp.where(up | done, hi, cand)
        n_lo = jnp.where(up[:, :1], cnt, n_lo)
        return it + 1, remaining(lo, hi, n_lo), lo, hi, n_lo

    lo0 = jnp.broadcast_to(_to_key(smin), (tq, LANES))
    hi0 = jnp.broadcast_to(_to_key(smax), (tq, LANES)) + 1
    n0 = (t0 + 1 + lax.broadcasted_iota(jnp.int32, (tq, 1), 0)).astype(F32)
    _, _, thr, _, n_ge = lax.while_loop(bisect_cond, bisect_step,
                                        (jnp.int32(0), remaining(lo0, hi0, n0), lo0, hi0, n0))

    @pl.when(jnp.max(n_ge) > kf)
    def _():
        need = kf - count(lambda blk, idx: blk > thr)

        def idx_bit(bi, last):
            cand = last | lax.shift_left(jnp.int32(1), idx_bits - 1 - bi)
            cnt = count(lambda blk, idx: (blk == thr) & (idx < cand))
            return jnp.where(cnt < need, cand, last)

        last = lax.fori_loop(0, idx_bits, idx_bit, jnp.zeros((tq, LANES), jnp.int32))

        def drop(c, carry):
            for j in range(n_sub):
                sl = slice(j * LANES, (j + 1) * LANES)
                blk = s_scr[c, :, sl]
                idx = lane + (c * tk + j * LANES)
                s_scr[c, :, sl] = jnp.where((blk == thr) & (idx > last), INT_MIN, blk)
            return carry

        lax.fori_loop(0, nch, drop, 0)

    m_scr[...] = jnp.full(m_scr.shape, NEG, F32)
    l_scr[...] = jnp.zeros(l_scr.shape, F32)
    acc_scr[...] = jnp.zeros(acc_scr.shape, F32)
    thr_full = _tile_lanes(thr, n_sub)
    qa = qa_ref[0].reshape(N_HEADS * tq, KV_LORA)

    def attn_chunk(c, carry):
        s0 = pl.multiple_of(c * tk, tk)
        kv = ckv_ref[0, pl.ds(s0, tk), :]
        mb = jnp.where(s_scr[c] >= thr_full, 0.0, NEG)
        lg_all = _dot_nt(qa, kv)
        d0 = (t0 - s0) // LANES
        for hh in range(N_HEADS):
            lg = lg_all[hh * tq:(hh + 1) * tq] + _bias_row(toe_ref, hh, d0, n_sub, nd) + mb
            _online_softmax_head(lg, hh, tq, m_scr, l_scr, acc_scr, p_scr, hh, hh)
        acc_scr[...] = acc_scr[...] + _dot(p_scr[...], kv)
        return carry

    lax.fori_loop(0, nch, attn_chunk, 0)
    for hh in range(N_HEADS):
        o = acc_scr[hh * tq:(hh + 1) * tq, :] / l_scr[hh]
        o_ref[0, :, hh * KV_LORA:(hh + 1) * KV_LORA] = o.astype(o_ref.dtype)


def _dsa_attn(qi, wi, kidx, qa, ckv, toe, tq=128, tk=512):
    B, _, T, _ = qa.shape
    tk = min(tk, T)
    nd = toe.shape[0]
    top_k = min(IDX_TOPK_MAX, T // 4)
    kern = functools.partial(_dsa_attn_kernel, tq=tq, tk=tk, top_k=top_k, nd=nd,
                             idx_bits=max(1, (T - 1).bit_length()))
    return pl.pallas_call(
        kern,
        grid=(B, T // tq),
        in_specs=[
            pl.BlockSpec((1, IDX_HEADS, tq, IDX_DIM), lambda b, i: (b, 0, i, 0)),
            pl.BlockSpec((1, tq, IDX_HEADS), lambda b, i: (b, i, 0)),
            pl.BlockSpec((1, T, IDX_DIM), lambda b, i: (b, 0, 0), pipeline_mode=pl.Buffered(1)),
            pl.BlockSpec((1, N_HEADS, tq, KV_LORA), lambda b, i: (b, 0, i, 0)),
            pl.BlockSpec((1, T, KV_LORA), lambda b, i: (b, 0, 0), pipeline_mode=pl.Buffered(1)),
            _const_spec(toe.shape),
        ],
        out_specs=pl.BlockSpec((1, tq, N_HEADS * KV_LORA), lambda b, i: (b, i, 0)),
        out_shape=jax.ShapeDtypeStruct((B, T, N_HEADS * KV_LORA), BF16),
        scratch_shapes=[
            pltpu.VMEM((T // tk, tq, tk), jnp.int32),
            pltpu.VMEM((N_HEADS * tq, tk), BF16),
            pltpu.VMEM((N_HEADS, tq, LANES), F32),
            pltpu.VMEM((N_HEADS, tq, LANES), F32),
            pltpu.VMEM((N_HEADS * tq, KV_LORA), F32),
        ],
        compiler_params=_params(("parallel", "arbitrary")),
        name="dsa_attn",
    )(qi, wi, kidx, qa, ckv, toe)


def _dsa_out_kernel(o_ref, x_ref, w_uv_ref, w_o_ref, y_ref):
    o_lat = o_ref[0]
    parts = [_dot(o_lat[:, hh * KV_LORA:(hh + 1) * KV_LORA], w_uv_ref[hh]).astype(BF16)
             for hh in range(N_HEADS)]
    y_ref[0] = x_ref[0] + _dot(jnp.concatenate(parts, axis=1), w_o_ref[...])


def _dsa_out(o_lat, x, w_uv, w_o, tm=512):
    B, T, D = x.shape
    tile = lambda b, i: (b, i, 0)
    return pl.pallas_call(
        _dsa_out_kernel,
        grid=(B, T // tm),
        in_specs=[pl.BlockSpec((1, tm, N_HEADS * KV_LORA), tile), pl.BlockSpec((1, tm, D), tile),
                  _const_spec((N_HEADS, KV_LORA, HEAD_DIM)), _const_spec((N_HEADS * HEAD_DIM, D))],
        out_specs=pl.BlockSpec((1, tm, D), tile),
        out_shape=jax.ShapeDtypeStruct((B, T, D), F32),
        compiler_params=_params(("parallel", "parallel")),
        name="dsa_out",
    )(o_lat, x, jnp.transpose(w_uv, (1, 0, 2)).astype(BF16), w_o.astype(BF16))


def _out_proj_kernel(o_ref, x_ref, w_o_ref, y_ref):
    y_ref[0] = x_ref[0] + _dot(o_ref[0], w_o_ref[...])


def _out_proj(o, x, w_o, tm=512):
    B, T, D = x.shape
    tile = lambda b, i: (b, i, 0)
    return pl.pallas_call(
        _out_proj_kernel,
        grid=(B, T // tm),
        in_specs=[pl.BlockSpec((1, tm, o.shape[-1]), tile), pl.BlockSpec((1, tm, D), tile),
                  _const_spec(w_o.shape)],
        out_specs=pl.BlockSpec((1, tm, D), tile),
        out_shape=jax.ShapeDtypeStruct((B, T, D), F32),
        compiler_params=_params(("parallel", "parallel")),
        name="out_proj",
    )(o, x, w_o.astype(BF16))


def _mlp_kernel(x_ref, g_ref, w_up_ref, w_down_ref, *rest, tf, final):
    y_ref = rest[-1]
    x = x_ref[0]
    h = _rms(x, g_ref[...]).astype(BF16)
    acc = x
    for f0 in range(0, w_up_ref.shape[1], tf):
        u = jnp.maximum(_dot(h, w_up_ref[:, f0:f0 + tf]), 0.0)
        acc = acc + _dot((u * u).astype(BF16), w_down_ref[f0:f0 + tf, :])
    y_ref[0] = _rms(acc, rest[0][...]) if final else acc


def _mlp(x, g, w_up, w_down, g_final=None, tm=512, tf=512):
    B, T, D = x.shape
    F = w_up.shape[1]
    tile = lambda b, i: (b, i, 0)
    final = g_final is not None
    in_specs = [pl.BlockSpec((1, tm, D), tile), _const_spec((1, D)), _const_spec((D, F)), _const_spec((F, D))]
    args = [x, g.reshape(1, D), w_up.astype(BF16), w_down.astype(BF16)]
    if final:
        in_specs.append(_const_spec((1, D)))
        args.append(g_final.reshape(1, D))
    return pl.pallas_call(
        functools.partial(_mlp_kernel, tf=tf, final=final),
        grid=(B, T // tm),
        in_specs=in_specs,
        out_specs=pl.BlockSpec((1, tm, D), tile),
        out_shape=jax.ShapeDtypeStruct((B, T, D), F32),
        compiler_params=_params(("parallel", "parallel")),
        name="mlp_final" if final else "mlp",
    )(*args)


def _kv_proj_kernel(x_ref, g_ref, w_ref, kc_ref, vc_ref, ks_ref, vs_ref, kw_ref, vw_ref):
    h = _rms(x_ref[0], g_ref[...]).astype(BF16)
    kv = _dot(h, w_ref[...])
    for part, ref in enumerate((kc_ref, vc_ref, ks_ref, vs_ref, kw_ref, vw_ref)):
        for g in range(N_KV_GROUPS):
            o = (part * N_KV_GROUPS + g) * HEAD_DIM
            ref[0, g] = kv[:, o:o + HEAD_DIM].astype(ref.dtype)


def _kv_proj(x, g, w_kv, tm=512):
    B, T, D = x.shape
    gtile = lambda b, i: (b, 0, i, 0)
    spec = pl.BlockSpec((1, N_KV_GROUPS, tm, HEAD_DIM), gtile)
    shp = lambda dt: jax.ShapeDtypeStruct((B, N_KV_GROUPS, T, HEAD_DIM), dt)
    return pl.pallas_call(
        _kv_proj_kernel,
        grid=(B, T // tm),
        in_specs=[pl.BlockSpec((1, tm, D), lambda b, i: (b, i, 0)), _const_spec((1, D)),
                  _const_spec(w_kv.shape)],
        out_specs=[spec] * 6,
        out_shape=[shp(F32), shp(F32), shp(BF16), shp(BF16), shp(BF16), shp(BF16)],
        compiler_params=_params(("parallel", "parallel")),
        name="kv_proj",
    )(x, g.reshape(1, D), w_kv.astype(BF16))


def _compress_kernel(raw_ref, pos_ref, w1_ref, w2_ref, o_ref):
    rows = raw_ref[0, 0]
    half = rows.shape[1]
    a = _dot((rows + pos_ref[:, :half]).astype(BF16), w1_ref[:half, :])
    b = _dot((rows + pos_ref[:, half:]).astype(BF16), w1_ref[half:, :])
    pre = a + pltpu.roll(b, rows.shape[0] - 1, 0)
    act = 0.5 * pre * (1.0 + jnp.tanh(math.sqrt(2.0 / math.pi) * (pre + 0.044715 * (pre * pre * pre))))
    o_ref[0, 0] = _dot(act.astype(BF16), w2_ref[...]).astype(o_ref.dtype)


def _compress(raw, pos, w1, w2):
    B, G, T, Dh = raw.shape
    nr = T // CMP_STRIDE
    rows = raw.reshape(B, G, nr, CMP_STRIDE * Dh)
    return pl.pallas_call(
        _compress_kernel,
        grid=(B, G),
        in_specs=[pl.BlockSpec((1, 1, nr, CMP_STRIDE * Dh), lambda b, g: (b, g, 0, 0)),
                  _const_spec((1, CMP_LEN * Dh)), _const_spec(w1.shape), _const_spec(w2.shape)],
        out_specs=pl.BlockSpec((1, 1, nr, Dh), lambda b, g: (b, g, 0, 0)),
        out_shape=jax.ShapeDtypeStruct((B, G, nr, Dh), BF16),
        compiler_params=_params(("parallel", "parallel")),
        name="compress",
    )(rows, pos.reshape(1, CMP_LEN * Dh), w1.astype(BF16), w2.astype(BF16))


def _nsa_proj_kernel(x_ref, g_ref, wq_ref, wg_ref, q_ref, gate_ref):
    h = _rms(x_ref[0], g_ref[...]).astype(BF16)
    q = _dot(h, wq_ref[...]) * HEAD_DIM ** -0.5
    for hh in range(N_HEADS):
        q_ref[0, hh] = q[:, hh * HEAD_DIM:(hh + 1) * HEAD_DIM].astype(BF16)
    gate_ref[0] = jax.nn.sigmoid(_dot(h, wg_ref[...]))[:, :3 * N_HEADS]


def _nsa_proj(x, g, w_in, tm=512):
    B, T, D = x.shape
    hd = N_HEADS * HEAD_DIM
    wq = w_in[:, :hd].astype(BF16)
    wg = jnp.zeros((D, LANES), F32).at[:, :3 * N_HEADS].set(w_in[:, hd:]).astype(BF16)
    return pl.pallas_call(
        _nsa_proj_kernel,
        grid=(B, T // tm),
        in_specs=[pl.BlockSpec((1, tm, D), lambda b, i: (b, i, 0)), _const_spec((1, D)),
                  _const_spec((D, hd)), _const_spec((D, LANES))],
        out_specs=[pl.BlockSpec((1, N_HEADS, tm, HEAD_DIM), lambda b, i: (b, 0, i, 0)),
                   pl.BlockSpec((1, tm, 3 * N_HEADS), lambda b, i: (b, i, 0))],
        out_shape=[jax.ShapeDtypeStruct((B, N_HEADS, T, HEAD_DIM), BF16),
                   jax.ShapeDtypeStruct((B, T, 3 * N_HEADS), F32)],
        compiler_params=_params(("parallel", "parallel")),
        name="nsa_proj",
    )(x, g.reshape(1, D), wq, wg)


def _nsa_attn_kernel(q_ref, gate_ref, kc_ref, vc_ref, ks_ref, vs_ref, kw_ref, vw_ref, selmap_ref, toe_ref,
                     o_ref, sel_scr, p_scr, m_scr, l_scr, acc_scr, ocw_scr, *, tq, tk, nd, n_slc, n_sel):
    G, R = N_KV_GROUPS, HEADS_PER_GROUP
    t0 = pl.program_id(1) * tq
    n_cmp = kc_ref.shape[2]
    n_sub = tk // LANES
    gates = gate_ref[0]
    t_col = t0 + lax.broadcasted_iota(jnp.int32, (tq, 1), 0)

    cmp_end = CMP_STRIDE * lax.broadcasted_iota(jnp.int32, (tq, n_cmp), 1) + (CMP_LEN - 1)
    cmask = cmp_end <= t_col
    blk = lax.broadcasted_iota(jnp.int32, (tq, LANES), 1)
    cur = t_col // SLC_BLK
    forced = (blk == 0) | (blk == cur) | (blk == cur - 1)
    future = blk * SLC_BLK > t_col
    for g in range(G):
        qg = q_ref[0, g * R:(g + 1) * R].reshape(R * tq, HEAD_DIM)
        lc_all = _dot_nt(qg, kc_ref[0, g])
        pc_sum = jnp.zeros((tq, n_cmp), F32)
        for r in range(R):
            hh = g * R + r
            lc = jnp.where(cmask, lc_all[r * tq:(r + 1) * tq], NEG)
            e = jnp.where(cmask, jnp.exp(lc - jnp.max(lc, axis=1, keepdims=True)), 0.0)
            pc = e / jnp.maximum(jnp.sum(e, axis=1, keepdims=True), 1e-30)
            pc_sum = pc_sum + pc
            ocw_scr[hh * tq:(hh + 1) * tq, :] = gates[:, 3 * hh:3 * hh + 1] * _dot(pc.astype(BF16), vc_ref[0, g])
        hi = pc_sum.astype(BF16)
        lo = (pc_sum - hi.astype(F32)).astype(BF16)
        imp = _dot(hi, selmap_ref[...]) + _dot(lo, selmap_ref[...])
        imp = jnp.where(forced, 1e9, imp)
        imp = jnp.where(future, NEG, imp)
        imp = jnp.where(blk < n_slc, imp, -jnp.inf)
        sel = jnp.zeros((tq, LANES), F32)
        for _ in range(n_sel):
            hit = blk == jnp.argmax(imp, axis=1, keepdims=True).astype(jnp.int32)
            sel = jnp.where(hit, 1.0, sel)
            imp = jnp.where(hit, -jnp.inf, imp)
        sel_scr[g] = sel

    m_scr[...] = jnp.full(m_scr.shape, NEG, F32)
    l_scr[...] = jnp.zeros(l_scr.shape, F32)
    acc_scr[...] = jnp.zeros(acc_scr.shape, F32)
    nch = (t0 + tq + tk - 1) // tk
    row_t = t0 + lax.broadcasted_iota(jnp.int32, (tq, tk), 0)
    col = lax.broadcasted_iota(jnp.int32, (tq, tk), 1)
    e_row = lax.broadcasted_iota(jnp.int32, (LANES, tk), 0)
    e_col = lax.broadcasted_iota(jnp.int32, (LANES, tk), 1) // SLC_BLK

    def slc_chunk(c, carry):
        s0 = pl.multiple_of(c * tk, tk)
        expand = jnp.where(e_row == e_col + s0 // SLC_BLK, 1.0, 0.0).astype(BF16)
        causal = col + s0 <= row_t
        d0 = (t0 - s0) // LANES
        for g in range(G):
            kch = ks_ref[0, g, pl.ds(s0, tk), :]
            picked = _dot(sel_scr[g].astype(BF16), expand)
            mb = jnp.where((picked > 0.5) & causal, 0.0, NEG)
            qg = q_ref[0, g * R:(g + 1) * R].reshape(R * tq, HEAD_DIM)
            lg_all = _dot_nt(qg, kch)
            for r in range(R):
                hh = g * R + r
                lg = lg_all[r * tq:(r + 1) * tq] + _bias_row(toe_ref, hh, d0, n_sub, nd) + mb
                _online_softmax_head(lg, hh, tq, m_scr, l_scr, acc_scr, p_scr, hh, r)
            rows = slice(g * R * tq, (g + 1) * R * tq)
            acc_scr[rows, :] = acc_scr[rows, :] + _dot(p_scr[...], vs_ref[0, g, pl.ds(s0, tk), :])
        return carry

    lax.fori_loop(0, nch, slc_chunk, 0)

    wlen = WIN + tq
    start = pl.multiple_of(jnp.maximum(t0 - WIN, 0), LANES)
    s_w = start + lax.broadcasted_iota(jnp.int32, (tq, wlen), 1)
    dist = t_col - s_w
    wmask = (dist >= 0) & (dist < WIN)
    dw = (t0 - start) // LANES
    for g in range(G):
        kwc = kw_ref[0, g, pl.ds(start, wlen), :]
        vwc = vw_ref[0, g, pl.ds(start, wlen), :]
        qg = q_ref[0, g * R:(g + 1) * R].reshape(R * tq, HEAD_DIM)
        lw_all = _dot_nt(qg, kwc)
        for r in range(R):
            hh = g * R + r
            lw = lw_all[r * tq:(r + 1) * tq] + _bias_row(toe_ref, hh, dw, wlen // LANES, nd)
            lw = jnp.where(wmask, lw, NEG)
            e = jnp.where(wmask, jnp.exp(lw - jnp.max(lw, axis=1, keepdims=True)), 0.0)
            pw = e / jnp.maximum(jnp.sum(e, axis=1, keepdims=True), 1e-30)
            o_win = _dot(pw.astype(BF16), vwc)
            rows = slice(hh * tq, (hh + 1) * tq)
            o_slc = acc_scr[rows, :] / jnp.maximum(l_scr[hh], 1e-30)
            o = (ocw_scr[rows, :] + gates[:, 3 * hh + 1:3 * hh + 2] * o_slc
                 + gates[:, 3 * hh + 2:3 * hh + 3] * o_win)
            o_ref[0, :, hh * HEAD_DIM:(hh + 1) * HEAD_DIM] = o.astype(o_ref.dtype)


def _selection_map(n_cmp, n_slc):
    c0 = CMP_STRIDE * np.arange(n_cmp)[:, None]
    s0 = SLC_BLK * np.arange(n_slc)[None, :]
    ov = np.clip(np.minimum(c0 + CMP_LEN, s0 + SLC_BLK) - np.maximum(c0, s0), 0, None)
    return (ov / CMP_LEN).astype(np.float32)


def _nsa_attn(q, gates, kc, vc, ks, vs, kw, vw, toe, tq=128, tk=512):
    B, _, T, _ = q.shape
    tk = min(tk, T)
    nd = toe.shape[0]
    n_cmp_pad = kc.shape[2]
    n_cmp = (T - CMP_LEN) // CMP_STRIDE + 1
    n_slc = T // SLC_BLK
    assert n_slc <= LANES and T >= WIN + tq
    selmap = np.zeros((n_cmp_pad, LANES), np.float32)
    selmap[:n_cmp, :n_slc] = _selection_map(n_cmp, n_slc)
    kern = functools.partial(_nsa_attn_kernel, tq=tq, tk=tk, nd=nd, n_slc=n_slc, n_sel=min(N_SLC_MAX, n_slc))
    G = N_KV_GROUPS
    res = lambda n: pl.BlockSpec((1, G, n, HEAD_DIM), lambda b, i: (b, 0, 0, 0), pipeline_mode=pl.Buffered(1))
    return pl.pallas_call(
        kern,
        grid=(B, T // tq),
        in_specs=[
            pl.BlockSpec((1, N_HEADS, tq, HEAD_DIM), lambda b, i: (b, 0, i, 0)),
            pl.BlockSpec((1, tq, 3 * N_HEADS), lambda b, i: (b, i, 0)),
            res(n_cmp_pad), res(n_cmp_pad), res(T), res(T), res(T), res(T),
            _const_spec(selmap.shape),
            _const_spec(toe.shape),
        ],
        out_specs=pl.BlockSpec((1, tq, N_HEADS * HEAD_DIM), lambda b, i: (b, i, 0)),
        out_shape=jax.ShapeDtypeStruct((B, T, N_HEADS * HEAD_DIM), BF16),
        scratch_shapes=[
            pltpu.VMEM((G, tq, LANES), F32),
            pltpu.VMEM((HEADS_PER_GROUP * tq, tk), BF16),
            pltpu.VMEM((N_HEADS, tq, LANES), F32),
            pltpu.VMEM((N_HEADS, tq, LANES), F32),
            pltpu.VMEM((N_HEADS * tq, HEAD_DIM), F32),
            pltpu.VMEM((N_HEADS * tq, HEAD_DIM), F32),
        ],
        compiler_params=_params(("parallel", "arbitrary")),
        name="nsa_attn",
    )(q, gates, kc, vc, ks, vs, kw, vw, jnp.asarray(selmap, BF16), toe)


def _dsa_layer(x, toe, g_attn, w_in, gq, gkv, gk, w_uq, w_q_idx, w_uk, w_uv, w_o):
    qa, qi, ckv, kidx, wi = _dsa_proj(x, g_attn, w_in, gq, gkv, gk, w_uq, w_uk, w_q_idx)
    o_lat = _dsa_attn(qi, wi, kidx, qa, ckv, toe)
    return _dsa_out(o_lat, x, w_uv, w_o)


def _shared_kv(x, g_kv, w_kv, pos_k, pos_v, w1_k, w2_k, w1_v, w2_v):
    kc_raw, vc_raw, ks, vs, kw, vw = _kv_proj(x, g_kv, w_kv)
    return _compress(kc_raw, pos_k, w1_k, w2_k), _compress(vc_raw, pos_v, w1_v, w2_v), ks, vs, kw, vw


def _nsa_layer(x, toe, kv_shared, g_attn, w_in, w_o):
    q, gates = _nsa_proj(x, g_attn, w_in)
    return _out_proj(_nsa_attn(q, gates, *kv_shared, toe), x, w_o)


def kernel(x, g_attn, g_mlp, w_up, w_down, rel_bias, a_w_in, a_g_q_lat, a_g_kv_lat, a_g_k_idx, a_w_uq, a_w_q_idx, a_w_uk, a_w_uv, a_w_o, g_kv_shared, w_kv_shared, cmp_pos_k, cmp_pos_v, cmp_w1_k, cmp_w2_k, cmp_w1_v, cmp_w2_v, b_w_in, b_w_o, g_final):
    depth = g_attn.shape[0]
    n_a = a_w_in.shape[0]
    toe = _bias_tiles(rel_bias, x.shape[1])
    kv_shared = None
    for l in range(depth):
        if l < n_a:
            x = _dsa_layer(x, toe, g_attn[l], a_w_in[l], a_g_q_lat[l], a_g_kv_lat[l], a_g_k_idx[l],
                           a_w_uq[l], a_w_q_idx[l], a_w_uk[l], a_w_uv[l], a_w_o[l])
        else:
            j = l - n_a
            x = _nsa_layer(x, toe, kv_shared, g_attn[l], b_w_in[j], b_w_o[j])
        x = _mlp(x, g_mlp[l], w_up[l], w_down[l], g_final if l == depth - 1 else None)
        if l == n_a - 1:
            kv_shared = _shared_kv(x, g_kv_shared, w_kv_shared, cmp_pos_k, cmp_pos_v,
                                   cmp_w1_k, cmp_w2_k, cmp_w1_v, cmp_w2_v)
    return x
```

```python
import functools
import math

import numpy as np
import jax
import jax.numpy as jnp
from jax import lax
from jax.experimental import pallas as pl
from jax.experimental.pallas import tpu as pltpu

N_HEADS = 8
HEAD_DIM = 128
Q_LORA = 256
KV_LORA = 128
IDX_HEADS = 8
IDX_DIM = 64
IDX_TOPK_MAX = 256
N_KV_GROUPS = 2
HEADS_PER_GROUP = N_HEADS // N_KV_GROUPS
CMP_LEN = 32
CMP_STRIDE = 16
CMP_HID = 256
SLC_BLK = 64
N_SLC_MAX = 16
WIN = 512
REL_BUCKETS = 32
REL_MAX_DIST = 4096
EPS = 1e-6
NEG = -1e30
LOG2E = math.log2(math.e)

LANES = 128
ROWS = 16
KEY_ROWS = 64
INT_MIN = np.int32(-2 ** 31)
VMEM_LIMIT = 56 * 1024 * 1024

F32 = jnp.float32
BF16 = jnp.bfloat16
NT_DIMS = (((1,), (1,)), ((), ()))


def _dot(a, b):
    return jnp.dot(a, b, preferred_element_type=F32)


def _dot_nt(a, b):
    return lax.dot_general(a, b, NT_DIMS, preferred_element_type=F32)


def _rms(x, g):
    return x * lax.rsqrt(jnp.mean(x * x, axis=-1, keepdims=True) + EPS) * g


def _to_key(v):
    bits = lax.bitcast_convert_type(v, jnp.int32)
    return bits ^ ((bits >> 31) & jnp.int32(0x7FFFFFFF))


def _const_spec(shape):
    nd = len(shape)
    return pl.BlockSpec(shape, lambda *_: (0,) * nd, pipeline_mode=pl.Buffered(1))


def _params(sem):
    return pltpu.CompilerParams(dimension_semantics=sem, vmem_limit_bytes=VMEM_LIMIT)


def _rel_bucket(dist):
    dist = jnp.maximum(dist, 0)
    exact = REL_BUCKETS // 2
    log_ratio = jnp.log(jnp.maximum(dist, 1).astype(F32) / exact) / math.log(REL_MAX_DIST / exact)
    large = exact + (log_ratio * (REL_BUCKETS - exact)).astype(jnp.int32)
    return jnp.where(dist < exact, dist, jnp.minimum(large, REL_BUCKETS - 1))


def _num_bias_tiles(T):
    exact = REL_BUCKETS // 2
    switch = exact * (REL_MAX_DIST / exact) ** ((REL_BUCKETS - exact - 1) / (REL_BUCKETS - exact))
    far = int(math.ceil(switch)) + 32
    return min(T // LANES, -(-(far + LANES - 1) // LANES) + 1)


def _bias_tiles(rel_bias, T):
    nd = _num_bias_tiles(T)
    assert nd > WIN // LANES
    c = LANES * (nd - 1)
    bd = rel_bias[_rel_bucket(jnp.arange(c + LANES, dtype=jnp.int32))].T.astype(F32) * LOG2E
    p = c + 2 * LANES
    w = jnp.concatenate([bd[:, c::-1], jnp.broadcast_to(bd[:, :1], (N_HEADS, LANES)), bd[:, :c:-1]], axis=1)
    band = jnp.tile(w, (1, LANES))[:, :LANES * (p - 1)].reshape(N_HEADS, LANES, p - 1)[:, :, :c + LANES]
    tiles = band.reshape(N_HEADS, LANES, nd, LANES)[:, :, ::-1]
    tiles = jnp.transpose(tiles, (2, 0, 1, 3))
    i = jnp.arange(LANES)[:, None]
    j = jnp.arange(LANES)[None, :]
    ahead = jnp.full((1,) + tiles.shape[1:], NEG, F32)
    diag = jnp.where(i >= j, tiles[:1], NEG)
    win_edge = jnp.where(i < j, tiles[WIN // LANES:WIN // LANES + 1], NEG)
    return jnp.concatenate([ahead, diag, tiles[1:], win_edge], axis=0)


def _tile_ids(d0, n_sub, nd):
    return [jnp.clip(d0 - k, -1, nd - 1) + 1 for k in range(n_sub)]


def _dsa_proj_kernel(x_ref, g_ref, w_in_ref, w_wt_ref, gq_ref, gkv_ref, gk_ref, w_uq_ref, w_ukt_ref, w_qi_ref,
                     qa_ref, qi_ref, ckv_ref, kidx_ref, wi_ref):
    h = _rms(x_ref[0], g_ref[...]).astype(BF16)
    proj = _dot(h, w_in_ref[...])
    c_q = _rms(proj[:, :Q_LORA], gq_ref[...]).astype(BF16)
    ckv_ref[0] = _rms(proj[:, Q_LORA:Q_LORA + KV_LORA], gkv_ref[...]).astype(BF16)
    o_k = Q_LORA + KV_LORA
    kidx_ref[0] = _rms(proj[:, o_k:o_k + IDX_DIM], gk_ref[...]).astype(BF16)
    wi_ref[0] = _dot_nt(w_wt_ref[...], h) * (IDX_HEADS ** -0.5 * IDX_DIM ** -0.5)
    q = _dot(c_q, w_uq_ref[...])
    for hh in range(N_HEADS):
        qh = q[:, hh * HEAD_DIM:(hh + 1) * HEAD_DIM].astype(BF16)
        qa_ref[0, hh] = (_dot(qh, w_ukt_ref[hh]) * (HEAD_DIM ** -0.5 * LOG2E)).astype(BF16)
        qi_ref[0, hh] = _dot(c_q, w_qi_ref[hh]).astype(BF16)


def _dsa_proj(x, g, w_in, gq, gkv, gk, w_uq, w_uk, w_q_idx, tm=256):
    B, T, D = x.shape
    n_in = Q_LORA + KV_LORA + LANES
    o_k = Q_LORA + KV_LORA
    w_in_p = jnp.zeros((D, n_in), F32).at[:, :o_k + IDX_DIM].set(w_in[:, :o_k + IDX_DIM])
    w_wt = w_in[:, o_k + IDX_DIM:].T.astype(BF16)
    w_ukt = jnp.transpose(w_uk, (1, 2, 0)).astype(BF16)
    w_qi = jnp.transpose(w_q_idx, (1, 0, 2)).astype(BF16)
    tile = lambda b, i: (b, i, 0)
    htile = lambda b, i: (b, 0, i, 0)
    return pl.pallas_call(
        _dsa_proj_kernel,
        grid=(B, T // tm),
        in_specs=[
            pl.BlockSpec((1, tm, D), tile),
            _const_spec((1, D)),
            _const_spec((D, n_in)),
            _const_spec((IDX_HEADS, D)),
            _const_spec((1, Q_LORA)), _const_spec((1, KV_LORA)), _const_spec((1, IDX_DIM)),
            _const_spec((Q_LORA, N_HEADS * HEAD_DIM)),
            _const_spec((N_HEADS, HEAD_DIM, KV_LORA)),
            _const_spec((IDX_HEADS, Q_LORA, IDX_DIM)),
        ],
        out_specs=[
            pl.BlockSpec((1, N_HEADS, tm, KV_LORA), htile),
            pl.BlockSpec((1, IDX_HEADS, tm, IDX_DIM), htile),
            pl.BlockSpec((1, tm, KV_LORA), tile),
            pl.BlockSpec((1, tm, IDX_DIM), tile),
            pl.BlockSpec((1, IDX_HEADS, tm), lambda b, i: (b, 0, i)),
        ],
        out_shape=[
            jax.ShapeDtypeStruct((B, N_HEADS, T, KV_LORA), BF16),
            jax.ShapeDtypeStruct((B, IDX_HEADS, T, IDX_DIM), BF16),
            jax.ShapeDtypeStruct((B, T, KV_LORA), BF16),
            jax.ShapeDtypeStruct((B, T, IDX_DIM), BF16),
            jax.ShapeDtypeStruct((B, IDX_HEADS, T), F32),
        ],
        compiler_params=_params(("parallel", "parallel")),
        name="dsa_proj",
    )(x, g.reshape(1, D), w_in_p.astype(BF16), w_wt, gq.reshape(1, -1), gkv.reshape(1, -1), gk.reshape(1, -1),
      w_uq.reshape(Q_LORA, N_HEADS * HEAD_DIM).astype(BF16), w_ukt, w_qi)


def _tile_lanes(a, n):
    return a if n == 1 else jnp.concatenate([a] * n, axis=1)


def _bias_rows(toe_ref, head, tiles, rows):
    parts = [toe_ref[d, head, rows, :] for d in tiles]
    return parts[0] if len(parts) == 1 else jnp.concatenate(parts, axis=1)


def _flash_head(lg_scr, lrow0, toe_ref, tiles, head, m_scr, acc_scr, p_scr, prow0, tq):
    n_rep = len(tiles)
    tk = n_rep * LANES
    for r in range(tq // ROWS):
        rows = slice(r * ROWS, (r + 1) * ROWS)
        lg = lg_scr[lrow0 + r * ROWS:lrow0 + (r + 1) * ROWS, 0:tk] + _bias_rows(toe_ref, head, tiles, rows)
        m_old = m_scr[head, rows]
        m_new = jnp.maximum(m_old, jnp.max(lg, axis=1, keepdims=True))
        m_scr[head, rows] = m_new
        p_scr[prow0 + r * ROWS:prow0 + (r + 1) * ROWS, 0:tk] = jnp.exp2(lg - _tile_lanes(m_new, n_rep)).astype(BF16)
        arows = slice(head * tq + r * ROWS, head * tq + (r + 1) * ROWS)
        acc_scr[arows, :] = acc_scr[arows, :] * _tile_lanes(jnp.exp2(m_old - m_new), 2)


def _with_ones(v):
    return jnp.concatenate([v, jnp.ones(v.shape, v.dtype)], axis=1)


def _flash_groups(nch, qx_scr, k_aug, v_aug, tile_ids, toe_ref, lg_scrs, p_scrs, m_scr, acc_scr, tq):
    R = HEADS_PER_GROUP
    grows = [slice(g * R * tq, (g + 1) * R * tq) for g in range(N_KV_GROUPS)]

    def logits(c, g):
        k = k_aug(c, g)
        lg_scrs[g][:, 0:k.shape[0]] = _dot_nt(qx_scr[grows[g], :], k)

    def softmax(c, g):
        tiles = tile_ids(c)
        for rr in range(R):
            _flash_head(lg_scrs[g], rr * tq, toe_ref, tiles, g * R + rr, m_scr, acc_scr, p_scrs[g], rr * tq, tq)

    def values(c, g):
        v = v_aug(c, g)
        acc_scr[grows[g], :] = acc_scr[grows[g], :] + _dot(p_scrs[g][:, 0:v.shape[0]], v)

    m_scr[...] = jnp.full(m_scr.shape, NEG, F32)
    acc_scr[...] = jnp.zeros(acc_scr.shape, F32)
    p_scrs[1][...] = jnp.zeros(p_scrs[1].shape, BF16)
    logits(0, 0)

    def chunk(c, carry):
        logits(c, 1)
        softmax(c, 0)
        values(jnp.maximum(c - 1, 0), 1)
        logits(jnp.minimum(c + 1, nch - 1), 0)
        softmax(c, 1)
        values(c, 0)
        return carry

    lax.fori_loop(0, nch, chunk, 0)
    values(nch - 1, 1)


def _masked_softmax_rows(lg, mask):
    lg = jnp.where(mask, lg, NEG)
    e = jnp.where(mask, jnp.exp2(lg - jnp.max(lg, axis=1, keepdims=True)), 0.0)
    return e / jnp.maximum(jnp.sum(e, axis=1, keepdims=True), 1e-30)


def _fold_rows(a, n):
    parts = [a[i:i + n] for i in range(0, a.shape[0], n)]
    while len(parts) > 1:
        parts = [parts[i] + parts[i + 1] for i in range(0, len(parts), 2)]
    return parts[0]


def _dsa_attn_kernel(qi_ref, wi_ref, kidx_ref, qa_ref, ckv_ref, toe_ref, o_ref,
                     s_scr, rel_scr, lg0_scr, lg1_scr, p0_scr, p1_scr, qx_scr, m_scr, acc_scr,
                     *, tq, tk, top_k, nd, idx_bits):
    t0 = pl.program_id(1) * tq
    nch = (t0 + tq + tk - 1) // tk
    n_sub = tk // LANES
    sub = 8
    key_row = lax.broadcasted_iota(jnp.int32, (tk, tq), 0)

    wi = wi_ref[0]
    qi = qi_ref[0].reshape(IDX_HEADS * tq, IDX_DIM)
    k_slab = lax.broadcasted_iota(jnp.int32, (KEY_ROWS, tq), 0)
    q_slab = lax.broadcasted_iota(jnp.int32, (KEY_ROWS, tq), 1)

    def score_chunk(c, carry):
        s0 = pl.multiple_of(c * tk, tk)
        rel_scr[...] = _dot_nt(kidx_ref[0, pl.ds(s0, tk), :], qi)
        for r in range(tk // KEY_ROWS):
            rows = slice(r * KEY_ROWS, (r + 1) * KEY_ROWS)
            acc = jnp.zeros((KEY_ROWS, tq), F32)
            for hh in range(IDX_HEADS):
                acc = acc + wi[hh:hh + 1, :] * jnp.maximum(rel_scr[rows, hh * tq:(hh + 1) * tq], 0.0)
            key = _to_key(acc + 0.0)
            s_scr[c, rows, :] = jnp.where(k_slab + (s0 + r * KEY_ROWS) <= q_slab + t0, key, INT_MIN)
        return carry

    lax.fori_loop(0, nch, score_chunk, 0)

    def count(pred):
        def body(c, cnt):
            return cnt + _fold_rows(jnp.where(pred(s_scr[c], key_row + c * tk), 1.0, 0.0), sub)
        cnt = lax.fori_loop(0, nch, body, jnp.zeros((sub, tq), F32))
        return jnp.sum(cnt, axis=0, keepdims=True)

    kf = jnp.float32(top_k)

    def thr_bit(bi, carry):
        thr, n_ge = carry
        cand = thr ^ lax.shift_left(jnp.int32(1), 31 - bi)
        cnt = count(lambda blk, idx: blk >= cand)
        ok = cnt >= kf
        return jnp.where(ok, cand, thr), jnp.where(ok, cnt, n_ge)

    thr, n_ge = lax.fori_loop(0, 32, thr_bit,
                              (jnp.full((1, tq), INT_MIN, jnp.int32), jnp.zeros((1, tq), F32)))
    thr = jnp.maximum(thr, INT_MIN + 1)

    @pl.when(jnp.max(n_ge) > kf)
    def _():
        need = kf - count(lambda blk, idx: blk > thr)

        def idx_bit(bi, last):
            cand = last | lax.shift_left(jnp.int32(1), idx_bits - 1 - bi)
            cnt = count(lambda blk, idx: (blk == thr) & (idx < cand))
            return jnp.where(cnt < need, cand, last)

        last = lax.fori_loop(0, idx_bits, idx_bit, jnp.zeros((1, tq), jnp.int32))

        def drop(c, carry):
            blk = s_scr[c]
            s_scr[c] = jnp.where((blk == thr) & (key_row + c * tk > last), INT_MIN, blk)
            return carry

        lax.fori_loop(0, nch, drop, 0)

    eye = (lax.broadcasted_iota(jnp.int32, (tq, tq), 0) == lax.broadcasted_iota(jnp.int32, (tq, tq), 1))
    qx_scr[:, 0:KV_LORA] = qa_ref[0].reshape(N_HEADS * tq, KV_LORA)
    for hh in range(N_HEADS):
        qx_scr[hh * tq:(hh + 1) * tq, KV_LORA:KV_LORA + tq] = jnp.where(eye, 1.0, 0.0).astype(BF16)

    def latents(c):
        return ckv_ref[0, pl.ds(pl.multiple_of(c * tk, tk), tk), :]

    def k_aug(c, g):
        mask_t = jnp.where(s_scr[c] >= thr, 0.0, NEG).astype(BF16)
        return jnp.concatenate([latents(c), mask_t], axis=1)

    _flash_groups(nch, qx_scr, k_aug, lambda c, g: _with_ones(latents(c)),
                  lambda c: _tile_ids((t0 - c * tk) // LANES, n_sub, nd),
                  toe_ref, (lg0_scr, lg1_scr), (p0_scr, p1_scr), m_scr, acc_scr, tq)
    for hh in range(N_HEADS):
        rows = slice(hh * tq, (hh + 1) * tq)
        o = acc_scr[rows, 0:KV_LORA] / acc_scr[rows, KV_LORA:2 * KV_LORA]
        o_ref[0, :, hh * KV_LORA:(hh + 1) * KV_LORA] = o.astype(o_ref.dtype)


def _dsa_attn(qi, wi, kidx, qa, ckv, toe, tq=128, tk=512):
    B, _, T, _ = qa.shape
    tk = min(tk, T)
    assert tq == LANES
    nd = toe.shape[0] - 2
    top_k = min(IDX_TOPK_MAX, T // 4)
    kern = functools.partial(_dsa_attn_kernel, tq=tq, tk=tk, top_k=top_k, nd=nd,
                             idx_bits=max(1, (T - 1).bit_length()))
    return pl.pallas_call(
        kern,
        grid=(B, T // tq),
        in_specs=[
            pl.BlockSpec((1, IDX_HEADS, tq, IDX_DIM), lambda b, i: (b, 0, i, 0)),
            pl.BlockSpec((1, IDX_HEADS, tq), lambda b, i: (b, 0, i)),
            pl.BlockSpec((1, T, IDX_DIM), lambda b, i: (b, 0, 0), pipeline_mode=pl.Buffered(1)),
            pl.BlockSpec((1, N_HEADS, tq, KV_LORA), lambda b, i: (b, 0, i, 0)),
            pl.BlockSpec((1, T, KV_LORA), lambda b, i: (b, 0, 0), pipeline_mode=pl.Buffered(1)),
            _const_spec(toe.shape),
        ],
        out_specs=pl.BlockSpec((1, tq, N_HEADS * KV_LORA), lambda b, i: (b, i, 0)),
        out_shape=jax.ShapeDtypeStruct((B, T, N_HEADS * KV_LORA), BF16),
        scratch_shapes=[
            pltpu.VMEM((T // tk, tk, tq), jnp.int32),
            pltpu.VMEM((tk, IDX_HEADS * tq), F32),
            pltpu.VMEM((HEADS_PER_GROUP * tq, tk), F32),
            pltpu.VMEM((HEADS_PER_GROUP * tq, tk), F32),
            pltpu.VMEM((HEADS_PER_GROUP * tq, tk), BF16),
            pltpu.VMEM((HEADS_PER_GROUP * tq, tk), BF16),
            pltpu.VMEM((N_HEADS * tq, KV_LORA + tq), BF16),
            pltpu.VMEM((N_HEADS, tq, LANES), F32),
            pltpu.VMEM((N_HEADS * tq, 2 * KV_LORA), F32),
        ],
        compiler_params=_params(("parallel", "arbitrary")),
        name="dsa_attn",
    )(qi, wi, kidx, qa, ckv, toe)


def _dsa_out_kernel(o_ref, x_ref, w_uv_ref, w_o_ref, y_ref):
    o_lat = o_ref[0]
    parts = [_dot(o_lat[:, hh * KV_LORA:(hh + 1) * KV_LORA], w_uv_ref[hh]).astype(BF16)
             for hh in range(N_HEADS)]
    y_ref[0] = x_ref[0] + _dot(jnp.concatenate(parts, axis=1), w_o_ref[...])


def _dsa_out(o_lat, x, w_uv, w_o, tm=512):
    B, T, D = x.shape
    tile = lambda b, i: (b, i, 0)
    return pl.pallas_call(
        _dsa_out_kernel,
        grid=(B, T // tm),
        in_specs=[pl.BlockSpec((1, tm, N_HEADS * KV_LORA), tile), pl.BlockSpec((1, tm, D), tile),
                  _const_spec((N_HEADS, KV_LORA, HEAD_DIM)), _const_spec((N_HEADS * HEAD_DIM, D))],
        out_specs=pl.BlockSpec((1, tm, D), tile),
        out_shape=jax.ShapeDtypeStruct((B, T, D), F32),
        compiler_params=_params(("parallel", "parallel")),
        name="dsa_out",
    )(o_lat, x, jnp.transpose(w_uv, (1, 0, 2)).astype(BF16), w_o.astype(BF16))


def _out_proj_kernel(o_ref, x_ref, w_o_ref, y_ref):
    y_ref[0] = x_ref[0] + _dot(o_ref[0], w_o_ref[...])


def _out_proj(o, x, w_o, tm=512):
    B, T, D = x.shape
    tile = lambda b, i: (b, i, 0)
    return pl.pallas_call(
        _out_proj_kernel,
        grid=(B, T // tm),
        in_specs=[pl.BlockSpec((1, tm, o.shape[-1]), tile), pl.BlockSpec((1, tm, D), tile),
                  _const_spec(w_o.shape)],
        out_specs=pl.BlockSpec((1, tm, D), tile),
        out_shape=jax.ShapeDtypeStruct((B, T, D), F32),
        compiler_params=_params(("parallel", "parallel")),
        name="out_proj",
    )(o, x, w_o.astype(BF16))


def _mlp_kernel(x_ref, g_ref, w_up_ref, w_down_ref, *rest, tf, final):
    y_ref = rest[-1]
    x = x_ref[0]
    h = _rms(x, g_ref[...]).astype(BF16)
    acc = x
    for f0 in range(0, w_up_ref.shape[1], tf):
        u = jnp.maximum(_dot(h, w_up_ref[:, f0:f0 + tf]), 0.0)
        acc = acc + _dot((u * u).astype(BF16), w_down_ref[f0:f0 + tf, :])
    y_ref[0] = _rms(acc, rest[0][...]) if final else acc


def _mlp(x, g, w_up, w_down, g_final=None, tm=512, tf=512):
    B, T, D = x.shape
    F = w_up.shape[1]
    tile = lambda b, i: (b, i, 0)
    final = g_final is not None
    in_specs = [pl.BlockSpec((1, tm, D), tile), _const_spec((1, D)), _const_spec((D, F)), _const_spec((F, D))]
    args = [x, g.reshape(1, D), w_up.astype(BF16), w_down.astype(BF16)]
    if final:
        in_specs.append(_const_spec((1, D)))
        args.append(g_final.reshape(1, D))
    return pl.pallas_call(
        functools.partial(_mlp_kernel, tf=tf, final=final),
        grid=(B, T // tm),
        in_specs=in_specs,
        out_specs=pl.BlockSpec((1, tm, D), tile),
        out_shape=jax.ShapeDtypeStruct((B, T, D), F32),
        compiler_params=_params(("parallel", "parallel")),
        name="mlp_final" if final else "mlp",
    )(*args)


def _kv_proj_kernel(x_ref, g_ref, w_ref, kc_ref, vc_ref, ks_ref, vs_ref, kw_ref, vw_ref):
    h = _rms(x_ref[0], g_ref[...]).astype(BF16)
    kv = _dot(h, w_ref[...])
    for part, ref in enumerate((kc_ref, vc_ref, ks_ref, vs_ref, kw_ref, vw_ref)):
        for g in range(N_KV_GROUPS):
            o = (part * N_KV_GROUPS + g) * HEAD_DIM
            ref[0, g] = kv[:, o:o + HEAD_DIM].astype(ref.dtype)


def _kv_proj(x, g, w_kv, tm=512):
    B, T, D = x.shape
    gtile = lambda b, i: (b, 0, i, 0)
    spec = pl.BlockSpec((1, N_KV_GROUPS, tm, HEAD_DIM), gtile)
    shp = lambda dt: jax.ShapeDtypeStruct((B, N_KV_GROUPS, T, HEAD_DIM), dt)
    return pl.pallas_call(
        _kv_proj_kernel,
        grid=(B, T // tm),
        in_specs=[pl.BlockSpec((1, tm, D), lambda b, i: (b, i, 0)), _const_spec((1, D)),
                  _const_spec(w_kv.shape)],
        out_specs=[spec] * 6,
        out_shape=[shp(F32), shp(F32), shp(BF16), shp(BF16), shp(BF16), shp(BF16)],
        compiler_params=_params(("parallel", "parallel")),
        name="kv_proj",
    )(x, g.reshape(1, D), w_kv.astype(BF16))


def _compress_kernel(raw_ref, pos_ref, w1_ref, w2_ref, o_ref):
    rows = raw_ref[0, 0]
    half = rows.shape[1]
    a = _dot((rows + pos_ref[:, :half]).astype(BF16), w1_ref[:half, :])
    b = _dot((rows + pos_ref[:, half:]).astype(BF16), w1_ref[half:, :])
    pre = a + pltpu.roll(b, rows.shape[0] - 1, 0)
    act = 0.5 * pre * (1.0 + jnp.tanh(math.sqrt(2.0 / math.pi) * (pre + 0.044715 * (pre * pre * pre))))
    o_ref[0, 0] = _dot(act.astype(BF16), w2_ref[...]).astype(o_ref.dtype)


def _compress(raw, pos, w1, w2):
    B, G, T, Dh = raw.shape
    nr = T // CMP_STRIDE
    rows = raw.reshape(B, G, nr, CMP_STRIDE * Dh)
    return pl.pallas_call(
        _compress_kernel,
        grid=(B, G),
        in_specs=[pl.BlockSpec((1, 1, nr, CMP_STRIDE * Dh), lambda b, g: (b, g, 0, 0)),
                  _const_spec((1, CMP_LEN * Dh)), _const_spec(w1.shape), _const_spec(w2.shape)],
        out_specs=pl.BlockSpec((1, 1, nr, Dh), lambda b, g: (b, g, 0, 0)),
        out_shape=jax.ShapeDtypeStruct((B, G, nr, Dh), BF16),
        compiler_params=_params(("parallel", "parallel")),
        name="compress",
    )(rows, pos.reshape(1, CMP_LEN * Dh), w1.astype(BF16), w2.astype(BF16))


def _nsa_proj_kernel(x_ref, g_ref, wq_ref, wg_ref, q_ref, gate_ref):
    h = _rms(x_ref[0], g_ref[...]).astype(BF16)
    q = _dot(h, wq_ref[...]) * (HEAD_DIM ** -0.5 * LOG2E)
    for hh in range(N_HEADS):
        q_ref[0, hh] = q[:, hh * HEAD_DIM:(hh + 1) * HEAD_DIM].astype(BF16)
    gate_ref[0] = jax.nn.sigmoid(_dot(h, wg_ref[...]))[:, :3 * N_HEADS]


def _nsa_proj(x, g, w_in, tm=512):
    B, T, D = x.shape
    hd = N_HEADS * HEAD_DIM
    wq = w_in[:, :hd].astype(BF16)
    wg = jnp.zeros((D, LANES), F32).at[:, :3 * N_HEADS].set(w_in[:, hd:]).astype(BF16)
    return pl.pallas_call(
        _nsa_proj_kernel,
        grid=(B, T // tm),
        in_specs=[pl.BlockSpec((1, tm, D), lambda b, i: (b, i, 0)), _const_spec((1, D)),
                  _const_spec((D, hd)), _const_spec((D, LANES))],
        out_specs=[pl.BlockSpec((1, N_HEADS, tm, HEAD_DIM), lambda b, i: (b, 0, i, 0)),
                   pl.BlockSpec((1, tm, 3 * N_HEADS), lambda b, i: (b, i, 0))],
        out_shape=[jax.ShapeDtypeStruct((B, N_HEADS, T, HEAD_DIM), BF16),
                   jax.ShapeDtypeStruct((B, T, 3 * N_HEADS), F32)],
        compiler_params=_params(("parallel", "parallel")),
        name="nsa_proj",
    )(x, g.reshape(1, D), wq, wg)


def _top_blocks(imp, blk, n_sel, exact_ties):
    sel = jnp.zeros(imp.shape, F32)
    blk_f = blk.astype(F32)
    for _ in range(n_sel):
        if exact_ties:
            best = jnp.max(imp, axis=1, keepdims=True)
            hit = blk_f == jnp.min(jnp.where(imp == best, blk_f, float(LANES)), axis=1, keepdims=True)
        else:
            hit = blk == jnp.argmax(imp, axis=1, keepdims=True).astype(jnp.int32)
        sel = jnp.where(hit, 1.0, sel)
        imp = jnp.where(hit, -jnp.inf, imp)
    return sel


def _nsa_attn_kernel(q_ref, gate_ref, kc_ref, vc_ref, ks_ref, vs_ref, kw_ref, vw_ref, selmap_ref, toe_ref,
                     o_ref, sel_scr, imp_scr, lg0_scr, lg1_scr, p0_scr, p1_scr, psum_scr, qx_scr, m_scr, acc_scr,
                     ocw_scr, *, tq, tk, nd, n_slc, n_sel):
    G, R = N_KV_GROUPS, HEADS_PER_GROUP
    lg_scrs, p_scrs = (lg0_scr, lg1_scr), (p0_scr, p1_scr)
    t0 = pl.program_id(1) * tq
    n_cmp = kc_ref.shape[2]
    n_sub = tk // LANES
    wlen = WIN + tq
    gates = gate_ref[0]
    t_col = t0 + lax.broadcasted_iota(jnp.int32, (tq, 1), 0)
    t_rows = t0 + lax.broadcasted_iota(jnp.int32, (ROWS, 1), 0)
    slabs = [slice(r * ROWS, (r + 1) * ROWS) for r in range(tq // ROWS)]

    cmp_end = CMP_STRIDE * lax.broadcasted_iota(jnp.int32, (ROWS, n_cmp), 1) + (CMP_LEN - 1)
    blk = lax.broadcasted_iota(jnp.int32, (tq, LANES), 1)
    cur = t_col // SLC_BLK
    forced = (blk == 0) | (blk == cur) | (blk == cur - 1)
    future = blk * SLC_BLK > t_col
    for g in range(G):
        lg_scr, p_scr = lg_scrs[g], p_scrs[g]
        qg = q_ref[0, g * R:(g + 1) * R].reshape(R * tq, HEAD_DIM)
        lg_scr[0:R * tq, 0:n_cmp] = _dot_nt(qg, kc_ref[0, g])
        for r, rows in enumerate(slabs):
            cmask = cmp_end <= t_rows + r * ROWS
            pc_sum = jnp.zeros((ROWS, n_cmp), F32)
            for rr in range(R):
                pc = _masked_softmax_rows(lg_scr[rr * tq + r * ROWS:rr * tq + (r + 1) * ROWS, 0:n_cmp], cmask)
                pc_sum = pc_sum + pc
                p_scr[rr * tq + r * ROWS:rr * tq + (r + 1) * ROWS, 0:n_cmp] = pc.astype(BF16)
            psum_scr[rows, :] = pc_sum
        o_cmp = _dot(p_scr[:, 0:n_cmp], vc_ref[0, g])
        for rr in range(R):
            hh = g * R + rr
            ocw_scr[hh * tq:(hh + 1) * tq, :] = gates[:, 3 * hh:3 * hh + 1] * o_cmp[rr * tq:(rr + 1) * tq]
        pc_sum = psum_scr[...]
        hi = pc_sum.astype(BF16)
        lo = (pc_sum - hi.astype(F32)).astype(BF16)
        imp = _dot(hi, selmap_ref[...]) + _dot(lo, selmap_ref[...])
        imp = jnp.where(forced, 1e9, imp)
        imp = jnp.where(future, NEG, imp)
        imp = jnp.where(blk < n_slc, imp, -jnp.inf)
        sel = _top_blocks(imp, blk, n_sel, exact_ties=False)
        sel_scr[g] = sel
        imp_scr[g] = imp
        cut = jnp.min(jnp.where(sel > 0.0, imp, jnp.inf), axis=1, keepdims=True)
        open_tie = jnp.max(jnp.where((imp == cut) & (sel == 0.0) & ~future, 1.0, 0.0)) > 0.0

        @pl.when(open_tie)
        def _():
            sel_scr[g] = _top_blocks(imp_scr[g], blk, n_sel, exact_ties=True)

        block_bias = jnp.where((sel_scr[g] > 0.0) & ~future, 0.0, NEG).astype(BF16)
        grows = slice(g * R * tq, (g + 1) * R * tq)
        qx_scr[grows, 0:HEAD_DIM] = qg
        for rr in range(R):
            qx_scr[(g * R + rr) * tq:(g * R + rr + 1) * tq, HEAD_DIM:HEAD_DIM + LANES] = block_bias

    nch = (t0 + tq + tk - 1) // tk
    lane_minus_block = (lax.broadcasted_iota(jnp.int32, (tk, LANES), 1)
                        - lax.broadcasted_iota(jnp.int32, (tk, LANES), 0) // SLC_BLK)

    def k_aug(c, g):
        s0 = pl.multiple_of(c * tk, tk)
        block_of_key = jnp.where(lane_minus_block == s0 // SLC_BLK, 1.0, 0.0).astype(BF16)
        return jnp.concatenate([ks_ref[0, g, pl.ds(s0, tk), :], block_of_key], axis=1)

    _flash_groups(nch, qx_scr, k_aug, lambda c, g: _with_ones(vs_ref[0, g, pl.ds(pl.multiple_of(c * tk, tk), tk), :]),
                  lambda c: _tile_ids((t0 - c * tk) // LANES, n_sub, nd),
                  toe_ref, lg_scrs, p_scrs, m_scr, acc_scr, tq)

    start = pl.multiple_of(jnp.maximum(t0 - WIN, 0), LANES)
    dw = (t0 - start) // LANES
    wtiles = _tile_ids(dw, wlen // LANES, nd)
    wtiles[0] = jnp.where(dw == WIN // LANES, nd + 1, wtiles[0])
    for g in range(G):
        lg_scr, p_scr = lg_scrs[g], p_scrs[g]
        qg = q_ref[0, g * R:(g + 1) * R].reshape(R * tq, HEAD_DIM)
        lg_scr[0:R * tq, 0:wlen] = _dot_nt(qg, kw_ref[0, g, pl.ds(start, wlen), :])
        for rr in range(R):
            hh = g * R + rr
            for r, rows in enumerate(slabs):
                lw = lg_scr[rr * tq + r * ROWS:rr * tq + (r + 1) * ROWS, 0:wlen] + _bias_rows(toe_ref, hh, wtiles, rows)
                pw = jnp.exp2(lw - jnp.max(lw, axis=1, keepdims=True))
                p_scr[rr * tq + r * ROWS:rr * tq + (r + 1) * ROWS, 0:wlen] = pw.astype(BF16)
        o_win = _dot(p_scr[:, 0:wlen], _with_ones(vw_ref[0, g, pl.ds(start, wlen), :]))
        for rr in range(R):
            hh = g * R + rr
            rows = slice(hh * tq, (hh + 1) * tq)
            wrows = slice(rr * tq, (rr + 1) * tq)
            o_slc = acc_scr[rows, 0:HEAD_DIM] / acc_scr[rows, HEAD_DIM:2 * HEAD_DIM]
            o = (ocw_scr[rows, :] + gates[:, 3 * hh + 1:3 * hh + 2] * o_slc
                 + gates[:, 3 * hh + 2:3 * hh + 3] * (o_win[wrows, 0:HEAD_DIM] / o_win[wrows, HEAD_DIM:2 * HEAD_DIM]))
            o_ref[0, :, hh * HEAD_DIM:(hh + 1) * HEAD_DIM] = o.astype(o_ref.dtype)


def _selection_map(n_cmp, n_slc):
    c0 = CMP_STRIDE * np.arange(n_cmp)[:, None]
    s0 = SLC_BLK * np.arange(n_slc)[None, :]
    ov = np.clip(np.minimum(c0 + CMP_LEN, s0 + SLC_BLK) - np.maximum(c0, s0), 0, None)
    return (ov / CMP_LEN).astype(np.float32)


def _nsa_attn(q, gates, kc, vc, ks, vs, kw, vw, toe, tq=128, tk=512):
    B, _, T, _ = q.shape
    tk = min(tk, T)
    assert tq == LANES
    nd = toe.shape[0] - 2
    n_cmp_pad = kc.shape[2]
    n_cmp = (T - CMP_LEN) // CMP_STRIDE + 1
    n_slc = T // SLC_BLK
    assert n_slc <= LANES and T >= WIN + tq
    selmap = np.zeros((n_cmp_pad, LANES), np.float32)
    selmap[:n_cmp, :n_slc] = _selection_map(n_cmp, n_slc)
    kern = functools.partial(_nsa_attn_kernel, tq=tq, tk=tk, nd=nd, n_slc=n_slc, n_sel=min(N_SLC_MAX, n_slc))
    G, R = N_KV_GROUPS, HEADS_PER_GROUP
    width = max(n_cmp_pad, tk, WIN + tq)
    res = lambda n: pl.BlockSpec((1, G, n, HEAD_DIM), lambda b, i: (b, 0, 0, 0), pipeline_mode=pl.Buffered(1))
    return pl.pallas_call(
        kern,
        grid=(B, T // tq),
        in_specs=[
            pl.BlockSpec((1, N_HEADS, tq, HEAD_DIM), lambda b, i: (b, 0, i, 0)),
            pl.BlockSpec((1, tq, 3 * N_HEADS), lambda b, i: (b, i, 0)),
            res(n_cmp_pad), res(n_cmp_pad), res(T), res(T), res(T), res(T),
            _const_spec(selmap.shape),
            _const_spec(toe.shape),
        ],
        out_specs=pl.BlockSpec((1, tq, N_HEADS * HEAD_DIM), lambda b, i: (b, i, 0)),
        out_shape=jax.ShapeDtypeStruct((B, T, N_HEADS * HEAD_DIM), BF16),
        scratch_shapes=[
            pltpu.VMEM((G, tq, LANES), F32),
            pltpu.VMEM((G, tq, LANES), F32),
            pltpu.VMEM((R * tq, width), F32),
            pltpu.VMEM((R * tq, width), F32),
            pltpu.VMEM((R * tq, width), BF16),
            pltpu.VMEM((R * tq, width), BF16),
            pltpu.VMEM((tq, n_cmp_pad), F32),
            pltpu.VMEM((N_HEADS * tq, HEAD_DIM + LANES), BF16),
            pltpu.VMEM((N_HEADS, tq, LANES), F32),
            pltpu.VMEM((N_HEADS * tq, 2 * HEAD_DIM), F32),
            pltpu.VMEM((N_HEADS * tq, HEAD_DIM), F32),
        ],
        compiler_params=_params(("parallel", "arbitrary")),
        name="nsa_attn",
    )(q, gates, kc, vc, ks, vs, kw, vw, jnp.asarray(selmap, BF16), toe)


def _dsa_layer(x, toe, g_attn, w_in, gq, gkv, gk, w_uq, w_q_idx, w_uk, w_uv, w_o):
    qa, qi, ckv, kidx, wi = _dsa_proj(x, g_attn, w_in, gq, gkv, gk, w_uq, w_uk, w_q_idx)
    o_lat = _dsa_attn(qi, wi, kidx, qa, ckv, toe)
    return _dsa_out(o_lat, x, w_uv, w_o)


def _shared_kv(x, g_kv, w_kv, pos_k, pos_v, w1_k, w2_k, w1_v, w2_v):
    kc_raw, vc_raw, ks, vs, kw, vw = _kv_proj(x, g_kv, w_kv)
    return _compress(kc_raw, pos_k, w1_k, w2_k), _compress(vc_raw, pos_v, w1_v, w2_v), ks, vs, kw, vw


def _nsa_layer(x, toe, kv_shared, g_attn, w_in, w_o):
    q, gates = _nsa_proj(x, g_attn, w_in)
    return _out_proj(_nsa_attn(q, gates, *kv_shared, toe), x, w_o)


def kernel(x, g_attn, g_mlp, w_up, w_down, rel_bias, a_w_in, a_g_q_lat, a_g_kv_lat, a_g_k_idx, a_w_uq, a_w_q_idx, a_w_uk, a_w_uv, a_w_o, g_kv_shared, w_kv_shared, cmp_pos_k, cmp_pos_v, cmp_w1_k, cmp_w2_k, cmp_w1_v, cmp_w2_v, b_w_in, b_w_o, g_final):
    depth = g_attn.shape[0]
    n_a = a_w_in.shape[0]
    toe = _bias_tiles(rel_bias, x.shape[1])
    kv_shared = None
    for l in range(depth):
        if l < n_a:
            x = _dsa_layer(x, toe, g_attn[l], a_w_in[l], a_g_q_lat[l], a_g_kv_lat[l], a_g_k_idx[l],
                           a_w_uq[l], a_w_q_idx[l], a_w_uk[l], a_w_uv[l], a_w_o[l])
        else:
            j = l - n_a
            x = _nsa_layer(x, toe, kv_shared, g_attn[l], b_w_in[j], b_w_o[j])
        x = _mlp(x, g_mlp[l], w_up[l], w_down[l], g_final if l == depth - 1 else None)
        if l == n_a - 1:
            kv_shared = _shared_kv(x, g_kv_shared, w_kv_shared, cmp_pos_k, cmp_pos_v,
                                   cmp_w1_k, cmp_w2_k, cmp_w1_v, cmp_w2_v)
    return x
```

```python
import functools
import math

import numpy as np
import jax
import jax.numpy as jnp
from jax import lax
from jax.experimental import pallas as pl
from jax.experimental.pallas import tpu as pltpu

N_HEADS = 8
HEAD_DIM = 128
Q_LORA = 256
KV_LORA = 128
IDX_HEADS = 8
IDX_DIM = 64
IDX_TOPK_MAX = 256
N_KV_GROUPS = 2
HEADS_PER_GROUP = N_HEADS // N_KV_GROUPS
CMP_LEN = 32
CMP_STRIDE = 16
CMP_HID = 256
SLC_BLK = 64
N_SLC_MAX = 16
WIN = 512
REL_BUCKETS = 32
REL_MAX_DIST = 4096
EPS = 1e-6
NEG = -1e30
LOG2E = math.log2(math.e)

LANES = 128
ROWS = 16
KEY_ROWS = 64
INT_MIN = np.int32(-2 ** 31)
VALUE_STEPS = 16
VMEM_LIMIT = 56 * 1024 * 1024

F32 = jnp.float32
BF16 = jnp.bfloat16
NT_DIMS = (((1,), (1,)), ((), ()))


def _dot(a, b):
    return jnp.dot(a, b, preferred_element_type=F32)


def _dot_nt(a, b):
    return lax.dot_general(a, b, NT_DIMS, preferred_element_type=F32)


def _rms(x, g):
    return x * lax.rsqrt(jnp.mean(x * x, axis=-1, keepdims=True) + EPS) * g


def _to_key(v):
    bits = lax.bitcast_convert_type(v, jnp.int32)
    return bits ^ ((bits >> 31) & jnp.int32(0x7FFFFFFF))


def _const_spec(shape):
    nd = len(shape)
    return pl.BlockSpec(shape, lambda *_: (0,) * nd, pipeline_mode=pl.Buffered(1))


def _params(sem):
    return pltpu.CompilerParams(dimension_semantics=sem, vmem_limit_bytes=VMEM_LIMIT)


def _rel_bucket(dist):
    dist = jnp.maximum(dist, 0)
    exact = REL_BUCKETS // 2
    log_ratio = jnp.log(jnp.maximum(dist, 1).astype(F32) / exact) / math.log(REL_MAX_DIST / exact)
    large = exact + (log_ratio * (REL_BUCKETS - exact)).astype(jnp.int32)
    return jnp.where(dist < exact, dist, jnp.minimum(large, REL_BUCKETS - 1))


def _num_bias_tiles(T):
    exact = REL_BUCKETS // 2
    switch = exact * (REL_MAX_DIST / exact) ** ((REL_BUCKETS - exact - 1) / (REL_BUCKETS - exact))
    far = int(math.ceil(switch)) + 32
    return min(T // LANES, -(-(far + LANES - 1) // LANES) + 1)


def _bias_tiles(rel_bias, T):
    nd = _num_bias_tiles(T)
    assert nd > WIN // LANES
    c = LANES * (nd - 1)
    bd = rel_bias[_rel_bucket(jnp.arange(c + LANES, dtype=jnp.int32))].T.astype(F32) * LOG2E
    p = c + 2 * LANES
    w = jnp.concatenate([bd[:, c::-1], jnp.broadcast_to(bd[:, :1], (N_HEADS, LANES)), bd[:, :c:-1]], axis=1)
    band = jnp.tile(w, (1, LANES))[:, :LANES * (p - 1)].reshape(N_HEADS, LANES, p - 1)[:, :, :c + LANES]
    tiles = band.reshape(N_HEADS, LANES, nd, LANES)[:, :, ::-1]
    tiles = jnp.transpose(tiles, (2, 0, 1, 3))
    i = jnp.arange(LANES)[:, None]
    j = jnp.arange(LANES)[None, :]
    ahead = jnp.full((1,) + tiles.shape[1:], NEG, F32)
    diag = jnp.where(i >= j, tiles[:1], NEG)
    win_edge = jnp.where(i < j, tiles[WIN // LANES:WIN // LANES + 1], NEG)
    return jnp.concatenate([ahead, diag, tiles[1:], win_edge], axis=0)


def _tile_ids(d0, n_sub, nd):
    return [jnp.clip(d0 - k, -1, nd - 1) + 1 for k in range(n_sub)]


def _dsa_proj_kernel(x_ref, g_ref, w_in_ref, w_wt_ref, gq_ref, gkv_ref, gk_ref, w_uq_ref, w_ukt_ref, w_qi_ref,
                     qa_ref, qi_ref, ckv_ref, kidx_ref, wi_ref):
    h = _rms(x_ref[0], g_ref[...]).astype(BF16)
    proj = _dot(h, w_in_ref[...])
    c_q = _rms(proj[:, :Q_LORA], gq_ref[...]).astype(BF16)
    ckv_ref[0] = _rms(proj[:, Q_LORA:Q_LORA + KV_LORA], gkv_ref[...]).astype(BF16)
    o_k = Q_LORA + KV_LORA
    kidx_ref[0] = _rms(proj[:, o_k:o_k + IDX_DIM], gk_ref[...]).astype(BF16)
    wi_ref[0] = _dot_nt(w_wt_ref[...], h) * (IDX_HEADS ** -0.5 * IDX_DIM ** -0.5)
    q = _dot(c_q, w_uq_ref[...])
    for hh in range(N_HEADS):
        qh = q[:, hh * HEAD_DIM:(hh + 1) * HEAD_DIM].astype(BF16)
        qa_ref[0, hh] = (_dot(qh, w_ukt_ref[hh]) * (HEAD_DIM ** -0.5 * LOG2E)).astype(BF16)
        qi_ref[0, hh] = _dot(c_q, w_qi_ref[hh]).astype(BF16)


def _dsa_proj(x, g, w_in, gq, gkv, gk, w_uq, w_uk, w_q_idx, tm=256):
    B, T, D = x.shape
    n_in = Q_LORA + KV_LORA + LANES
    o_k = Q_LORA + KV_LORA
    w_in_p = jnp.zeros((D, n_in), F32).at[:, :o_k + IDX_DIM].set(w_in[:, :o_k + IDX_DIM])
    w_wt = w_in[:, o_k + IDX_DIM:].T.astype(BF16)
    w_ukt = jnp.transpose(w_uk, (1, 2, 0)).astype(BF16)
    w_qi = jnp.transpose(w_q_idx, (1, 0, 2)).astype(BF16)
    tile = lambda b, i: (b, i, 0)
    htile = lambda b, i: (b, 0, i, 0)
    return pl.pallas_call(
        _dsa_proj_kernel,
        grid=(B, T // tm),
        in_specs=[
            pl.BlockSpec((1, tm, D), tile),
            _const_spec((1, D)),
            _const_spec((D, n_in)),
            _const_spec((IDX_HEADS, D)),
            _const_spec((1, Q_LORA)), _const_spec((1, KV_LORA)), _const_spec((1, IDX_DIM)),
            _const_spec((Q_LORA, N_HEADS * HEAD_DIM)),
            _const_spec((N_HEADS, HEAD_DIM, KV_LORA)),
            _const_spec((IDX_HEADS, Q_LORA, IDX_DIM)),
        ],
        out_specs=[
            pl.BlockSpec((1, N_HEADS, tm, KV_LORA), htile),
            pl.BlockSpec((1, IDX_HEADS, tm, IDX_DIM), htile),
            pl.BlockSpec((1, tm, KV_LORA), tile),
            pl.BlockSpec((1, tm, IDX_DIM), tile),
            pl.BlockSpec((1, IDX_HEADS, tm), lambda b, i: (b, 0, i)),
        ],
        out_shape=[
            jax.ShapeDtypeStruct((B, N_HEADS, T, KV_LORA), BF16),
            jax.ShapeDtypeStruct((B, IDX_HEADS, T, IDX_DIM), BF16),
            jax.ShapeDtypeStruct((B, T, KV_LORA), BF16),
            jax.ShapeDtypeStruct((B, T, IDX_DIM), BF16),
            jax.ShapeDtypeStruct((B, IDX_HEADS, T), F32),
        ],
        compiler_params=_params(("parallel", "parallel")),
        name="dsa_proj",
    )(x, g.reshape(1, D), w_in_p.astype(BF16), w_wt, gq.reshape(1, -1), gkv.reshape(1, -1), gk.reshape(1, -1),
      w_uq.reshape(Q_LORA, N_HEADS * HEAD_DIM).astype(BF16), w_ukt, w_qi)


def _tile_lanes(a, n):
    return a if n == 1 else jnp.concatenate([a] * n, axis=1)


def _bias_rows(toe_ref, head, tiles, rows):
    parts = [toe_ref[d, head, rows, :] for d in tiles]
    return parts[0] if len(parts) == 1 else jnp.concatenate(parts, axis=1)


def _flash_head(lg_scr, lrow0, toe_ref, tiles, head, m_scr, acc_scr, p_scr, prow0, tq):
    n_rep = len(tiles)
    tk = n_rep * LANES
    for r in range(tq // ROWS):
        rows = slice(r * ROWS, (r + 1) * ROWS)
        lg = lg_scr[lrow0 + r * ROWS:lrow0 + (r + 1) * ROWS, 0:tk] + _bias_rows(toe_ref, head, tiles, rows)
        m_old = m_scr[head, rows]
        m_new = jnp.maximum(m_old, jnp.max(lg, axis=1, keepdims=True))
        m_scr[head, rows] = m_new
        p_scr[prow0 + r * ROWS:prow0 + (r + 1) * ROWS, 0:tk] = jnp.exp2(lg - _tile_lanes(m_new, n_rep)).astype(BF16)
        arows = slice(head * tq + r * ROWS, head * tq + (r + 1) * ROWS)
        acc_scr[arows, :] = acc_scr[arows, :] * _tile_lanes(jnp.exp2(m_old - m_new), 2)


def _with_ones(v):
    return jnp.concatenate([v, jnp.ones(v.shape, v.dtype)], axis=1)


def _flash_groups(nch, qx_scr, k_aug, v_aug, tile_ids, toe_ref, lg_scrs, p_scrs, m_scr, acc_scr, tq):
    R = HEADS_PER_GROUP
    grows = [slice(g * R * tq, (g + 1) * R * tq) for g in range(N_KV_GROUPS)]

    def logits(c, g):
        k = k_aug(c, g)
        lg_scrs[g][:, 0:k.shape[0]] = _dot_nt(qx_scr[grows[g], :], k)

    def softmax(c, g):
        tiles = tile_ids(c)
        for rr in range(R):
            _flash_head(lg_scrs[g], rr * tq, toe_ref, tiles, g * R + rr, m_scr, acc_scr, p_scrs[g], rr * tq, tq)

    def values(c, g):
        v = v_aug(c, g)
        acc_scr[grows[g], :] = acc_scr[grows[g], :] + _dot(p_scrs[g][:, 0:v.shape[0]], v)

    m_scr[...] = jnp.full(m_scr.shape, NEG, F32)
    acc_scr[...] = jnp.zeros(acc_scr.shape, F32)
    p_scrs[1][...] = jnp.zeros(p_scrs[1].shape, BF16)
    logits(0, 0)

    def chunk(c, carry):
        logits(c, 1)
        softmax(c, 0)
        values(jnp.maximum(c - 1, 0), 1)
        logits(jnp.minimum(c + 1, nch - 1), 0)
        softmax(c, 1)
        values(c, 0)
        return carry

    lax.fori_loop(0, nch, chunk, 0)
    values(nch - 1, 1)


def _masked_softmax_rows(lg, mask):
    lg = jnp.where(mask, lg, NEG)
    e = jnp.where(mask, jnp.exp2(lg - jnp.max(lg, axis=1, keepdims=True)), 0.0)
    return e * (1.0 / jnp.maximum(jnp.sum(e, axis=1, keepdims=True), 1e-30))


def _fold_rows(a, n, op=jnp.add):
    parts = [a[i:i + n] for i in range(0, a.shape[0], n)]
    while len(parts) > 1:
        parts = [op(parts[i], parts[i + 1]) for i in range(0, len(parts), 2)]
    return parts[0]


def _from_key(k):
    return lax.bitcast_convert_type(k ^ ((k >> 31) & jnp.int32(0x7FFFFFFF)), F32)


def _dsa_attn_kernel(qi_ref, wi_ref, kidx_ref, qa_ref, ckv_ref, toe_ref, o_ref,
                     s_scr, rel_scr, lg0_scr, lg1_scr, p0_scr, p1_scr, qx_scr, m_scr, acc_scr,
                     *, tq, tk, top_k, nd, idx_bits):
    t0 = pl.program_id(1) * tq
    nch = (t0 + tq + tk - 1) // tk
    n_sub = tk // LANES
    sub = 8
    key_row = lax.broadcasted_iota(jnp.int32, (tk, tq), 0)

    wi = wi_ref[0]
    qi = qi_ref[0].reshape(IDX_HEADS * tq, IDX_DIM)
    k_slab = lax.broadcasted_iota(jnp.int32, (KEY_ROWS, tq), 0)
    q_slab = lax.broadcasted_iota(jnp.int32, (KEY_ROWS, tq), 1)

    def score_chunk(c, top):
        s0 = pl.multiple_of(c * tk, tk)
        rel_scr[...] = _dot_nt(kidx_ref[0, pl.ds(s0, tk), :], qi)
        for r in range(tk // KEY_ROWS):
            rows = slice(r * KEY_ROWS, (r + 1) * KEY_ROWS)
            acc = jnp.zeros((KEY_ROWS, tq), F32)
            for hh in range(IDX_HEADS):
                acc = acc + wi[hh:hh + 1, :] * jnp.maximum(rel_scr[rows, hh * tq:(hh + 1) * tq], 0.0)
            acc = acc + 0.0
            valid = k_slab + (s0 + r * KEY_ROWS) <= q_slab + t0
            s_scr[c, rows, :] = jnp.where(valid, _to_key(acc), INT_MIN)
            top = jnp.maximum(top, _fold_rows(jnp.where(valid, jnp.abs(acc), 0.0), sub, jnp.maximum))
        return top

    top = lax.fori_loop(0, nch, score_chunk, jnp.zeros((sub, tq), F32))
    top = jnp.max(top, axis=0, keepdims=True)

    def count(pred):
        def body(c, cnt):
            return cnt + _fold_rows(jnp.where(pred(s_scr[c], key_row + c * tk), 1.0, 0.0), sub)
        cnt = lax.fori_loop(0, nch, body, jnp.zeros((sub, tq), F32))
        return jnp.sum(cnt, axis=0, keepdims=True)

    kf = jnp.float32(top_k)

    def settled(lo, hi, n_lo):
        return (n_lo <= kf) | (hi - 1 <= lo)

    def probe(lo, hi, n_lo, key_space):
        done = settled(lo, hi, n_lo)
        lo_v, hi_v = _from_key(lo), _from_key(hi)
        mid_v = _to_key(lo_v + (hi_v - lo_v) * 0.5 + 0.0)
        mid_k = (lo >> 1) + (hi >> 1) + (lo & hi & 1)
        cand = jnp.where((mid_v <= lo) | (mid_v >= hi) | key_space, mid_k, mid_v)
        cand = jnp.where(done, lo, cand)
        cnt = count(lambda blk, idx: blk >= cand)
        up = (cnt >= kf) & ~done
        return jnp.where(up, cand, lo), jnp.where(up | done, hi, cand), jnp.where(up, cnt, n_lo)

    def bisect_cond(st):
        return (st[0] < VALUE_STEPS + 34) & (st[1] > 0.0)

    def bisect_pair(st):
        it, _, lo, hi, n_lo = st
        key_space = it >= VALUE_STEPS
        lo, hi, n_lo = probe(lo, hi, n_lo, key_space)
        lo, hi, n_lo = probe(lo, hi, n_lo, key_space)
        return it + 2, jnp.sum(jnp.where(settled(lo, hi, n_lo), 0.0, 1.0)), lo, hi, n_lo

    lo0 = _to_key(-top)
    hi0 = _to_key(top) + 1
    n0 = (t0 + 1 + lax.broadcasted_iota(jnp.int32, (1, tq), 1)).astype(F32)
    _, _, thr, _, n_ge = lax.while_loop(bisect_cond, bisect_pair,
                                        (jnp.int32(0), jnp.float32(1.0), lo0, hi0, n0))

    @pl.when(jnp.max(n_ge) > kf)
    def _():
        need = kf - count(lambda blk, idx: blk > thr)

        def idx_bit(bi, last):
            cand = last | lax.shift_left(jnp.int32(1), idx_bits - 1 - bi)
            cnt = count(lambda blk, idx: (blk == thr) & (idx < cand))
            return jnp.where(cnt < need, cand, last)

        last = lax.fori_loop(0, idx_bits, idx_bit, jnp.zeros((1, tq), jnp.int32))

        def drop(c, carry):
            blk = s_scr[c]
            s_scr[c] = jnp.where((blk == thr) & (key_row + c * tk > last), INT_MIN, blk)
            return carry

        lax.fori_loop(0, nch, drop, 0)

    eye = (lax.broadcasted_iota(jnp.int32, (tq, tq), 0) == lax.broadcasted_iota(jnp.int32, (tq, tq), 1))
    qx_scr[:, 0:KV_LORA] = qa_ref[0].reshape(N_HEADS * tq, KV_LORA)
    for hh in range(N_HEADS):
        qx_scr[hh * tq:(hh + 1) * tq, KV_LORA:KV_LORA + tq] = jnp.where(eye, 1.0, 0.0).astype(BF16)

    def latents(c):
        return ckv_ref[0, pl.ds(pl.multiple_of(c * tk, tk), tk), :]

    def k_aug(c, g):
        mask_t = jnp.where(s_scr[c] >= thr, 0.0, NEG).astype(BF16)
        return jnp.concatenate([latents(c), mask_t], axis=1)

    _flash_groups(nch, qx_scr, k_aug, lambda c, g: _with_ones(latents(c)),
                  lambda c: _tile_ids((t0 - c * tk) // LANES, n_sub, nd),
                  toe_ref, (lg0_scr, lg1_scr), (p0_scr, p1_scr), m_scr, acc_scr, tq)
    for hh in range(N_HEADS):
        rows = slice(hh * tq, (hh + 1) * tq)
        o = acc_scr[rows, 0:KV_LORA] / acc_scr[rows, KV_LORA:2 * KV_LORA]
        o_ref[0, :, hh * KV_LORA:(hh + 1) * KV_LORA] = o.astype(o_ref.dtype)


def _dsa_attn(qi, wi, kidx, qa, ckv, toe, tq=128, tk=512):
    B, _, T, _ = qa.shape
    tk = min(tk, T)
    assert tq == LANES
    nd = toe.shape[0] - 2
    top_k = min(IDX_TOPK_MAX, T // 4)
    kern = functools.partial(_dsa_attn_kernel, tq=tq, tk=tk, top_k=top_k, nd=nd,
                             idx_bits=max(1, (T - 1).bit_length()))
    return pl.pallas_call(
        kern,
        grid=(B, T // tq),
        in_specs=[
            pl.BlockSpec((1, IDX_HEADS, tq, IDX_DIM), lambda b, i: (b, 0, i, 0)),
            pl.BlockSpec((1, IDX_HEADS, tq), lambda b, i: (b, 0, i)),
            pl.BlockSpec((1, T, IDX_DIM), lambda b, i: (b, 0, 0), pipeline_mode=pl.Buffered(1)),
            pl.BlockSpec((1, N_HEADS, tq, KV_LORA), lambda b, i: (b, 0, i, 0)),
            pl.BlockSpec((1, T, KV_LORA), lambda b, i: (b, 0, 0), pipeline_mode=pl.Buffered(1)),
            _const_spec(toe.shape),
        ],
        out_specs=pl.BlockSpec((1, tq, N_HEADS * KV_LORA), lambda b, i: (b, i, 0)),
        out_shape=jax.ShapeDtypeStruct((B, T, N_HEADS * KV_LORA), BF16),
        scratch_shapes=[
            pltpu.VMEM((T // tk, tk, tq), jnp.int32),
            pltpu.VMEM((tk, IDX_HEADS * tq), F32),
            pltpu.VMEM((HEADS_PER_GROUP * tq, tk), F32),
            pltpu.VMEM((HEADS_PER_GROUP * tq, tk), F32),
            pltpu.VMEM((HEADS_PER_GROUP * tq, tk), BF16),
            pltpu.VMEM((HEADS_PER_GROUP * tq, tk), BF16),
            pltpu.VMEM((N_HEADS * tq, KV_LORA + tq), BF16),
            pltpu.VMEM((N_HEADS, tq, LANES), F32),
            pltpu.VMEM((N_HEADS * tq, 2 * KV_LORA), F32),
        ],
        compiler_params=_params(("parallel", "arbitrary")),
        name="dsa_attn",
    )(qi, wi, kidx, qa, ckv, toe)


def _dsa_out_kernel(o_ref, x_ref, w_uv_ref, w_o_ref, y_ref):
    o_lat = o_ref[0]
    parts = [_dot(o_lat[:, hh * KV_LORA:(hh + 1) * KV_LORA], w_uv_ref[hh]).astype(BF16)
             for hh in range(N_HEADS)]
    y_ref[0] = x_ref[0] + _dot(jnp.concatenate(parts, axis=1), w_o_ref[...])


def _dsa_out(o_lat, x, w_uv, w_o, tm=512):
    B, T, D = x.shape
    tile = lambda b, i: (b, i, 0)
    return pl.pallas_call(
        _dsa_out_kernel,
        grid=(B, T // tm),
        in_specs=[pl.BlockSpec((1, tm, N_HEADS * KV_LORA), tile), pl.BlockSpec((1, tm, D), tile),
                  _const_spec((N_HEADS, KV_LORA, HEAD_DIM)), _const_spec((N_HEADS * HEAD_DIM, D))],
        out_specs=pl.BlockSpec((1, tm, D), tile),
        out_shape=jax.ShapeDtypeStruct((B, T, D), F32),
        compiler_params=_params(("parallel", "parallel")),
        name="dsa_out",
    )(o_lat, x, jnp.transpose(w_uv, (1, 0, 2)).astype(BF16), w_o.astype(BF16))


def _out_proj_kernel(o_ref, x_ref, w_o_ref, y_ref):
    y_ref[0] = x_ref[0] + _dot(o_ref[0], w_o_ref[...])


def _out_proj(o, x, w_o, tm=512):
    B, T, D = x.shape
    tile = lambda b, i: (b, i, 0)
    return pl.pallas_call(
        _out_proj_kernel,
        grid=(B, T // tm),
        in_specs=[pl.BlockSpec((1, tm, o.shape[-1]), tile), pl.BlockSpec((1, tm, D), tile),
                  _const_spec(w_o.shape)],
        out_specs=pl.BlockSpec((1, tm, D), tile),
        out_shape=jax.ShapeDtypeStruct((B, T, D), F32),
        compiler_params=_params(("parallel", "parallel")),
        name="out_proj",
    )(o, x, w_o.astype(BF16))


def _mlp_kernel(x_ref, g_ref, w_up_ref, w_down_ref, *rest, tf, final):
    y_ref = rest[-1]
    x = x_ref[0]
    h = _rms(x, g_ref[...]).astype(BF16)
    acc = x
    for f0 in range(0, w_up_ref.shape[1], tf):
        u = jnp.maximum(_dot(h, w_up_ref[:, f0:f0 + tf]), 0.0)
        acc = acc + _dot((u * u).astype(BF16), w_down_ref[f0:f0 + tf, :])
    y_ref[0] = _rms(acc, rest[0][...]) if final else acc


def _mlp(x, g, w_up, w_down, g_final=None, tm=512, tf=512):
    B, T, D = x.shape
    F = w_up.shape[1]
    tile = lambda b, i: (b, i, 0)
    final = g_final is not None
    in_specs = [pl.BlockSpec((1, tm, D), tile), _const_spec((1, D)), _const_spec((D, F)), _const_spec((F, D))]
    args = [x, g.reshape(1, D), w_up.astype(BF16), w_down.astype(BF16)]
    if final:
        in_specs.append(_const_spec((1, D)))
        args.append(g_final.reshape(1, D))
    return pl.pallas_call(
        functools.partial(_mlp_kernel, tf=tf, final=final),
        grid=(B, T // tm),
        in_specs=in_specs,
        out_specs=pl.BlockSpec((1, tm, D), tile),
        out_shape=jax.ShapeDtypeStruct((B, T, D), F32),
        compiler_params=_params(("parallel", "parallel")),
        name="mlp_final" if final else "mlp",
    )(*args)


def _kv_proj_kernel(x_ref, g_ref, w_ref, kc_ref, vc_ref, ks_ref, vs_ref, kw_ref, vw_ref):
    h = _rms(x_ref[0], g_ref[...]).astype(BF16)
    kv = _dot(h, w_ref[...])
    for part, ref in enumerate((kc_ref, vc_ref, ks_ref, vs_ref, kw_ref, vw_ref)):
        for g in range(N_KV_GROUPS):
            o = (part * N_KV_GROUPS + g) * HEAD_DIM
            ref[0, g] = kv[:, o:o + HEAD_DIM].astype(ref.dtype)


def _kv_proj(x, g, w_kv, tm=512):
    B, T, D = x.shape
    gtile = lambda b, i: (b, 0, i, 0)
    spec = pl.BlockSpec((1, N_KV_GROUPS, tm, HEAD_DIM), gtile)
    shp = lambda dt: jax.ShapeDtypeStruct((B, N_KV_GROUPS, T, HEAD_DIM), dt)
    return pl.pallas_call(
        _kv_proj_kernel,
        grid=(B, T // tm),
        in_specs=[pl.BlockSpec((1, tm, D), lambda b, i: (b, i, 0)), _const_spec((1, D)),
                  _const_spec(w_kv.shape)],
        out_specs=[spec] * 6,
        out_shape=[shp(F32), shp(F32), shp(BF16), shp(BF16), shp(BF16), shp(BF16)],
        compiler_params=_params(("parallel", "parallel")),
        name="kv_proj",
    )(x, g.reshape(1, D), w_kv.astype(BF16))


def _compress_kernel(raw_ref, pos_ref, w1_ref, w2_ref, o_ref):
    rows = raw_ref[0, 0]
    half = rows.shape[1]
    a = _dot((rows + pos_ref[:, :half]).astype(BF16), w1_ref[:half, :])
    b = _dot((rows + pos_ref[:, half:]).astype(BF16), w1_ref[half:, :])
    pre = a + pltpu.roll(b, rows.shape[0] - 1, 0)
    act = 0.5 * pre * (1.0 + jnp.tanh(math.sqrt(2.0 / math.pi) * (pre + 0.044715 * (pre * pre * pre))))
    o_ref[0, 0] = _dot(act.astype(BF16), w2_ref[...]).astype(o_ref.dtype)


def _compress(raw, pos, w1, w2):
    B, G, T, Dh = raw.shape
    nr = T // CMP_STRIDE
    rows = raw.reshape(B, G, nr, CMP_STRIDE * Dh)
    return pl.pallas_call(
        _compress_kernel,
        grid=(B, G),
        in_specs=[pl.BlockSpec((1, 1, nr, CMP_STRIDE * Dh), lambda b, g: (b, g, 0, 0)),
                  _const_spec((1, CMP_LEN * Dh)), _const_spec(w1.shape), _const_spec(w2.shape)],
        out_specs=pl.BlockSpec((1, 1, nr, Dh), lambda b, g: (b, g, 0, 0)),
        out_shape=jax.ShapeDtypeStruct((B, G, nr, Dh), BF16),
        compiler_params=_params(("parallel", "parallel")),
        name="compress",
    )(rows, pos.reshape(1, CMP_LEN * Dh), w1.astype(BF16), w2.astype(BF16))


def _nsa_proj_kernel(x_ref, g_ref, wq_ref, wg_ref, q_ref, gate_ref):
    h = _rms(x_ref[0], g_ref[...]).astype(BF16)
    q = _dot(h, wq_ref[...]) * (HEAD_DIM ** -0.5 * LOG2E)
    for hh in range(N_HEADS):
        q_ref[0, hh] = q[:, hh * HEAD_DIM:(hh + 1) * HEAD_DIM].astype(BF16)
    gate_ref[0] = jax.nn.sigmoid(_dot(h, wg_ref[...]))[:, :3 * N_HEADS]


def _nsa_proj(x, g, w_in, tm=512):
    B, T, D = x.shape
    hd = N_HEADS * HEAD_DIM
    wq = w_in[:, :hd].astype(BF16)
    wg = jnp.zeros((D, LANES), F32).at[:, :3 * N_HEADS].set(w_in[:, hd:]).astype(BF16)
    return pl.pallas_call(
        _nsa_proj_kernel,
        grid=(B, T // tm),
        in_specs=[pl.BlockSpec((1, tm, D), lambda b, i: (b, i, 0)), _const_spec((1, D)),
                  _const_spec((D, hd)), _const_spec((D, LANES))],
        out_specs=[pl.BlockSpec((1, N_HEADS, tm, HEAD_DIM), lambda b, i: (b, 0, i, 0)),
                   pl.BlockSpec((1, tm, 3 * N_HEADS), lambda b, i: (b, i, 0))],
        out_shape=[jax.ShapeDtypeStruct((B, N_HEADS, T, HEAD_DIM), BF16),
                   jax.ShapeDtypeStruct((B, T, 3 * N_HEADS), F32)],
        compiler_params=_params(("parallel", "parallel")),
        name="nsa_proj",
    )(x, g.reshape(1, D), wq, wg)


def _top_blocks(imp, blk, n_sel, exact_ties):
    sel = jnp.zeros(imp.shape, F32)
    blk_f = blk.astype(F32)
    for _ in range(n_sel):
        if exact_ties:
            best = jnp.max(imp, axis=1, keepdims=True)
            hit = blk_f == jnp.min(jnp.where(imp == best, blk_f, float(LANES)), axis=1, keepdims=True)
        else:
            hit = blk == jnp.argmax(imp, axis=1, keepdims=True).astype(jnp.int32)
        sel = jnp.where(hit, 1.0, sel)
        imp = jnp.where(hit, -jnp.inf, imp)
    return sel


def _nsa_attn_kernel(q_ref, gate_ref, kc_ref, vc_ref, ks_ref, vs_ref, kw_ref, vw_ref, selmap_ref, toe_ref,
                     o_ref, sel_scr, imp_scr, lg0_scr, lg1_scr, p0_scr, p1_scr, lw_scr, pw_scr, psum_scr, qx_scr, m_scr, acc_scr,
                     ocw_scr, *, tq, tk, nd, n_slc, n_sel):
    G, R = N_KV_GROUPS, HEADS_PER_GROUP
    lg_scrs, p_scrs = (lg0_scr, lg1_scr), (p0_scr, p1_scr)
    t0 = pl.program_id(1) * tq
    n_cmp = kc_ref.shape[2]
    n_sub = tk // LANES
    wlen = WIN + tq
    gates = gate_ref[0]
    t_col = t0 + lax.broadcasted_iota(jnp.int32, (tq, 1), 0)
    t_rows = t0 + lax.broadcasted_iota(jnp.int32, (ROWS, 1), 0)
    slabs = [slice(r * ROWS, (r + 1) * ROWS) for r in range(tq // ROWS)]

    cmp_end = CMP_STRIDE * lax.broadcasted_iota(jnp.int32, (ROWS, n_cmp), 1) + (CMP_LEN - 1)
    blk = lax.broadcasted_iota(jnp.int32, (tq, LANES), 1)
    cur = t_col // SLC_BLK
    forced = (blk == 0) | (blk == cur) | (blk == cur - 1)
    future = blk * SLC_BLK > t_col
    start = pl.multiple_of(jnp.maximum(t0 - WIN, 0), LANES)
    dw = (t0 - start) // LANES
    wtiles = _tile_ids(dw, wlen // LANES, nd)
    wtiles[0] = jnp.where(dw == WIN // LANES, nd + 1, wtiles[0])
    open_ties = []
    for g in range(G):
        lg_scr, p_scr = lg_scrs[g], p_scrs[g]
        qg = q_ref[0, g * R:(g + 1) * R].reshape(R * tq, HEAD_DIM)
        lg_scr[0:R * tq, 0:n_cmp] = _dot_nt(qg, kc_ref[0, g])
        for r, rows in enumerate(slabs):
            cmask = cmp_end <= t_rows + r * ROWS
            pc_sum = jnp.zeros((ROWS, n_cmp), F32)
            for rr in range(R):
                pc = _masked_softmax_rows(lg_scr[rr * tq + r * ROWS:rr * tq + (r + 1) * ROWS, 0:n_cmp], cmask)
                pc_sum = pc_sum + pc
                p_scr[rr * tq + r * ROWS:rr * tq + (r + 1) * ROWS, 0:n_cmp] = pc.astype(BF16)
            psum_scr[rows, :] = pc_sum
        o_cmp = _dot(p_scr[:, 0:n_cmp], vc_ref[0, g])

        lw_scr[0:R * tq, 0:wlen] = _dot_nt(qg, kw_ref[0, g, pl.ds(start, wlen), :])
        for rr in range(R):
            for r, rows in enumerate(slabs):
                lw = (lw_scr[rr * tq + r * ROWS:rr * tq + (r + 1) * ROWS, 0:wlen]
                      + _bias_rows(toe_ref, g * R + rr, wtiles, rows))
                pw = jnp.exp2(lw - jnp.max(lw, axis=1, keepdims=True))
                pw_scr[rr * tq + r * ROWS:rr * tq + (r + 1) * ROWS, 0:wlen] = pw.astype(BF16)
        o_win = _dot(pw_scr[:, 0:wlen], _with_ones(vw_ref[0, g, pl.ds(start, wlen), :]))
        for rr in range(R):
            hh = g * R + rr
            wrows = slice(rr * tq, (rr + 1) * tq)
            ocw_scr[hh * tq:(hh + 1) * tq, :] = (
                gates[:, 3 * hh:3 * hh + 1] * o_cmp[wrows]
                + gates[:, 3 * hh + 2:3 * hh + 3] * (o_win[wrows, 0:HEAD_DIM] / o_win[wrows, HEAD_DIM:2 * HEAD_DIM]))
        pc_sum = psum_scr[...]
        hi = pc_sum.astype(BF16)
        lo = (pc_sum - hi.astype(F32)).astype(BF16)
        imp = _dot(hi, selmap_ref[...]) + _dot(lo, selmap_ref[...])
        imp = jnp.where(forced, 1e9, imp)
        imp = jnp.where(future, NEG, imp)
        imp = jnp.where(blk < n_slc, imp, -jnp.inf)
        sel = _top_blocks(imp, blk, n_sel, exact_ties=False)
        sel_scr[g] = sel
        imp_scr[g] = imp
        cut = jnp.min(jnp.where(sel > 0.0, imp, jnp.inf), axis=1, keepdims=True)
        open_ties.append(jnp.max(jnp.where((imp == cut) & (sel == 0.0) & ~future, 1.0, 0.0)))
        qx_scr[g * R * tq:(g + 1) * R * tq, 0:HEAD_DIM] = qg

    @pl.when(jnp.maximum(*open_ties) > 0.0)
    def _():
        for g in range(G):
            sel_scr[g] = _top_blocks(imp_scr[g], blk, n_sel, exact_ties=True)

    for g in range(G):
        block_bias = jnp.where((sel_scr[g] > 0.0) & ~future, 0.0, NEG).astype(BF16)
        for rr in range(R):
            qx_scr[(g * R + rr) * tq:(g * R + rr + 1) * tq, HEAD_DIM:HEAD_DIM + LANES] = block_bias

    nch = (t0 + tq + tk - 1) // tk
    lane_minus_block = (lax.broadcasted_iota(jnp.int32, (tk, LANES), 1)
                        - lax.broadcasted_iota(jnp.int32, (tk, LANES), 0) // SLC_BLK)

    def k_aug(c, g):
        s0 = pl.multiple_of(c * tk, tk)
        block_of_key = jnp.where(lane_minus_block == s0 // SLC_BLK, 1.0, 0.0).astype(BF16)
        return jnp.concatenate([ks_ref[0, g, pl.ds(s0, tk), :], block_of_key], axis=1)

    _flash_groups(nch, qx_scr, k_aug, lambda c, g: _with_ones(vs_ref[0, g, pl.ds(pl.multiple_of(c * tk, tk), tk), :]),
                  lambda c: _tile_ids((t0 - c * tk) // LANES, n_sub, nd),
                  toe_ref, lg_scrs, p_scrs, m_scr, acc_scr, tq)

    for hh in range(N_HEADS):
        rows = slice(hh * tq, (hh + 1) * tq)
        o_slc = acc_scr[rows, 0:HEAD_DIM] / acc_scr[rows, HEAD_DIM:2 * HEAD_DIM]
        o = ocw_scr[rows, :] + gates[:, 3 * hh + 1:3 * hh + 2] * o_slc
        o_ref[0, :, hh * HEAD_DIM:(hh + 1) * HEAD_DIM] = o.astype(o_ref.dtype)


def _selection_map(n_cmp, n_slc):
    c0 = CMP_STRIDE * np.arange(n_cmp)[:, None]
    s0 = SLC_BLK * np.arange(n_slc)[None, :]
    ov = np.clip(np.minimum(c0 + CMP_LEN, s0 + SLC_BLK) - np.maximum(c0, s0), 0, None)
    return (ov / CMP_LEN).astype(np.float32)


def _nsa_attn(q, gates, kc, vc, ks, vs, kw, vw, toe, tq=128, tk=512):
    B, _, T, _ = q.shape
    tk = min(tk, T)
    assert tq == LANES
    nd = toe.shape[0] - 2
    n_cmp_pad = kc.shape[2]
    n_cmp = (T - CMP_LEN) // CMP_STRIDE + 1
    n_slc = T // SLC_BLK
    assert n_slc <= LANES and T >= WIN + tq
    selmap = np.zeros((n_cmp_pad, LANES), np.float32)
    selmap[:n_cmp, :n_slc] = _selection_map(n_cmp, n_slc)
    kern = functools.partial(_nsa_attn_kernel, tq=tq, tk=tk, nd=nd, n_slc=n_slc, n_sel=min(N_SLC_MAX, n_slc))
    G, R = N_KV_GROUPS, HEADS_PER_GROUP
    width = max(n_cmp_pad, tk, WIN + tq)
    res = lambda n: pl.BlockSpec((1, G, n, HEAD_DIM), lambda b, i: (b, 0, 0, 0), pipeline_mode=pl.Buffered(1))
    return pl.pallas_call(
        kern,
        grid=(B, T // tq),
        in_specs=[
            pl.BlockSpec((1, N_HEADS, tq, HEAD_DIM), lambda b, i: (b, 0, i, 0)),
            pl.BlockSpec((1, tq, 3 * N_HEADS), lambda b, i: (b, i, 0)),
            res(n_cmp_pad), res(n_cmp_pad), res(T), res(T), res(T), res(T),
            _const_spec(selmap.shape),
            _const_spec(toe.shape),
        ],
        out_specs=pl.BlockSpec((1, tq, N_HEADS * HEAD_DIM), lambda b, i: (b, i, 0)),
        out_shape=jax.ShapeDtypeStruct((B, T, N_HEADS * HEAD_DIM), BF16),
        scratch_shapes=[
            pltpu.VMEM((G, tq, LANES), F32),
            pltpu.VMEM((G, tq, LANES), F32),
            pltpu.VMEM((R * tq, width), F32),
            pltpu.VMEM((R * tq, width), F32),
            pltpu.VMEM((R * tq, width), BF16),
            pltpu.VMEM((R * tq, width), BF16),
            pltpu.VMEM((R * tq, WIN + tq), F32),
            pltpu.VMEM((R * tq, WIN + tq), BF16),
            pltpu.VMEM((tq, n_cmp_pad), F32),
            pltpu.VMEM((N_HEADS * tq, HEAD_DIM + LANES), BF16),
            pltpu.VMEM((N_HEADS, tq, LANES), F32),
            pltpu.VMEM((N_HEADS * tq, 2 * HEAD_DIM), F32),
            pltpu.VMEM((N_HEADS * tq, HEAD_DIM), F32),
        ],
        compiler_params=_params(("parallel", "arbitrary")),
        name="nsa_attn",
    )(q, gates, kc, vc, ks, vs, kw, vw, jnp.asarray(selmap, BF16), toe)


def _dsa_layer(x, toe, g_attn, w_in, gq, gkv, gk, w_uq, w_q_idx, w_uk, w_uv, w_o):
    qa, qi, ckv, kidx, wi = _dsa_proj(x, g_attn, w_in, gq, gkv, gk, w_uq, w_uk, w_q_idx)
    o_lat = _dsa_attn(qi, wi, kidx, qa, ckv, toe)
    return _dsa_out(o_lat, x, w_uv, w_o)


def _shared_kv(x, g_kv, w_kv, pos_k, pos_v, w1_k, w2_k, w1_v, w2_v):
    kc_raw, vc_raw, ks, vs, kw, vw = _kv_proj(x, g_kv, w_kv)
    return _compress(kc_raw, pos_k, w1_k, w2_k), _compress(vc_raw, pos_v, w1_v, w2_v), ks, vs, kw, vw


def _nsa_layer(x, toe, kv_shared, g_attn, w_in, w_o):
    q, gates = _nsa_proj(x, g_attn, w_in)
    return _out_proj(_nsa_attn(q, gates, *kv_shared, toe), x, w_o)


def kernel(x, g_attn, g_mlp, w_up, w_down, rel_bias, a_w_in, a_g_q_lat, a_g_kv_lat, a_g_k_idx, a_w_uq, a_w_q_idx, a_w_uk, a_w_uv, a_w_o, g_kv_shared, w_kv_shared, cmp_pos_k, cmp_pos_v, cmp_w1_k, cmp_w2_k, cmp_w1_v, cmp_w2_v, b_w_in, b_w_o, g_final):
    depth = g_attn.shape[0]
    n_a = a_w_in.shape[0]
    toe = _bias_tiles(rel_bias, x.shape[1])
    kv_shared = None
    for l in range(depth):
        if l < n_a:
            x = _dsa_layer(x, toe, g_attn[l], a_w_in[l], a_g_q_lat[l], a_g_kv_lat[l], a_g_k_idx[l],
                           a_w_uq[l], a_w_q_idx[l], a_w_uk[l], a_w_uv[l], a_w_o[l])
        else:
            j = l - n_a
            x = _nsa_layer(x, toe, kv_shared, g_attn[l], b_w_in[j], b_w_o[j])
        x = _mlp(x, g_mlp[l], w_up[l], w_down[l], g_final if l == depth - 1 else None)
        if l == n_a - 1:
            kv_shared = _shared_kv(x, g_kv_shared, w_kv_shared, cmp_pos_k, cmp_pos_v,
                                   cmp_w1_k, cmp_w2_k, cmp_w1_v, cmp_w2_v)
    return x
```

```python
import functools
import math

import numpy as np
import jax
import jax.numpy as jnp
from jax import lax
from jax.experimental import pallas as pl
from jax.experimental.pallas import tpu as pltpu

N_HEADS = 8
HEAD_DIM = 128
Q_LORA = 256
KV_LORA = 128
IDX_HEADS = 8
IDX_DIM = 64
IDX_TOPK_MAX = 256
N_KV_GROUPS = 2
HEADS_PER_GROUP = N_HEADS // N_KV_GROUPS
CMP_LEN = 32
CMP_STRIDE = 16
CMP_HID = 256
SLC_BLK = 64
N_SLC_MAX = 16
WIN = 512
REL_BUCKETS = 32
REL_MAX_DIST = 4096
EPS = 1e-6
NEG = -1e30
LOG2E = math.log2(math.e)

LANES = 128
ROWS = 16
KEY_ROWS = 64
INT_MIN = np.int32(-2 ** 31)
INT_MAX = np.int32(2 ** 31 - 1)
FIXED_PROBES = 15
STRAGGLER_PROBES = 40
VMEM_LIMIT = 56 * 1024 * 1024

F32 = jnp.float32
BF16 = jnp.bfloat16
NT_DIMS = (((1,), (1,)), ((), ()))


def _dot(a, b):
    return jnp.dot(a, b, preferred_element_type=F32)


def _dot_nt(a, b):
    return lax.dot_general(a, b, NT_DIMS, preferred_element_type=F32)


def _rms(x, g):
    return x * lax.rsqrt(jnp.mean(x * x, axis=-1, keepdims=True) + EPS) * g


def _to_key(v):
    bits = lax.bitcast_convert_type(v, jnp.int32)
    return bits ^ ((bits >> 31) & jnp.int32(0x7FFFFFFF))


def _const_spec(shape):
    nd = len(shape)
    return pl.BlockSpec(shape, lambda *_: (0,) * nd, pipeline_mode=pl.Buffered(1))


def _params(sem):
    return pltpu.CompilerParams(dimension_semantics=sem, vmem_limit_bytes=VMEM_LIMIT)


def _rel_bucket(dist):
    dist = jnp.maximum(dist, 0)
    exact = REL_BUCKETS // 2
    log_ratio = jnp.log(jnp.maximum(dist, 1).astype(F32) / exact) / math.log(REL_MAX_DIST / exact)
    large = exact + (log_ratio * (REL_BUCKETS - exact)).astype(jnp.int32)
    return jnp.where(dist < exact, dist, jnp.minimum(large, REL_BUCKETS - 1))


def _num_bias_tiles(T):
    exact = REL_BUCKETS // 2
    switch = exact * (REL_MAX_DIST / exact) ** ((REL_BUCKETS - exact - 1) / (REL_BUCKETS - exact))
    far = int(math.ceil(switch)) + 32
    return min(T // LANES, -(-(far + LANES - 1) // LANES) + 1)


def _bias_tiles(rel_bias, T):
    nd = _num_bias_tiles(T)
    assert nd > WIN // LANES
    c = LANES * (nd - 1)
    bd = rel_bias[_rel_bucket(jnp.arange(c + LANES, dtype=jnp.int32))].T.astype(F32) * LOG2E
    p = c + 2 * LANES
    w = jnp.concatenate([bd[:, c::-1], jnp.broadcast_to(bd[:, :1], (N_HEADS, LANES)), bd[:, :c:-1]], axis=1)
    band = jnp.tile(w, (1, LANES))[:, :LANES * (p - 1)].reshape(N_HEADS, LANES, p - 1)[:, :, :c + LANES]
    tiles = band.reshape(N_HEADS, LANES, nd, LANES)[:, :, ::-1]
    tiles = jnp.transpose(tiles, (2, 0, 1, 3))
    i = jnp.arange(LANES)[:, None]
    j = jnp.arange(LANES)[None, :]
    ahead = jnp.full((1,) + tiles.shape[1:], NEG, F32)
    diag = jnp.where(i >= j, tiles[:1], NEG)
    win_edge = jnp.where(i < j, tiles[WIN // LANES:WIN // LANES + 1], NEG)
    return jnp.concatenate([ahead, diag, tiles[1:], win_edge], axis=0)


def _tile_ids(d0, n_sub, nd):
    return [jnp.clip(d0 - k, -1, nd - 1) + 1 for k in range(n_sub)]


def _dsa_proj_kernel(x_ref, g_ref, w_in_ref, w_wt_ref, gq_ref, gkv_ref, gk_ref, w_uq_ref, w_ukt_ref, w_qi_ref,
                     qa_ref, qi_ref, ckv_ref, kidx_ref, wi_ref):
    h = _rms(x_ref[0], g_ref[...]).astype(BF16)
    proj = _dot(h, w_in_ref[...])
    c_q = _rms(proj[:, :Q_LORA], gq_ref[...]).astype(BF16)
    ckv_ref[0] = _rms(proj[:, Q_LORA:Q_LORA + KV_LORA], gkv_ref[...]).astype(BF16)
    o_k = Q_LORA + KV_LORA
    kidx_ref[0] = _rms(proj[:, o_k:o_k + IDX_DIM], gk_ref[...]).astype(BF16)
    wi_ref[0] = _dot_nt(w_wt_ref[...], h) * (IDX_HEADS ** -0.5 * IDX_DIM ** -0.5)
    q = _dot(c_q, w_uq_ref[...])
    for hh in range(N_HEADS):
        qh = q[:, hh * HEAD_DIM:(hh + 1) * HEAD_DIM].astype(BF16)
        qa_ref[0, hh] = (_dot(qh, w_ukt_ref[hh]) * (HEAD_DIM ** -0.5 * LOG2E)).astype(BF16)
        qi_ref[0, hh] = _dot(c_q, w_qi_ref[hh]).astype(BF16)


def _dsa_proj(x, g, w_in, gq, gkv, gk, w_uq, w_uk, w_q_idx, tm=256):
    B, T, D = x.shape
    n_in = Q_LORA + KV_LORA + LANES
    o_k = Q_LORA + KV_LORA
    w_in_p = jnp.zeros((D, n_in), F32).at[:, :o_k + IDX_DIM].set(w_in[:, :o_k + IDX_DIM])
    w_wt = w_in[:, o_k + IDX_DIM:].T.astype(BF16)
    w_ukt = jnp.transpose(w_uk, (1, 2, 0)).astype(BF16)
    w_qi = jnp.transpose(w_q_idx, (1, 0, 2)).astype(BF16)
    tile = lambda b, i: (b, i, 0)
    htile = lambda b, i: (b, 0, i, 0)
    return pl.pallas_call(
        _dsa_proj_kernel,
        grid=(B, T // tm),
        in_specs=[
            pl.BlockSpec((1, tm, D), tile),
            _const_spec((1, D)),
            _const_spec((D, n_in)),
            _const_spec((IDX_HEADS, D)),
            _const_spec((1, Q_LORA)), _const_spec((1, KV_LORA)), _const_spec((1, IDX_DIM)),
            _const_spec((Q_LORA, N_HEADS * HEAD_DIM)),
            _const_spec((N_HEADS, HEAD_DIM, KV_LORA)),
            _const_spec((IDX_HEADS, Q_LORA, IDX_DIM)),
        ],
        out_specs=[
            pl.BlockSpec((1, N_HEADS, tm, KV_LORA), htile),
            pl.BlockSpec((1, IDX_HEADS, tm, IDX_DIM), htile),
            pl.BlockSpec((1, tm, KV_LORA), tile),
            pl.BlockSpec((1, tm, IDX_DIM), tile),
            pl.BlockSpec((1, IDX_HEADS, tm), lambda b, i: (b, 0, i)),
        ],
        out_shape=[
            jax.ShapeDtypeStruct((B, N_HEADS, T, KV_LORA), BF16),
            jax.ShapeDtypeStruct((B, IDX_HEADS, T, IDX_DIM), BF16),
            jax.ShapeDtypeStruct((B, T, KV_LORA), BF16),
            jax.ShapeDtypeStruct((B, T, IDX_DIM), BF16),
            jax.ShapeDtypeStruct((B, IDX_HEADS, T), F32),
        ],
        compiler_params=_params(("parallel", "parallel")),
        name="dsa_proj",
    )(x, g.reshape(1, D), w_in_p.astype(BF16), w_wt, gq.reshape(1, -1), gkv.reshape(1, -1), gk.reshape(1, -1),
      w_uq.reshape(Q_LORA, N_HEADS * HEAD_DIM).astype(BF16), w_ukt, w_qi)


def _tile_lanes(a, n):
    return a if n == 1 else jnp.concatenate([a] * n, axis=1)


def _bias_rows(toe_ref, head, tiles, rows):
    parts = [toe_ref[d, head, rows, :] for d in tiles]
    return parts[0] if len(parts) == 1 else jnp.concatenate(parts, axis=1)


def _flash_head(lg_scr, lrow0, toe_ref, tiles, head, m_scr, acc_scr, p_scr, prow0, tq):
    n_rep = len(tiles)
    tk = n_rep * LANES
    for r in range(tq // ROWS):
        rows = slice(r * ROWS, (r + 1) * ROWS)
        lg = lg_scr[lrow0 + r * ROWS:lrow0 + (r + 1) * ROWS, 0:tk] + _bias_rows(toe_ref, head, tiles, rows)
        m_old = m_scr[head, rows]
        m_new = jnp.maximum(m_old, jnp.max(lg, axis=1, keepdims=True))
        m_scr[head, rows] = m_new
        p_scr[prow0 + r * ROWS:prow0 + (r + 1) * ROWS, 0:tk] = jnp.exp2(lg - _tile_lanes(m_new, n_rep)).astype(BF16)
        arows = slice(head * tq + r * ROWS, head * tq + (r + 1) * ROWS)
        acc_scr[arows, :] = acc_scr[arows, :] * _tile_lanes(jnp.exp2(m_old - m_new), 2)


def _with_ones(v):
    return jnp.concatenate([v, jnp.ones(v.shape, v.dtype)], axis=1)


def _flash_groups(nch, qx_scr, k_aug, v_aug, tile_ids, toe_ref, lg_scrs, p_scrs, m_scr, acc_scr, tq):
    R = HEADS_PER_GROUP
    grows = [slice(g * R * tq, (g + 1) * R * tq) for g in range(N_KV_GROUPS)]

    def logits(c, g):
        k = k_aug(c, g)
        lg_scrs[g][:, 0:k.shape[0]] = _dot_nt(qx_scr[grows[g], :], k)

    def softmax(c, g):
        tiles = tile_ids(c)
        for rr in range(R):
            _flash_head(lg_scrs[g], rr * tq, toe_ref, tiles, g * R + rr, m_scr, acc_scr, p_scrs[g], rr * tq, tq)

    def values(c, g):
        v = v_aug(c, g)
        acc_scr[grows[g], :] = acc_scr[grows[g], :] + _dot(p_scrs[g][:, 0:v.shape[0]], v)

    m_scr[...] = jnp.full(m_scr.shape, NEG, F32)
    acc_scr[...] = jnp.zeros(acc_scr.shape, F32)
    p_scrs[1][...] = jnp.zeros(p_scrs[1].shape, BF16)
    logits(0, 0)

    def chunk(c, carry):
        logits(c, 1)
        softmax(c, 0)
        values(jnp.maximum(c - 1, 0), 1)
        logits(jnp.minimum(c + 1, nch - 1), 0)
        softmax(c, 1)
        values(c, 0)
        return carry

    lax.fori_loop(0, nch, chunk, 0)
    values(nch - 1, 1)


def _masked_softmax_rows(lg, mask):
    lg = jnp.where(mask, lg, NEG)
    e = jnp.where(mask, jnp.exp2(lg - jnp.max(lg, axis=1, keepdims=True)), 0.0)
    return e * (1.0 / jnp.maximum(jnp.sum(e, axis=1, keepdims=True), 1e-30))


def _fold_rows(a, n, op=jnp.add):
    parts = [a[i:i + n] for i in range(0, a.shape[0], n)]
    while len(parts) > 1:
        parts = [op(parts[i], parts[i + 1]) for i in range(0, len(parts), 2)]
    return parts[0]


def _from_key(k):
    return lax.bitcast_convert_type(k ^ ((k >> 31) & jnp.int32(0x7FFFFFFF)), F32)


def _dsa_attn_kernel(qi_ref, wi_ref, kidx_ref, qa_ref, ckv_ref, toe_ref, o_ref,
                     s_scr, rel_scr, lg0_scr, lg1_scr, p0_scr, p1_scr, qx_scr, m_scr, acc_scr,
                     *, tq, tk, top_k, nd, idx_bits):
    t0 = pl.program_id(1) * tq
    nch = (t0 + tq + tk - 1) // tk
    n_sub = tk // LANES
    sub = 8
    key_row = lax.broadcasted_iota(jnp.int32, (tk, tq), 0)

    wi = wi_ref[0]
    qi = qi_ref[0].reshape(IDX_HEADS * tq, IDX_DIM)
    k_slab = lax.broadcasted_iota(jnp.int32, (KEY_ROWS, tq), 0)
    q_slab = lax.broadcasted_iota(jnp.int32, (KEY_ROWS, tq), 1)

    def score_chunk(c, top):
        s0 = pl.multiple_of(c * tk, tk)
        rel_scr[...] = _dot_nt(kidx_ref[0, pl.ds(s0, tk), :], qi)
        for r in range(tk // KEY_ROWS):
            rows = slice(r * KEY_ROWS, (r + 1) * KEY_ROWS)
            acc = jnp.zeros((KEY_ROWS, tq), F32)
            for hh in range(IDX_HEADS):
                acc = acc + wi[hh:hh + 1, :] * jnp.maximum(rel_scr[rows, hh * tq:(hh + 1) * tq], 0.0)
            acc = acc + 0.0
            valid = k_slab + (s0 + r * KEY_ROWS) <= q_slab + t0
            s_scr[c, rows, :] = jnp.where(valid, _to_key(acc), INT_MIN)
            top = jnp.maximum(top, _fold_rows(jnp.where(valid, jnp.abs(acc), 0.0), sub, jnp.maximum))
        return top

    top = lax.fori_loop(0, nch, score_chunk, jnp.zeros((sub, tq), F32))
    top = jnp.max(top, axis=0, keepdims=True)

    def count(pred):
        def body(c, cnt):
            return cnt + _fold_rows(jnp.where(pred(s_scr[c], key_row + c * tk), 1.0, 0.0), sub)
        cnt = lax.fori_loop(0, nch, body, jnp.zeros((sub, tq), F32))
        return jnp.sum(cnt, axis=0, keepdims=True)

    kf = jnp.float32(top_k)

    def settled(st):
        lo, hi, n_lo, _ = st
        return (n_lo <= kf) | (hi - 1 <= lo)

    def score_mid(st):
        lo_v, hi_v = _from_key(st[0]), _from_key(st[1])
        return _to_key(lo_v + (hi_v - lo_v) * 0.5 + 0.0)

    def probe(st, want):
        lo, hi, n_lo, n_hi = st
        done = settled(st)
        cand = jnp.where((want > lo) & (want < hi), want, (lo >> 1) + (hi >> 1) + (lo & hi & 1))
        cand = jnp.where(done, lo, cand)
        cnt = count(lambda blk, idx: blk >= cand)
        up = (cnt >= kf) & ~done
        down = ~(up | done)
        return (jnp.where(up, cand, lo), jnp.where(down, cand, hi),
                jnp.where(up, cnt, n_lo), jnp.where(down, cnt, n_hi))

    n0 = (t0 + 1 + lax.broadcasted_iota(jnp.int32, (1, tq), 1)).astype(F32)
    st = (_to_key(-top), _to_key(top) + 1, n0, jnp.zeros((1, tq), F32))
    st = lax.fori_loop(0, FIXED_PROBES, lambda i, s: probe(s, score_mid(s)), st)

    lo, hi, n_lo, n_hi = st

    def inside_range(c, carry):
        blk = s_scr[c]
        inside = (blk >= lo) & (blk < hi)
        return (jnp.maximum(carry[0], _fold_rows(jnp.where(inside, blk, INT_MIN), sub, jnp.maximum)),
                jnp.minimum(carry[1], _fold_rows(jnp.where(inside, blk, INT_MAX), sub, jnp.minimum)))

    big, small = lax.fori_loop(0, nch, inside_range, (jnp.full((sub, tq), INT_MIN, jnp.int32),
                                                      jnp.full((sub, tq), INT_MAX, jnp.int32)))
    big = _fold_rows(big, 1, jnp.maximum)
    small = _fold_rows(small, 1, jnp.minimum)
    flat = (big == small) & ~settled(st)
    st = (jnp.where(flat, big, lo), jnp.where(flat, big + 1, hi), n_lo, n_hi)
    want = jnp.where(kf - n_hi == 1.0, big, jnp.where(kf - n_hi == n_lo - n_hi - 1.0, small + 1, score_mid(st)))
    st = probe(st, want)

    def unsettled(st):
        return jnp.sum(jnp.where(settled(st), 0.0, 1.0))

    def more_cond(c):
        return (c[0] < STRAGGLER_PROBES) & (c[1] > 0.0)

    def more_probe(c):
        it, _, st = c
        st = probe(st, jnp.where(it < 4, score_mid(st), st[0]))
        return it + 1, unsettled(st), st

    _, _, st = lax.while_loop(more_cond, more_probe, (jnp.int32(0), unsettled(st), st))
    thr, n_ge = st[0], st[2]

    @pl.when(jnp.max(n_ge) > kf)
    def _():
        need = kf - count(lambda blk, idx: blk > thr)

        def idx_bit(bi, last):
            cand = last | lax.shift_left(jnp.int32(1), idx_bits - 1 - bi)
            cnt = count(lambda blk, idx: (blk == thr) & (idx < cand))
            return jnp.where(cnt < need, cand, last)

        last = lax.fori_loop(0, idx_bits, idx_bit, jnp.zeros((1, tq), jnp.int32))

        def drop(c, carry):
            blk = s_scr[c]
            s_scr[c] = jnp.where((blk == thr) & (key_row + c * tk > last), INT_MIN, blk)
            return carry

        lax.fori_loop(0, nch, drop, 0)

    eye = (lax.broadcasted_iota(jnp.int32, (tq, tq), 0) == lax.broadcasted_iota(jnp.int32, (tq, tq), 1))
    qx_scr[:, 0:KV_LORA] = qa_ref[0].reshape(N_HEADS * tq, KV_LORA)
    for hh in range(N_HEADS):
        qx_scr[hh * tq:(hh + 1) * tq, KV_LORA:KV_LORA + tq] = jnp.where(eye, 1.0, 0.0).astype(BF16)

    def latents(c):
        return ckv_ref[0, pl.ds(pl.multiple_of(c * tk, tk), tk), :]

    def k_aug(c, g):
        mask_t = jnp.where(s_scr[c] >= thr, 0.0, NEG).astype(BF16)
        return jnp.concatenate([latents(c), mask_t], axis=1)

    _flash_groups(nch, qx_scr, k_aug, lambda c, g: _with_ones(latents(c)),
                  lambda c: _tile_ids((t0 - c * tk) // LANES, n_sub, nd),
                  toe_ref, (lg0_scr, lg1_scr), (p0_scr, p1_scr), m_scr, acc_scr, tq)
    for hh in range(N_HEADS):
        rows = slice(hh * tq, (hh + 1) * tq)
        o = acc_scr[rows, 0:KV_LORA] / acc_scr[rows, KV_LORA:2 * KV_LORA]
        o_ref[0, :, hh * KV_LORA:(hh + 1) * KV_LORA] = o.astype(o_ref.dtype)


def _dsa_attn(qi, wi, kidx, qa, ckv, toe, tq=128, tk=512):
    B, _, T, _ = qa.shape
    tk = min(tk, T)
    assert tq == LANES
    nd = toe.shape[0] - 2
    top_k = min(IDX_TOPK_MAX, T // 4)
    kern = functools.partial(_dsa_attn_kernel, tq=tq, tk=tk, top_k=top_k, nd=nd,
                             idx_bits=max(1, (T - 1).bit_length()))
    return pl.pallas_call(
        kern,
        grid=(B, T // tq),
        in_specs=[
            pl.BlockSpec((1, IDX_HEADS, tq, IDX_DIM), lambda b, i: (b, 0, i, 0)),
            pl.BlockSpec((1, IDX_HEADS, tq), lambda b, i: (b, 0, i)),
            pl.BlockSpec((1, T, IDX_DIM), lambda b, i: (b, 0, 0), pipeline_mode=pl.Buffered(1)),
            pl.BlockSpec((1, N_HEADS, tq, KV_LORA), lambda b, i: (b, 0, i, 0)),
            pl.BlockSpec((1, T, KV_LORA), lambda b, i: (b, 0, 0), pipeline_mode=pl.Buffered(1)),
            _const_spec(toe.shape),
        ],
        out_specs=pl.BlockSpec((1, tq, N_HEADS * KV_LORA), lambda b, i: (b, i, 0)),
        out_shape=jax.ShapeDtypeStruct((B, T, N_HEADS * KV_LORA), BF16),
        scratch_shapes=[
            pltpu.VMEM((T // tk, tk, tq), jnp.int32),
            pltpu.VMEM((tk, IDX_HEADS * tq), F32),
            pltpu.VMEM((HEADS_PER_GROUP * tq, tk), F32),
            pltpu.VMEM((HEADS_PER_GROUP * tq, tk), F32),
            pltpu.VMEM((HEADS_PER_GROUP * tq, tk), BF16),
            pltpu.VMEM((HEADS_PER_GROUP * tq, tk), BF16),
            pltpu.VMEM((N_HEADS * tq, KV_LORA + tq), BF16),
            pltpu.VMEM((N_HEADS, tq, LANES), F32),
            pltpu.VMEM((N_HEADS * tq, 2 * KV_LORA), F32),
        ],
        compiler_params=_params(("parallel", "arbitrary")),
        name="dsa_attn",
    )(qi, wi, kidx, qa, ckv, toe)


def _dsa_out_kernel(o_ref, x_ref, w_uv_ref, w_o_ref, y_ref):
    o_lat = o_ref[0]
    parts = [_dot(o_lat[:, hh * KV_LORA:(hh + 1) * KV_LORA], w_uv_ref[hh]).astype(BF16)
             for hh in range(N_HEADS)]
    y_ref[0] = x_ref[0] + _dot(jnp.concatenate(parts, axis=1), w_o_ref[...])


def _dsa_out(o_lat, x, w_uv, w_o, tm=512):
    B, T, D = x.shape
    tile = lambda b, i: (b, i, 0)
    return pl.pallas_call(
        _dsa_out_kernel,
        grid=(B, T // tm),
        in_specs=[pl.BlockSpec((1, tm, N_HEADS * KV_LORA), tile), pl.BlockSpec((1, tm, D), tile),
                  _const_spec((N_HEADS, KV_LORA, HEAD_DIM)), _const_spec((N_HEADS * HEAD_DIM, D))],
        out_specs=pl.BlockSpec((1, tm, D), tile),
        out_shape=jax.ShapeDtypeStruct((B, T, D), F32),
        compiler_params=_params(("parallel", "parallel")),
        name="dsa_out",
    )(o_lat, x, jnp.transpose(w_uv, (1, 0, 2)).astype(BF16), w_o.astype(BF16))


def _out_proj_kernel(o_ref, x_ref, w_o_ref, y_ref):
    y_ref[0] = x_ref[0] + _dot(o_ref[0], w_o_ref[...])


def _out_proj(o, x, w_o, tm=512):
    B, T, D = x.shape
    tile = lambda b, i: (b, i, 0)
    return pl.pallas_call(
        _out_proj_kernel,
        grid=(B, T // tm),
        in_specs=[pl.BlockSpec((1, tm, o.shape[-1]), tile), pl.BlockSpec((1, tm, D), tile),
                  _const_spec(w_o.shape)],
        out_specs=pl.BlockSpec((1, tm, D), tile),
        out_shape=jax.ShapeDtypeStruct((B, T, D), F32),
        compiler_params=_params(("parallel", "parallel")),
        name="out_proj",
    )(o, x, w_o.astype(BF16))


def _mlp_kernel(x_ref, g_ref, w_up_ref, w_down_ref, *rest, tf, final):
    y_ref = rest[-1]
    x = x_ref[0]
    h = _rms(x, g_ref[...]).astype(BF16)
    acc = x
    for f0 in range(0, w_up_ref.shape[1], tf):
        u = jnp.maximum(_dot(h, w_up_ref[:, f0:f0 + tf]), 0.0)
        acc = acc + _dot((u * u).astype(BF16), w_down_ref[f0:f0 + tf, :])
    y_ref[0] = _rms(acc, rest[0][...]) if final else acc


def _mlp(x, g, w_up, w_down, g_final=None, tm=512, tf=512):
    B, T, D = x.shape
    F = w_up.shape[1]
    tile = lambda b, i: (b, i, 0)
    final = g_final is not None
    in_specs = [pl.BlockSpec((1, tm, D), tile), _const_spec((1, D)), _const_spec((D, F)), _const_spec((F, D))]
    args = [x, g.reshape(1, D), w_up.astype(BF16), w_down.astype(BF16)]
    if final:
        in_specs.append(_const_spec((1, D)))
        args.append(g_final.reshape(1, D))
    return pl.pallas_call(
        functools.partial(_mlp_kernel, tf=tf, final=final),
        grid=(B, T // tm),
        in_specs=in_specs,
        out_specs=pl.BlockSpec((1, tm, D), tile),
        out_shape=jax.ShapeDtypeStruct((B, T, D), F32),
        compiler_params=_params(("parallel", "parallel")),
        name="mlp_final" if final else "mlp",
    )(*args)


def _kv_proj_kernel(x_ref, g_ref, w_ref, kc_ref, vc_ref, ks_ref, vs_ref, kw_ref, vw_ref):
    h = _rms(x_ref[0], g_ref[...]).astype(BF16)
    kv = _dot(h, w_ref[...])
    for part, ref in enumerate((kc_ref, vc_ref, ks_ref, vs_ref, kw_ref, vw_ref)):
        for g in range(N_KV_GROUPS):
            o = (part * N_KV_GROUPS + g) * HEAD_DIM
            ref[0, g] = kv[:, o:o + HEAD_DIM].astype(ref.dtype)


def _kv_proj(x, g, w_kv, tm=512):
    B, T, D = x.shape
    gtile = lambda b, i: (b, 0, i, 0)
    spec = pl.BlockSpec((1, N_KV_GROUPS, tm, HEAD_DIM), gtile)
    shp = lambda dt: jax.ShapeDtypeStruct((B, N_KV_GROUPS, T, HEAD_DIM), dt)
    return pl.pallas_call(
        _kv_proj_kernel,
        grid=(B, T // tm),
        in_specs=[pl.BlockSpec((1, tm, D), lambda b, i: (b, i, 0)), _const_spec((1, D)),
                  _const_spec(w_kv.shape)],
        out_specs=[spec] * 6,
        out_shape=[shp(F32), shp(F32), shp(BF16), shp(BF16), shp(BF16), shp(BF16)],
        compiler_params=_params(("parallel", "parallel")),
        name="kv_proj",
    )(x, g.reshape(1, D), w_kv.astype(BF16))


def _compress_kernel(raw_ref, pos_ref, w1_ref, w2_ref, o_ref):
    rows = raw_ref[0, 0]
    half = rows.shape[1]
    a = _dot((rows + pos_ref[:, :half]).astype(BF16), w1_ref[:half, :])
    b = _dot((rows + pos_ref[:, half:]).astype(BF16), w1_ref[half:, :])
    pre = a + pltpu.roll(b, rows.shape[0] - 1, 0)
    act = 0.5 * pre * (1.0 + jnp.tanh(math.sqrt(2.0 / math.pi) * (pre + 0.044715 * (pre * pre * pre))))
    o_ref[0, 0] = _dot(act.astype(BF16), w2_ref[...]).astype(o_ref.dtype)


def _compress(raw, pos, w1, w2):
    B, G, T, Dh = raw.shape
    nr = T // CMP_STRIDE
    rows = raw.reshape(B, G, nr, CMP_STRIDE * Dh)
    return pl.pallas_call(
        _compress_kernel,
        grid=(B, G),
        in_specs=[pl.BlockSpec((1, 1, nr, CMP_STRIDE * Dh), lambda b, g: (b, g, 0, 0)),
                  _const_spec((1, CMP_LEN * Dh)), _const_spec(w1.shape), _const_spec(w2.shape)],
        out_specs=pl.BlockSpec((1, 1, nr, Dh), lambda b, g: (b, g, 0, 0)),
        out_shape=jax.ShapeDtypeStruct((B, G, nr, Dh), BF16),
        compiler_params=_params(("parallel", "parallel")),
        name="compress",
    )(rows, pos.reshape(1, CMP_LEN * Dh), w1.astype(BF16), w2.astype(BF16))


def _nsa_proj_kernel(x_ref, g_ref, wq_ref, wg_ref, q_ref, gate_ref):
    h = _rms(x_ref[0], g_ref[...]).astype(BF16)
    q = _dot(h, wq_ref[...]) * (HEAD_DIM ** -0.5 * LOG2E)
    for hh in range(N_HEADS):
        q_ref[0, hh] = q[:, hh * HEAD_DIM:(hh + 1) * HEAD_DIM].astype(BF16)
    gate_ref[0] = jax.nn.sigmoid(_dot(h, wg_ref[...]))[:, :3 * N_HEADS]


def _nsa_proj(x, g, w_in, tm=512):
    B, T, D = x.shape
    hd = N_HEADS * HEAD_DIM
    wq = w_in[:, :hd].astype(BF16)
    wg = jnp.zeros((D, LANES), F32).at[:, :3 * N_HEADS].set(w_in[:, hd:]).astype(BF16)
    return pl.pallas_call(
        _nsa_proj_kernel,
        grid=(B, T // tm),
        in_specs=[pl.BlockSpec((1, tm, D), lambda b, i: (b, i, 0)), _const_spec((1, D)),
                  _const_spec((D, hd)), _const_spec((D, LANES))],
        out_specs=[pl.BlockSpec((1, N_HEADS, tm, HEAD_DIM), lambda b, i: (b, 0, i, 0)),
                   pl.BlockSpec((1, tm, 3 * N_HEADS), lambda b, i: (b, i, 0))],
        out_shape=[jax.ShapeDtypeStruct((B, N_HEADS, T, HEAD_DIM), BF16),
                   jax.ShapeDtypeStruct((B, T, 3 * N_HEADS), F32)],
        compiler_params=_params(("parallel", "parallel")),
        name="nsa_proj",
    )(x, g.reshape(1, D), wq, wg)


def _top_blocks(imp, blk, n_sel, exact_ties):
    sel = jnp.zeros(imp.shape, F32)
    blk_f = blk.astype(F32)
    for _ in range(n_sel):
        if exact_ties:
            best = jnp.max(imp, axis=1, keepdims=True)
            hit = blk_f == jnp.min(jnp.where(imp == best, blk_f, float(LANES)), axis=1, keepdims=True)
        else:
            hit = blk == jnp.argmax(imp, axis=1, keepdims=True).astype(jnp.int32)
        sel = jnp.where(hit, 1.0, sel)
        imp = jnp.where(hit, -jnp.inf, imp)
    return sel


def _nsa_attn_kernel(q_ref, gate_ref, kc_ref, vc_ref, ks_ref, vs_ref, kw_ref, vw_ref, selmap_ref, toe_ref,
                     o_ref, sel_scr, imp_scr, lg0_scr, lg1_scr, p0_scr, p1_scr, lw_scr, pw_scr, psum_scr, qx_scr, m_scr, acc_scr,
                     ocw_scr, *, tq, tk, nd, n_slc, n_sel):
    G, R = N_KV_GROUPS, HEADS_PER_GROUP
    lg_scrs, p_scrs = (lg0_scr, lg1_scr), (p0_scr, p1_scr)
    t0 = pl.program_id(1) * tq
    n_cmp = kc_ref.shape[2]
    n_sub = tk // LANES
    wlen = WIN + tq
    gates = gate_ref[0]
    t_col = t0 + lax.broadcasted_iota(jnp.int32, (tq, 1), 0)
    t_rows = t0 + lax.broadcasted_iota(jnp.int32, (ROWS, 1), 0)
    slabs = [slice(r * ROWS, (r + 1) * ROWS) for r in range(tq // ROWS)]

    cmp_end = CMP_STRIDE * lax.broadcasted_iota(jnp.int32, (ROWS, n_cmp), 1) + (CMP_LEN - 1)
    blk = lax.broadcasted_iota(jnp.int32, (tq, LANES), 1)
    cur = t_col // SLC_BLK
    forced = (blk == 0) | (blk == cur) | (blk == cur - 1)
    future = blk * SLC_BLK > t_col
    start = pl.multiple_of(jnp.maximum(t0 - WIN, 0), LANES)
    dw = (t0 - start) // LANES
    wtiles = _tile_ids(dw, wlen // LANES, nd)
    wtiles[0] = jnp.where(dw == WIN // LANES, nd + 1, wtiles[0])
    open_ties = []
    for g in range(G):
        lg_scr, p_scr = lg_scrs[g], p_scrs[g]
        qg = q_ref[0, g * R:(g + 1) * R].reshape(R * tq, HEAD_DIM)
        lg_scr[0:R * tq, 0:n_cmp] = _dot_nt(qg, kc_ref[0, g])
        for r, rows in enumerate(slabs):
            cmask = cmp_end <= t_rows + r * ROWS
            pc_sum = jnp.zeros((ROWS, n_cmp), F32)
            for rr in range(R):
                pc = _masked_softmax_rows(lg_scr[rr * tq + r * ROWS:rr * tq + (r + 1) * ROWS, 0:n_cmp], cmask)
                pc_sum = pc_sum + pc
                p_scr[rr * tq + r * ROWS:rr * tq + (r + 1) * ROWS, 0:n_cmp] = pc.astype(BF16)
            psum_scr[rows, :] = pc_sum
        o_cmp = _dot(p_scr[:, 0:n_cmp], vc_ref[0, g])

        lw_scr[0:R * tq, 0:wlen] = _dot_nt(qg, kw_ref[0, g, pl.ds(start, wlen), :])
        for rr in range(R):
            for r, rows in enumerate(slabs):
                lw = (lw_scr[rr * tq + r * ROWS:rr * tq + (r + 1) * ROWS, 0:wlen]
                      + _bias_rows(toe_ref, g * R + rr, wtiles, rows))
                pw = jnp.exp2(lw - jnp.max(lw, axis=1, keepdims=True))
                pw_scr[rr * tq + r * ROWS:rr * tq + (r + 1) * ROWS, 0:wlen] = pw.astype(BF16)
        o_win = _dot(pw_scr[:, 0:wlen], _with_ones(vw_ref[0, g, pl.ds(start, wlen), :]))
        for rr in range(R):
            hh = g * R + rr
            wrows = slice(rr * tq, (rr + 1) * tq)
            ocw_scr[hh * tq:(hh + 1) * tq, :] = (
                gates[:, 3 * hh:3 * hh + 1] * o_cmp[wrows]
                + gates[:, 3 * hh + 2:3 * hh + 3] * (o_win[wrows, 0:HEAD_DIM] / o_win[wrows, HEAD_DIM:2 * HEAD_DIM]))
        pc_sum = psum_scr[...]
        hi = pc_sum.astype(BF16)
        lo = (pc_sum - hi.astype(F32)).astype(BF16)
        imp = _dot(hi, selmap_ref[...]) + _dot(lo, selmap_ref[...])
        imp = jnp.where(forced, 1e9, imp)
        imp = jnp.where(future, NEG, imp)
        imp = jnp.where(blk < n_slc, imp, -jnp.inf)
        sel = _top_blocks(imp, blk, n_sel, exact_ties=False)
        sel_scr[g] = sel
        imp_scr[g] = imp
        cut = jnp.min(jnp.where(sel > 0.0, imp, jnp.inf), axis=1, keepdims=True)
        open_ties.append(jnp.max(jnp.where((imp == cut) & (sel == 0.0) & ~future, 1.0, 0.0)))
        qx_scr[g * R * tq:(g + 1) * R * tq, 0:HEAD_DIM] = qg

    @pl.when(jnp.maximum(*open_ties) > 0.0)
    def _():
        for g in range(G):
            sel_scr[g] = _top_blocks(imp_scr[g], blk, n_sel, exact_ties=True)

    for g in range(G):
        block_bias = jnp.where((sel_scr[g] > 0.0) & ~future, 0.0, NEG).astype(BF16)
        for rr in range(R):
            qx_scr[(g * R + rr) * tq:(g * R + rr + 1) * tq, HEAD_DIM:HEAD_DIM + LANES] = block_bias

    nch = (t0 + tq + tk - 1) // tk
    lane_minus_block = (lax.broadcasted_iota(jnp.int32, (tk, LANES), 1)
                        - lax.broadcasted_iota(jnp.int32, (tk, LANES), 0) // SLC_BLK)

    def k_aug(c, g):
        s0 = pl.multiple_of(c * tk, tk)
        block_of_key = jnp.where(lane_minus_block == s0 // SLC_BLK, 1.0, 0.0).astype(BF16)
        return jnp.concatenate([ks_ref[0, g, pl.ds(s0, tk), :], block_of_key], axis=1)

    _flash_groups(nch, qx_scr, k_aug, lambda c, g: _with_ones(vs_ref[0, g, pl.ds(pl.multiple_of(c * tk, tk), tk), :]),
                  lambda c: _tile_ids((t0 - c * tk) // LANES, n_sub, nd),
                  toe_ref, lg_scrs, p_scrs, m_scr, acc_scr, tq)

    for hh in range(N_HEADS):
        rows = slice(hh * tq, (hh + 1) * tq)
        o_slc = acc_scr[rows, 0:HEAD_DIM] / acc_scr[rows, HEAD_DIM:2 * HEAD_DIM]
        o = ocw_scr[rows, :] + gates[:, 3 * hh + 1:3 * hh + 2] * o_slc
        o_ref[0, :, hh * HEAD_DIM:(hh + 1) * HEAD_DIM] = o.astype(o_ref.dtype)


def _selection_map(n_cmp, n_slc):
    c0 = CMP_STRIDE * np.arange(n_cmp)[:, None]
    s0 = SLC_BLK * np.arange(n_slc)[None, :]
    ov = np.clip(np.minimum(c0 + CMP_LEN, s0 + SLC_BLK) - np.maximum(c0, s0), 0, None)
    return (ov / CMP_LEN).astype(np.float32)


def _nsa_attn(q, gates, kc, vc, ks, vs, kw, vw, toe, tq=128, tk=512):
    B, _, T, _ = q.shape
    tk = min(tk, T)
    assert tq == LANES
    nd = toe.shape[0] - 2
    n_cmp_pad = kc.shape[2]
    n_cmp = (T - CMP_LEN) // CMP_STRIDE + 1
    n_slc = T // SLC_BLK
    assert n_slc <= LANES and T >= WIN + tq
    selmap = np.zeros((n_cmp_pad, LANES), np.float32)
    selmap[:n_cmp, :n_slc] = _selection_map(n_cmp, n_slc)
    kern = functools.partial(_nsa_attn_kernel, tq=tq, tk=tk, nd=nd, n_slc=n_slc, n_sel=min(N_SLC_MAX, n_slc))
    G, R = N_KV_GROUPS, HEADS_PER_GROUP
    width = max(n_cmp_pad, tk, WIN + tq)
    res = lambda n: pl.BlockSpec((1, G, n, HEAD_DIM), lambda b, i: (b, 0, 0, 0), pipeline_mode=pl.Buffered(1))
    return pl.pallas_call(
        kern,
        grid=(B, T // tq),
        in_specs=[
            pl.BlockSpec((1, N_HEADS, tq, HEAD_DIM), lambda b, i: (b, 0, i, 0)),
            pl.BlockSpec((1, tq, 3 * N_HEADS), lambda b, i: (b, i, 0)),
            res(n_cmp_pad), res(n_cmp_pad), res(T), res(T), res(T), res(T),
            _const_spec(selmap.shape),
            _const_spec(toe.shape),
        ],
        out_specs=pl.BlockSpec((1, tq, N_HEADS * HEAD_DIM), lambda b, i: (b, i, 0)),
        out_shape=jax.ShapeDtypeStruct((B, T, N_HEADS * HEAD_DIM), BF16),
        scratch_shapes=[
            pltpu.VMEM((G, tq, LANES), F32),
            pltpu.VMEM((G, tq, LANES), F32),
            pltpu.VMEM((R * tq, width), F32),
            pltpu.VMEM((R * tq, width), F32),
            pltpu.VMEM((R * tq, width), BF16),
            pltpu.VMEM((R * tq, width), BF16),
            pltpu.VMEM((R * tq, WIN + tq), F32),
            pltpu.VMEM((R * tq, WIN + tq), BF16),
            pltpu.VMEM((tq, n_cmp_pad), F32),
            pltpu.VMEM((N_HEADS * tq, HEAD_DIM + LANES), BF16),
            pltpu.VMEM((N_HEADS, tq, LANES), F32),
            pltpu.VMEM((N_HEADS * tq, 2 * HEAD_DIM), F32),
            pltpu.VMEM((N_HEADS * tq, HEAD_DIM), F32),
        ],
        compiler_params=_params(("parallel", "arbitrary")),
        name="nsa_attn",
    )(q, gates, kc, vc, ks, vs, kw, vw, jnp.asarray(selmap, BF16), toe)


def _dsa_layer(x, toe, g_attn, w_in, gq, gkv, gk, w_uq, w_q_idx, w_uk, w_uv, w_o):
    qa, qi, ckv, kidx, wi = _dsa_proj(x, g_attn, w_in, gq, gkv, gk, w_uq, w_uk, w_q_idx)
    o_lat = _dsa_attn(qi, wi, kidx, qa, ckv, toe)
    return _dsa_out(o_lat, x, w_uv, w_o)


def _shared_kv(x, g_kv, w_kv, pos_k, pos_v, w1_k, w2_k, w1_v, w2_v):
    kc_raw, vc_raw, ks, vs, kw, vw = _kv_proj(x, g_kv, w_kv)
    return _compress(kc_raw, pos_k, w1_k, w2_k), _compress(vc_raw, pos_v, w1_v, w2_v), ks, vs, kw, vw


def _nsa_layer(x, toe, kv_shared, g_attn, w_in, w_o):
    q, gates = _nsa_proj(x, g_attn, w_in)
    return _out_proj(_nsa_attn(q, gates, *kv_shared, toe), x, w_o)


def kernel(x, g_attn, g_mlp, w_up, w_down, rel_bias, a_w_in, a_g_q_lat, a_g_kv_lat, a_g_k_idx, a_w_uq, a_w_q_idx, a_w_uk, a_w_uv, a_w_o, g_kv_shared, w_kv_shared, cmp_pos_k, cmp_pos_v, cmp_w1_k, cmp_w2_k, cmp_w1_v, cmp_w2_v, b_w_in, b_w_o, g_final):
    depth = g_attn.shape[0]
    n_a = a_w_in.shape[0]
    toe = _bias_tiles(rel_bias, x.shape[1])
    kv_shared = None
    for l in range(depth):
        if l < n_a:
            x = _dsa_layer(x, toe, g_attn[l], a_w_in[l], a_g_q_lat[l], a_g_kv_lat[l], a_g_k_idx[l],
                           a_w_uq[l], a_w_q_idx[l], a_w_uk[l], a_w_uv[l], a_w_o[l])
        else:
            j = l - n_a
            x = _nsa_layer(x, toe, kv_shared, g_attn[l], b_w_in[j], b_w_o[j])
        x = _mlp(x, g_mlp[l], w_up[l], w_down[l], g_final if l == depth - 1 else None)
        if l == n_a - 1:
            kv_shared = _shared_kv(x, g_kv_shared, w_kv_shared, cmp_pos_k, cmp_pos_v,
                                   cmp_w1_k, cmp_w2_k, cmp_w1_v, cmp_w2_v)
    return x
```

```python
import functools
import math

import numpy as np
import jax
import jax.numpy as jnp
from jax import lax
from jax.experimental import pallas as pl
from jax.experimental.pallas import tpu as pltpu

N_HEADS = 8
HEAD_DIM = 128
Q_LORA = 256
KV_LORA = 128
IDX_HEADS = 8
IDX_DIM = 64
IDX_TOPK_MAX = 256
N_KV_GROUPS = 2
HEADS_PER_GROUP = N_HEADS // N_KV_GROUPS
CMP_LEN = 32
CMP_STRIDE = 16
CMP_HID = 256
SLC_BLK = 64
N_SLC_MAX = 16
WIN = 512
REL_BUCKETS = 32
REL_MAX_DIST = 4096
EPS = 1e-6
NEG = -1e30
LOG2E = math.log2(math.e)

LANES = 128
ROWS = 16
KEY_ROWS = 64
INT_MIN = np.int32(-2 ** 31)
INT_MAX = np.int32(2 ** 31 - 1)
FIXED_PROBES = 15
STRAGGLER_PROBES = 40
VMEM_LIMIT = 56 * 1024 * 1024

F32 = jnp.float32
BF16 = jnp.bfloat16
NT_DIMS = (((1,), (1,)), ((), ()))


def _dot(a, b):
    return jnp.dot(a, b, preferred_element_type=F32)


def _dot_nt(a, b):
    return lax.dot_general(a, b, NT_DIMS, preferred_element_type=F32)


def _rms(x, g):
    return x * lax.rsqrt(jnp.mean(x * x, axis=-1, keepdims=True) + EPS) * g


def _to_key(v):
    bits = lax.bitcast_convert_type(v, jnp.int32)
    return bits ^ ((bits >> 31) & jnp.int32(0x7FFFFFFF))


def _const_spec(shape):
    nd = len(shape)
    return pl.BlockSpec(shape, lambda *_: (0,) * nd, pipeline_mode=pl.Buffered(1))


def _params(sem):
    return pltpu.CompilerParams(dimension_semantics=sem, vmem_limit_bytes=VMEM_LIMIT)


def _rel_bucket(dist):
    dist = jnp.maximum(dist, 0)
    exact = REL_BUCKETS // 2
    log_ratio = jnp.log(jnp.maximum(dist, 1).astype(F32) / exact) / math.log(REL_MAX_DIST / exact)
    large = exact + (log_ratio * (REL_BUCKETS - exact)).astype(jnp.int32)
    return jnp.where(dist < exact, dist, jnp.minimum(large, REL_BUCKETS - 1))


def _num_bias_tiles(T):
    exact = REL_BUCKETS // 2
    switch = exact * (REL_MAX_DIST / exact) ** ((REL_BUCKETS - exact - 1) / (REL_BUCKETS - exact))
    far = int(math.ceil(switch)) + 32
    return min(T // LANES, -(-(far + LANES - 1) // LANES) + 1)


def _bias_tiles(rel_bias, T):
    nd = _num_bias_tiles(T)
    assert nd > WIN // LANES
    c = LANES * (nd - 1)
    bd = rel_bias[_rel_bucket(jnp.arange(c + LANES, dtype=jnp.int32))].T.astype(F32) * LOG2E
    p = c + 2 * LANES
    w = jnp.concatenate([bd[:, c::-1], jnp.broadcast_to(bd[:, :1], (N_HEADS, LANES)), bd[:, :c:-1]], axis=1)
    band = jnp.tile(w, (1, LANES))[:, :LANES * (p - 1)].reshape(N_HEADS, LANES, p - 1)[:, :, :c + LANES]
    tiles = band.reshape(N_HEADS, LANES, nd, LANES)[:, :, ::-1]
    tiles = jnp.transpose(tiles, (2, 0, 1, 3))
    i = jnp.arange(LANES)[:, None]
    j = jnp.arange(LANES)[None, :]
    ahead = jnp.full((1,) + tiles.shape[1:], NEG, F32)
    diag = jnp.where(i >= j, tiles[:1], NEG)
    win_edge = jnp.where(i < j, tiles[WIN // LANES:WIN // LANES + 1], NEG)
    return jnp.concatenate([ahead, diag, tiles[1:], win_edge], axis=0)


def _tile_ids(d0, n_sub, nd):
    return [jnp.clip(d0 - k, -1, nd - 1) + 1 for k in range(n_sub)]


def _dsa_proj_kernel(x_ref, g_ref, w_in_ref, w_wt_ref, gq_ref, gkv_ref, gk_ref, w_uq_ref, w_ukt_ref, w_qi_ref,
                     qa_ref, qi_ref, ckv_ref, kidx_ref, wi_ref):
    h = _rms(x_ref[0], g_ref[...]).astype(BF16)
    proj = _dot(h, w_in_ref[...])
    c_q = _rms(proj[:, :Q_LORA], gq_ref[...]).astype(BF16)
    ckv_ref[0] = _rms(proj[:, Q_LORA:Q_LORA + KV_LORA], gkv_ref[...]).astype(BF16)
    o_k = Q_LORA + KV_LORA
    kidx_ref[0] = _rms(proj[:, o_k:o_k + IDX_DIM], gk_ref[...]).astype(BF16)
    wi_ref[0] = _dot_nt(w_wt_ref[...], h) * (IDX_HEADS ** -0.5 * IDX_DIM ** -0.5)
    q = _dot(c_q, w_uq_ref[...])
    for hh in range(N_HEADS):
        qh = q[:, hh * HEAD_DIM:(hh + 1) * HEAD_DIM].astype(BF16)
        qa_ref[0, hh] = (_dot(qh, w_ukt_ref[hh]) * (HEAD_DIM ** -0.5 * LOG2E)).astype(BF16)
        qi_ref[0, hh] = _dot(c_q, w_qi_ref[hh]).astype(BF16)


def _dsa_proj(x, g, w_in, gq, gkv, gk, w_uq, w_uk, w_q_idx, tm=256):
    B, T, D = x.shape
    n_in = Q_LORA + KV_LORA + LANES
    o_k = Q_LORA + KV_LORA
    w_in_p = jnp.zeros((D, n_in), F32).at[:, :o_k + IDX_DIM].set(w_in[:, :o_k + IDX_DIM])
    w_wt = w_in[:, o_k + IDX_DIM:].T.astype(BF16)
    w_ukt = jnp.transpose(w_uk, (1, 2, 0)).astype(BF16)
    w_qi = jnp.transpose(w_q_idx, (1, 0, 2)).astype(BF16)
    tile = lambda b, i: (b, i, 0)
    htile = lambda b, i: (b, 0, i, 0)
    return pl.pallas_call(
        _dsa_proj_kernel,
        grid=(B, T // tm),
        in_specs=[
            pl.BlockSpec((1, tm, D), tile),
            _const_spec((1, D)),
            _const_spec((D, n_in)),
            _const_spec((IDX_HEADS, D)),
            _const_spec((1, Q_LORA)), _const_spec((1, KV_LORA)), _const_spec((1, IDX_DIM)),
            _const_spec((Q_LORA, N_HEADS * HEAD_DIM)),
            _const_spec((N_HEADS, HEAD_DIM, KV_LORA)),
            _const_spec((IDX_HEADS, Q_LORA, IDX_DIM)),
        ],
        out_specs=[
            pl.BlockSpec((1, N_HEADS, tm, KV_LORA), htile),
            pl.BlockSpec((1, IDX_HEADS, tm, IDX_DIM), htile),
            pl.BlockSpec((1, tm, KV_LORA), tile),
            pl.BlockSpec((1, tm, IDX_DIM), tile),
            pl.BlockSpec((1, IDX_HEADS, tm), lambda b, i: (b, 0, i)),
        ],
        out_shape=[
            jax.ShapeDtypeStruct((B, N_HEADS, T, KV_LORA), BF16),
            jax.ShapeDtypeStruct((B, IDX_HEADS, T, IDX_DIM), BF16),
            jax.ShapeDtypeStruct((B, T, KV_LORA), BF16),
            jax.ShapeDtypeStruct((B, T, IDX_DIM), BF16),
            jax.ShapeDtypeStruct((B, IDX_HEADS, T), F32),
        ],
        compiler_params=_params(("parallel", "parallel")),
        name="dsa_proj",
    )(x, g.reshape(1, D), w_in_p.astype(BF16), w_wt, gq.reshape(1, -1), gkv.reshape(1, -1), gk.reshape(1, -1),
      w_uq.reshape(Q_LORA, N_HEADS * HEAD_DIM).astype(BF16), w_ukt, w_qi)


def _tile_lanes(a, n):
    return a if n == 1 else jnp.concatenate([a] * n, axis=1)


def _bias_rows(toe_ref, head, tiles, rows):
    parts = [toe_ref[d, head, rows, :] for d in tiles]
    return parts[0] if len(parts) == 1 else jnp.concatenate(parts, axis=1)


def _flash_head(lg_scr, lrow0, toe_ref, tiles, head, m_scr, acc_scr, p_scr, prow0, tq):
    n_rep = len(tiles)
    tk = n_rep * LANES
    for r in range(tq // ROWS):
        rows = slice(r * ROWS, (r + 1) * ROWS)
        lg = lg_scr[lrow0 + r * ROWS:lrow0 + (r + 1) * ROWS, 0:tk] + _bias_rows(toe_ref, head, tiles, rows)
        m_old = m_scr[head, rows]
        m_new = jnp.maximum(m_old, jnp.max(lg, axis=1, keepdims=True))
        m_scr[head, rows] = m_new
        p_scr[prow0 + r * ROWS:prow0 + (r + 1) * ROWS, 0:tk] = jnp.exp2(lg - _tile_lanes(m_new, n_rep)).astype(BF16)
        arows = slice(head * tq + r * ROWS, head * tq + (r + 1) * ROWS)
        acc_scr[arows, :] = acc_scr[arows, :] * _tile_lanes(jnp.exp2(m_old - m_new), 2)


def _with_ones(v):
    return jnp.concatenate([v, jnp.ones(v.shape, v.dtype)], axis=1)


def _flash_groups(nch, qx_scr, k_aug, v_aug, tile_ids, toe_ref, lg_scrs, p_scrs, m_scr, acc_scr, tq):
    R = HEADS_PER_GROUP
    grows = [slice(g * R * tq, (g + 1) * R * tq) for g in range(N_KV_GROUPS)]

    def logits(c, g):
        k = k_aug(c, g)
        lg_scrs[g][:, 0:k.shape[0]] = _dot_nt(qx_scr[grows[g], :], k)

    def softmax(c, g):
        tiles = tile_ids(c)
        for rr in range(R):
            _flash_head(lg_scrs[g], rr * tq, toe_ref, tiles, g * R + rr, m_scr, acc_scr, p_scrs[g], rr * tq, tq)

    def values(c, g):
        v = v_aug(c, g)
        acc_scr[grows[g], :] = acc_scr[grows[g], :] + _dot(p_scrs[g][:, 0:v.shape[0]], v)

    m_scr[...] = jnp.full(m_scr.shape, NEG, F32)
    acc_scr[...] = jnp.zeros(acc_scr.shape, F32)
    p_scrs[1][...] = jnp.zeros(p_scrs[1].shape, BF16)
    logits(0, 0)

    def chunk(c, carry):
        logits(c, 1)
        softmax(c, 0)
        values(jnp.maximum(c - 1, 0), 1)
        logits(jnp.minimum(c + 1, nch - 1), 0)
        softmax(c, 1)
        values(c, 0)
        return carry

    lax.fori_loop(0, nch, chunk, 0)
    values(nch - 1, 1)


def _masked_softmax_rows(lg, mask):
    lg = jnp.where(mask, lg, NEG)
    e = jnp.where(mask, jnp.exp2(lg - jnp.max(lg, axis=1, keepdims=True)), 0.0)
    return e * (1.0 / jnp.maximum(jnp.sum(e, axis=1, keepdims=True), 1e-30))


def _fold_rows(a, n, op=jnp.add):
    parts = [a[i:i + n] for i in range(0, a.shape[0], n)]
    while len(parts) > 1:
        parts = [op(parts[i], parts[i + 1]) for i in range(0, len(parts), 2)]
    return parts[0]


def _from_key(k):
    return lax.bitcast_convert_type(k ^ ((k >> 31) & jnp.int32(0x7FFFFFFF)), F32)


def _dsa_attn_kernel(qi_ref, wi_ref, kidx_ref, qa_ref, ckv_ref, toe_ref, o_ref,
                     s_scr, rel0_scr, rel1_scr, lg0_scr, lg1_scr, p0_scr, p1_scr, qx_scr, m_scr, acc_scr,
                     *, tq, tk, top_k, nd, idx_bits):
    t0 = pl.program_id(1) * tq
    nch = (t0 + tq + tk - 1) // tk
    n_sub = tk // LANES
    sub = 8
    key_row = lax.broadcasted_iota(jnp.int32, (tk, tq), 0)

    wi = wi_ref[0]
    qi = qi_ref[0].reshape(IDX_HEADS * tq, IDX_DIM)
    k_slab = lax.broadcasted_iota(jnp.int32, (KEY_ROWS, tq), 0)
    q_slab = lax.broadcasted_iota(jnp.int32, (KEY_ROWS, tq), 1)

    def head_dots(c, rel_scr):
        s0 = pl.multiple_of(jnp.minimum(c, nch - 1) * tk, tk)
        rel_scr[...] = _dot_nt(kidx_ref[0, pl.ds(s0, tk), :], qi)

    def scores(c, rel_scr, top):
        s0 = c * tk
        for r in range(tk // KEY_ROWS):
            rows = slice(r * KEY_ROWS, (r + 1) * KEY_ROWS)
            acc = jnp.zeros((KEY_ROWS, tq), F32)
            for hh in range(IDX_HEADS):
                acc = acc + wi[hh:hh + 1, :] * jnp.maximum(rel_scr[rows, hh * tq:(hh + 1) * tq], 0.0)
            acc = acc + 0.0
            valid = k_slab + (s0 + r * KEY_ROWS) <= q_slab + t0
            s_scr[c, rows, :] = jnp.where(valid, _to_key(acc), INT_MIN)
            top = jnp.maximum(top, _fold_rows(jnp.where(valid, jnp.abs(acc), 0.0), sub, jnp.maximum))
        return top

    def score_pair(j, top):
        head_dots(2 * j + 1, rel1_scr)
        top = scores(2 * j, rel0_scr, top)
        head_dots(2 * j + 2, rel0_scr)
        return scores(2 * j + 1, rel1_scr, top)

    head_dots(0, rel0_scr)
    top = lax.fori_loop(0, (nch + 1) // 2, score_pair, jnp.zeros((sub, tq), F32))
    top = jnp.max(top, axis=0, keepdims=True)

    def count(pred):
        def body(c, cnt):
            return cnt + _fold_rows(jnp.where(pred(s_scr[c], key_row + c * tk), 1.0, 0.0), sub)
        cnt = lax.fori_loop(0, nch, body, jnp.zeros((sub, tq), F32))
        return jnp.sum(cnt, axis=0, keepdims=True)

    kf = jnp.float32(top_k)

    def settled(st):
        lo, hi, n_lo, _ = st
        return (n_lo <= kf) | (hi - 1 <= lo)

    def score_mid(st):
        lo_v, hi_v = _from_key(st[0]), _from_key(st[1])
        return _to_key(lo_v + (hi_v - lo_v) * 0.5 + 0.0)

    def probe(st, want):
        lo, hi, n_lo, n_hi = st
        done = settled(st)
        cand = jnp.where((want > lo) & (want < hi), want, (lo >> 1) + (hi >> 1) + (lo & hi & 1))
        cand = jnp.where(done, lo, cand)
        cnt = count(lambda blk, idx: blk >= cand)
        up = (cnt >= kf) & ~done
        down = ~(up | done)
        return (jnp.where(up, cand, lo), jnp.where(down, cand, hi),
                jnp.where(up, cnt, n_lo), jnp.where(down, cnt, n_hi))

    n0 = (t0 + 1 + lax.broadcasted_iota(jnp.int32, (1, tq), 1)).astype(F32)
    st = (_to_key(-top), _to_key(top) + 1, n0, jnp.zeros((1, tq), F32))
    st = lax.fori_loop(0, FIXED_PROBES, lambda i, s: probe(s, score_mid(s)), st)

    lo, hi, n_lo, n_hi = st

    def inside_range(c, carry):
        blk = s_scr[c]
        inside = (blk >= lo) & (blk < hi)
        return (jnp.maximum(carry[0], _fold_rows(jnp.where(inside, blk, INT_MIN), sub, jnp.maximum)),
                jnp.minimum(carry[1], _fold_rows(jnp.where(inside, blk, INT_MAX), sub, jnp.minimum)))

    big, small = lax.fori_loop(0, nch, inside_range, (jnp.full((sub, tq), INT_MIN, jnp.int32),
                                                      jnp.full((sub, tq), INT_MAX, jnp.int32)))
    big = _fold_rows(big, 1, jnp.maximum)
    small = _fold_rows(small, 1, jnp.minimum)
    flat = (big == small) & ~settled(st)
    st = (jnp.where(flat, big, lo), jnp.where(flat, big + 1, hi), n_lo, n_hi)
    want = jnp.where(kf - n_hi == 1.0, big, jnp.where(kf - n_hi == n_lo - n_hi - 1.0, small + 1, score_mid(st)))
    st = probe(st, want)

    def unsettled(st):
        return jnp.sum(jnp.where(settled(st), 0.0, 1.0))

    def more_cond(c):
        return (c[0] < STRAGGLER_PROBES) & (c[1] > 0.0)

    def more_probe(c):
        it, _, st = c
        st = probe(st, jnp.where(it < 4, score_mid(st), st[0]))
        return it + 1, unsettled(st), st

    _, _, st = lax.while_loop(more_cond, more_probe, (jnp.int32(0), unsettled(st), st))
    thr, n_ge = st[0], st[2]

    @pl.when(jnp.max(n_ge) > kf)
    def _():
        need = kf - count(lambda blk, idx: blk > thr)

        def idx_bit(bi, last):
            cand = last | lax.shift_left(jnp.int32(1), idx_bits - 1 - bi)
            cnt = count(lambda blk, idx: (blk == thr) & (idx < cand))
            return jnp.where(cnt < need, cand, last)

        last = lax.fori_loop(0, idx_bits, idx_bit, jnp.zeros((1, tq), jnp.int32))

        def drop(c, carry):
            blk = s_scr[c]
            s_scr[c] = jnp.where((blk == thr) & (key_row + c * tk > last), INT_MIN, blk)
            return carry

        lax.fori_loop(0, nch, drop, 0)

    eye = (lax.broadcasted_iota(jnp.int32, (tq, tq), 0) == lax.broadcasted_iota(jnp.int32, (tq, tq), 1))
    qx_scr[:, 0:KV_LORA] = qa_ref[0].reshape(N_HEADS * tq, KV_LORA)
    for hh in range(N_HEADS):
        qx_scr[hh * tq:(hh + 1) * tq, KV_LORA:KV_LORA + tq] = jnp.where(eye, 1.0, 0.0).astype(BF16)

    def latents(c):
        return ckv_ref[0, pl.ds(pl.multiple_of(c * tk, tk), tk), :]

    def k_aug(c, g):
        mask_t = jnp.where(s_scr[c] >= thr, 0.0, NEG).astype(BF16)
        return jnp.concatenate([latents(c), mask_t], axis=1)

    _flash_groups(nch, qx_scr, k_aug, lambda c, g: _with_ones(latents(c)),
                  lambda c: _tile_ids((t0 - c * tk) // LANES, n_sub, nd),
                  toe_ref, (lg0_scr, lg1_scr), (p0_scr, p1_scr), m_scr, acc_scr, tq)
    for hh in range(N_HEADS):
        rows = slice(hh * tq, (hh + 1) * tq)
        o = acc_scr[rows, 0:KV_LORA] / acc_scr[rows, KV_LORA:2 * KV_LORA]
        o_ref[0, :, hh * KV_LORA:(hh + 1) * KV_LORA] = o.astype(o_ref.dtype)


def _dsa_attn(qi, wi, kidx, qa, ckv, toe, tq=128, tk=512):
    B, _, T, _ = qa.shape
    tk = min(tk, T)
    assert tq == LANES
    nd = toe.shape[0] - 2
    top_k = min(IDX_TOPK_MAX, T // 4)
    kern = functools.partial(_dsa_attn_kernel, tq=tq, tk=tk, top_k=top_k, nd=nd,
                             idx_bits=max(1, (T - 1).bit_length()))
    return pl.pallas_call(
        kern,
        grid=(B, T // tq),
        in_specs=[
            pl.BlockSpec((1, IDX_HEADS, tq, IDX_DIM), lambda b, i: (b, 0, i, 0)),
            pl.BlockSpec((1, IDX_HEADS, tq), lambda b, i: (b, 0, i)),
            pl.BlockSpec((1, T, IDX_DIM), lambda b, i: (b, 0, 0), pipeline_mode=pl.Buffered(1)),
            pl.BlockSpec((1, N_HEADS, tq, KV_LORA), lambda b, i: (b, 0, i, 0)),
            pl.BlockSpec((1, T, KV_LORA), lambda b, i: (b, 0, 0), pipeline_mode=pl.Buffered(1)),
            _const_spec(toe.shape),
        ],
        out_specs=pl.BlockSpec((1, tq, N_HEADS * KV_LORA), lambda b, i: (b, i, 0)),
        out_shape=jax.ShapeDtypeStruct((B, T, N_HEADS * KV_LORA), BF16),
        scratch_shapes=[
            pltpu.VMEM((T // tk + 1, tk, tq), jnp.int32),
            pltpu.VMEM((tk, IDX_HEADS * tq), F32),
            pltpu.VMEM((tk, IDX_HEADS * tq), F32),
            pltpu.VMEM((HEADS_PER_GROUP * tq, tk), F32),
            pltpu.VMEM((HEADS_PER_GROUP * tq, tk), F32),
            pltpu.VMEM((HEADS_PER_GROUP * tq, tk), BF16),
            pltpu.VMEM((HEADS_PER_GROUP * tq, tk), BF16),
            pltpu.VMEM((N_HEADS * tq, KV_LORA + tq), BF16),
            pltpu.VMEM((N_HEADS, tq, LANES), F32),
            pltpu.VMEM((N_HEADS * tq, 2 * KV_LORA), F32),
        ],
        compiler_params=_params(("parallel", "arbitrary")),
        name="dsa_attn",
    )(qi, wi, kidx, qa, ckv, toe)


def _out_mlp_kernel(o_ref, x_ref, w_o_ref, g_ref, w_up_ref, w_down_ref, *rest, tf, latent, final):
    y_ref = rest[-1]
    o = o_ref[0]
    if latent:
        o = jnp.concatenate([_dot(o[:, hh * KV_LORA:(hh + 1) * KV_LORA], rest[0][hh]).astype(BF16)
                             for hh in range(N_HEADS)], axis=1)
    x = x_ref[0] + _dot(o, w_o_ref[...])
    h = _rms(x, g_ref[...]).astype(BF16)
    acc = x
    for f0 in range(0, w_up_ref.shape[1], tf):
        u = jnp.maximum(_dot(h, w_up_ref[:, f0:f0 + tf]), 0.0)
        acc = acc + _dot((u * u).astype(BF16), w_down_ref[f0:f0 + tf, :])
    y_ref[0] = _rms(acc, rest[-2][...]) if final else acc


def _out_mlp(o, x, w_o, g, w_up, w_down, w_uv=None, g_final=None, tm=512, tf=512):
    B, T, D = x.shape
    F = w_up.shape[1]
    tile = lambda b, i: (b, i, 0)
    latent, final = w_uv is not None, g_final is not None
    in_specs = [pl.BlockSpec((1, tm, o.shape[-1]), tile), pl.BlockSpec((1, tm, D), tile), _const_spec(w_o.shape),
                _const_spec((1, D)), _const_spec((D, F)), _const_spec((F, D))]
    args = [o, x, w_o.astype(BF16), g.reshape(1, D), w_up.astype(BF16), w_down.astype(BF16)]
    if latent:
        in_specs.append(_const_spec((N_HEADS, KV_LORA, HEAD_DIM)))
        args.append(jnp.transpose(w_uv, (1, 0, 2)).astype(BF16))
    if final:
        in_specs.append(_const_spec((1, D)))
        args.append(g_final.reshape(1, D))
    return pl.pallas_call(
        functools.partial(_out_mlp_kernel, tf=tf, latent=latent, final=final),
        grid=(B, T // tm),
        in_specs=in_specs,
        out_specs=pl.BlockSpec((1, tm, D), tile),
        out_shape=jax.ShapeDtypeStruct((B, T, D), F32),
        compiler_params=_params(("parallel", "parallel")),
        name="out_mlp_final" if final else "out_mlp",
    )(*args)


def _kv_proj_kernel(x_ref, g_ref, w_ref, kc_ref, vc_ref, ks_ref, vs_ref, kw_ref, vw_ref):
    h = _rms(x_ref[0], g_ref[...]).astype(BF16)
    kv = _dot(h, w_ref[...])
    for part, ref in enumerate((kc_ref, vc_ref, ks_ref, vs_ref, kw_ref, vw_ref)):
        for g in range(N_KV_GROUPS):
            o = (part * N_KV_GROUPS + g) * HEAD_DIM
            ref[0, g] = kv[:, o:o + HEAD_DIM].astype(ref.dtype)


def _kv_proj(x, g, w_kv, tm=512):
    B, T, D = x.shape
    gtile = lambda b, i: (b, 0, i, 0)
    spec = pl.BlockSpec((1, N_KV_GROUPS, tm, HEAD_DIM), gtile)
    shp = lambda dt: jax.ShapeDtypeStruct((B, N_KV_GROUPS, T, HEAD_DIM), dt)
    return pl.pallas_call(
        _kv_proj_kernel,
        grid=(B, T // tm),
        in_specs=[pl.BlockSpec((1, tm, D), lambda b, i: (b, i, 0)), _const_spec((1, D)),
                  _const_spec(w_kv.shape)],
        out_specs=[spec] * 6,
        out_shape=[shp(F32), shp(F32), shp(BF16), shp(BF16), shp(BF16), shp(BF16)],
        compiler_params=_params(("parallel", "parallel")),
        name="kv_proj",
    )(x, g.reshape(1, D), w_kv.astype(BF16))


def _compress_kernel(raw_ref, pos_ref, w1_ref, w2_ref, o_ref):
    rows = raw_ref[0, 0]
    half = rows.shape[1]
    a = _dot((rows + pos_ref[:, :half]).astype(BF16), w1_ref[:half, :])
    b = _dot((rows + pos_ref[:, half:]).astype(BF16), w1_ref[half:, :])
    pre = a + pltpu.roll(b, rows.shape[0] - 1, 0)
    act = 0.5 * pre * (1.0 + jnp.tanh(math.sqrt(2.0 / math.pi) * (pre + 0.044715 * (pre * pre * pre))))
    o_ref[0, 0] = _dot(act.astype(BF16), w2_ref[...]).astype(o_ref.dtype)


def _compress(raw, pos, w1, w2):
    B, G, T, Dh = raw.shape
    nr = T // CMP_STRIDE
    rows = raw.reshape(B, G, nr, CMP_STRIDE * Dh)
    return pl.pallas_call(
        _compress_kernel,
        grid=(B, G),
        in_specs=[pl.BlockSpec((1, 1, nr, CMP_STRIDE * Dh), lambda b, g: (b, g, 0, 0)),
                  _const_spec((1, CMP_LEN * Dh)), _const_spec(w1.shape), _const_spec(w2.shape)],
        out_specs=pl.BlockSpec((1, 1, nr, Dh), lambda b, g: (b, g, 0, 0)),
        out_shape=jax.ShapeDtypeStruct((B, G, nr, Dh), BF16),
        compiler_params=_params(("parallel", "parallel")),
        name="compress",
    )(rows, pos.reshape(1, CMP_LEN * Dh), w1.astype(BF16), w2.astype(BF16))


def _nsa_proj_kernel(x_ref, g_ref, wq_ref, wg_ref, q_ref, gate_ref):
    h = _rms(x_ref[0], g_ref[...]).astype(BF16)
    q = _dot(h, wq_ref[...]) * (HEAD_DIM ** -0.5 * LOG2E)
    for hh in range(N_HEADS):
        q_ref[0, hh] = q[:, hh * HEAD_DIM:(hh + 1) * HEAD_DIM].astype(BF16)
    gate_ref[0] = jax.nn.sigmoid(_dot(h, wg_ref[...]))[:, :3 * N_HEADS]


def _nsa_proj(x, g, w_in, tm=512):
    B, T, D = x.shape
    hd = N_HEADS * HEAD_DIM
    wq = w_in[:, :hd].astype(BF16)
    wg = jnp.zeros((D, LANES), F32).at[:, :3 * N_HEADS].set(w_in[:, hd:]).astype(BF16)
    return pl.pallas_call(
        _nsa_proj_kernel,
        grid=(B, T // tm),
        in_specs=[pl.BlockSpec((1, tm, D), lambda b, i: (b, i, 0)), _const_spec((1, D)),
                  _const_spec((D, hd)), _const_spec((D, LANES))],
        out_specs=[pl.BlockSpec((1, N_HEADS, tm, HEAD_DIM), lambda b, i: (b, 0, i, 0)),
                   pl.BlockSpec((1, tm, 3 * N_HEADS), lambda b, i: (b, i, 0))],
        out_shape=[jax.ShapeDtypeStruct((B, N_HEADS, T, HEAD_DIM), BF16),
                   jax.ShapeDtypeStruct((B, T, 3 * N_HEADS), F32)],
        compiler_params=_params(("parallel", "parallel")),
        name="nsa_proj",
    )(x, g.reshape(1, D), wq, wg)


def _top_blocks(imp, blk, n_sel, exact_ties):
    sel = jnp.zeros(imp.shape, F32)
    blk_f = blk.astype(F32)
    for _ in range(n_sel):
        if exact_ties:
            best = jnp.max(imp, axis=1, keepdims=True)
            hit = blk_f == jnp.min(jnp.where(imp == best, blk_f, float(LANES)), axis=1, keepdims=True)
        else:
            hit = blk == jnp.argmax(imp, axis=1, keepdims=True).astype(jnp.int32)
        sel = jnp.where(hit, 1.0, sel)
        imp = jnp.where(hit, -jnp.inf, imp)
    return sel


def _nsa_attn_kernel(q_ref, gate_ref, kc_ref, vc_ref, ks_ref, vs_ref, kw_ref, vw_ref, selmap_ref, toe_ref,
                     o_ref, sel_scr, imp_scr, lg0_scr, lg1_scr, p0_scr, p1_scr, lw_scr, pw_scr, psum_scr, qx_scr, m_scr, acc_scr,
                     ocw_scr, *, tq, tk, nd, n_slc, n_sel):
    G, R = N_KV_GROUPS, HEADS_PER_GROUP
    lg_scrs, p_scrs = (lg0_scr, lg1_scr), (p0_scr, p1_scr)
    t0 = pl.program_id(1) * tq
    n_cmp = kc_ref.shape[2]
    n_sub = tk // LANES
    wlen = WIN + tq
    gates = gate_ref[0]
    t_col = t0 + lax.broadcasted_iota(jnp.int32, (tq, 1), 0)
    t_rows = t0 + lax.broadcasted_iota(jnp.int32, (ROWS, 1), 0)
    slabs = [slice(r * ROWS, (r + 1) * ROWS) for r in range(tq // ROWS)]

    cmp_end = CMP_STRIDE * lax.broadcasted_iota(jnp.int32, (ROWS, n_cmp), 1) + (CMP_LEN - 1)
    blk = lax.broadcasted_iota(jnp.int32, (tq, LANES), 1)
    cur = t_col // SLC_BLK
    forced = (blk == 0) | (blk == cur) | (blk == cur - 1)
    future = blk * SLC_BLK > t_col
    start = pl.multiple_of(jnp.maximum(t0 - WIN, 0), LANES)
    dw = (t0 - start) // LANES
    wtiles = _tile_ids(dw, wlen // LANES, nd)
    wtiles[0] = jnp.where(dw == WIN // LANES, nd + 1, wtiles[0])
    open_ties = []
    for g in range(G):
        lg_scr, p_scr = lg_scrs[g], p_scrs[g]
        qg = q_ref[0, g * R:(g + 1) * R].reshape(R * tq, HEAD_DIM)
        lg_scr[0:R * tq, 0:n_cmp] = _dot_nt(qg, kc_ref[0, g])
        for r, rows in enumerate(slabs):
            cmask = cmp_end <= t_rows + r * ROWS
            pc_sum = jnp.zeros((ROWS, n_cmp), F32)
            for rr in range(R):
                pc = _masked_softmax_rows(lg_scr[rr * tq + r * ROWS:rr * tq + (r + 1) * ROWS, 0:n_cmp], cmask)
                pc_sum = pc_sum + pc
                p_scr[rr * tq + r * ROWS:rr * tq + (r + 1) * ROWS, 0:n_cmp] = pc.astype(BF16)
            psum_scr[rows, :] = pc_sum
        o_cmp = _dot(p_scr[:, 0:n_cmp], vc_ref[0, g])

        lw_scr[0:R * tq, 0:wlen] = _dot_nt(qg, kw_ref[0, g, pl.ds(start, wlen), :])
        for rr in range(R):
            for r, rows in enumerate(slabs):
                lw = (lw_scr[rr * tq + r * ROWS:rr * tq + (r + 1) * ROWS, 0:wlen]
                      + _bias_rows(toe_ref, g * R + rr, wtiles, rows))
                pw = jnp.exp2(lw - jnp.max(lw, axis=1, keepdims=True))
                pw_scr[rr * tq + r * ROWS:rr * tq + (r + 1) * ROWS, 0:wlen] = pw.astype(BF16)
        o_win = _dot(pw_scr[:, 0:wlen], _with_ones(vw_ref[0, g, pl.ds(start, wlen), :]))
        for rr in range(R):
            hh = g * R + rr
            wrows = slice(rr * tq, (rr + 1) * tq)
            ocw_scr[hh * tq:(hh + 1) * tq, :] = (
                gates[:, 3 * hh:3 * hh + 1] * o_cmp[wrows]
                + gates[:, 3 * hh + 2:3 * hh + 3] * (o_win[wrows, 0:HEAD_DIM] / o_win[wrows, HEAD_DIM:2 * HEAD_DIM]))
        pc_sum = psum_scr[...]
        hi = pc_sum.astype(BF16)
        lo = (pc_sum - hi.astype(F32)).astype(BF16)
        imp = _dot(hi, selmap_ref[...]) + _dot(lo, selmap_ref[...])
        imp = jnp.where(forced, 1e9, imp)
        imp = jnp.where(future, NEG, imp)
        imp = jnp.where(blk < n_slc, imp, -jnp.inf)
        sel = _top_blocks(imp, blk, n_sel, exact_ties=False)
        sel_scr[g] = sel
        imp_scr[g] = imp
        cut = jnp.min(jnp.where(sel > 0.0, imp, jnp.inf), axis=1, keepdims=True)
        open_ties.append(jnp.max(jnp.where((imp == cut) & (sel == 0.0) & ~future, 1.0, 0.0)))
        qx_scr[g * R * tq:(g + 1) * R * tq, 0:HEAD_DIM] = qg

    @pl.when(jnp.maximum(*open_ties) > 0.0)
    def _():
        for g in range(G):
            sel_scr[g] = _top_blocks(imp_scr[g], blk, n_sel, exact_ties=True)

    for g in range(G):
        block_bias = jnp.where((sel_scr[g] > 0.0) & ~future, 0.0, NEG).astype(BF16)
        for rr in range(R):
            qx_scr[(g * R + rr) * tq:(g * R + rr + 1) * tq, HEAD_DIM:HEAD_DIM + LANES] = block_bias

    nch = (t0 + tq + tk - 1) // tk
    lane_minus_block = (lax.broadcasted_iota(jnp.int32, (tk, LANES), 1)
                        - lax.broadcasted_iota(jnp.int32, (tk, LANES), 0) // SLC_BLK)

    def k_aug(c, g):
        s0 = pl.multiple_of(c * tk, tk)
        block_of_key = jnp.where(lane_minus_block == s0 // SLC_BLK, 1.0, 0.0).astype(BF16)
        return jnp.concatenate([ks_ref[0, g, pl.ds(s0, tk), :], block_of_key], axis=1)

    _flash_groups(nch, qx_scr, k_aug, lambda c, g: _with_ones(vs_ref[0, g, pl.ds(pl.multiple_of(c * tk, tk), tk), :]),
                  lambda c: _tile_ids((t0 - c * tk) // LANES, n_sub, nd),
                  toe_ref, lg_scrs, p_scrs, m_scr, acc_scr, tq)

    for hh in range(N_HEADS):
        rows = slice(hh * tq, (hh + 1) * tq)
        o_slc = acc_scr[rows, 0:HEAD_DIM] / acc_scr[rows, HEAD_DIM:2 * HEAD_DIM]
        o = ocw_scr[rows, :] + gates[:, 3 * hh + 1:3 * hh + 2] * o_slc
        o_ref[0, :, hh * HEAD_DIM:(hh + 1) * HEAD_DIM] = o.astype(o_ref.dtype)


def _selection_map(n_cmp, n_slc):
    c0 = CMP_STRIDE * np.arange(n_cmp)[:, None]
    s0 = SLC_BLK * np.arange(n_slc)[None, :]
    ov = np.clip(np.minimum(c0 + CMP_LEN, s0 + SLC_BLK) - np.maximum(c0, s0), 0, None)
    return (ov / CMP_LEN).astype(np.float32)


def _nsa_attn(q, gates, kc, vc, ks, vs, kw, vw, toe, tq=128, tk=512):
    B, _, T, _ = q.shape
    tk = min(tk, T)
    assert tq == LANES
    nd = toe.shape[0] - 2
    n_cmp_pad = kc.shape[2]
    n_cmp = (T - CMP_LEN) // CMP_STRIDE + 1
    n_slc = T // SLC_BLK
    assert n_slc <= LANES and T >= WIN + tq
    selmap = np.zeros((n_cmp_pad, LANES), np.float32)
    selmap[:n_cmp, :n_slc] = _selection_map(n_cmp, n_slc)
    kern = functools.partial(_nsa_attn_kernel, tq=tq, tk=tk, nd=nd, n_slc=n_slc, n_sel=min(N_SLC_MAX, n_slc))
    G, R = N_KV_GROUPS, HEADS_PER_GROUP
    width = max(n_cmp_pad, tk, WIN + tq)
    res = lambda n: pl.BlockSpec((1, G, n, HEAD_DIM), lambda b, i: (b, 0, 0, 0), pipeline_mode=pl.Buffered(1))
    return pl.pallas_call(
        kern,
        grid=(B, T // tq),
        in_specs=[
            pl.BlockSpec((1, N_HEADS, tq, HEAD_DIM), lambda b, i: (b, 0, i, 0)),
            pl.BlockSpec((1, tq, 3 * N_HEADS), lambda b, i: (b, i, 0)),
            res(n_cmp_pad), res(n_cmp_pad), res(T), res(T), res(T), res(T),
            _const_spec(selmap.shape),
            _const_spec(toe.shape),
        ],
        out_specs=pl.BlockSpec((1, tq, N_HEADS * HEAD_DIM), lambda b, i: (b, i, 0)),
        out_shape=jax.ShapeDtypeStruct((B, T, N_HEADS * HEAD_DIM), BF16),
        scratch_shapes=[
            pltpu.VMEM((G, tq, LANES), F32),
            pltpu.VMEM((G, tq, LANES), F32),
            pltpu.VMEM((R * tq, width), F32),
            pltpu.VMEM((R * tq, width), F32),
            pltpu.VMEM((R * tq, width), BF16),
            pltpu.VMEM((R * tq, width), BF16),
            pltpu.VMEM((R * tq, WIN + tq), F32),
            pltpu.VMEM((R * tq, WIN + tq), BF16),
            pltpu.VMEM((tq, n_cmp_pad), F32),
            pltpu.VMEM((N_HEADS * tq, HEAD_DIM + LANES), BF16),
            pltpu.VMEM((N_HEADS, tq, LANES), F32),
            pltpu.VMEM((N_HEADS * tq, 2 * HEAD_DIM), F32),
            pltpu.VMEM((N_HEADS * tq, HEAD_DIM), F32),
        ],
        compiler_params=_params(("parallel", "arbitrary")),
        name="nsa_attn",
    )(q, gates, kc, vc, ks, vs, kw, vw, jnp.asarray(selmap, BF16), toe)


def _dsa_attention(x, toe, g_attn, w_in, gq, gkv, gk, w_uq, w_q_idx, w_uk):
    qa, qi, ckv, kidx, wi = _dsa_proj(x, g_attn, w_in, gq, gkv, gk, w_uq, w_uk, w_q_idx)
    return _dsa_attn(qi, wi, kidx, qa, ckv, toe)


def _shared_kv(x, g_kv, w_kv, pos_k, pos_v, w1_k, w2_k, w1_v, w2_v):
    kc_raw, vc_raw, ks, vs, kw, vw = _kv_proj(x, g_kv, w_kv)
    return _compress(kc_raw, pos_k, w1_k, w2_k), _compress(vc_raw, pos_v, w1_v, w2_v), ks, vs, kw, vw


def _nsa_attention(x, toe, kv_shared, g_attn, w_in):
    q, gates = _nsa_proj(x, g_attn, w_in)
    return _nsa_attn(q, gates, *kv_shared, toe)


def kernel(x, g_attn, g_mlp, w_up, w_down, rel_bias, a_w_in, a_g_q_lat, a_g_kv_lat, a_g_k_idx, a_w_uq, a_w_q_idx, a_w_uk, a_w_uv, a_w_o, g_kv_shared, w_kv_shared, cmp_pos_k, cmp_pos_v, cmp_w1_k, cmp_w2_k, cmp_w1_v, cmp_w2_v, b_w_in, b_w_o, g_final):
    depth = g_attn.shape[0]
    n_a = a_w_in.shape[0]
    toe = _bias_tiles(rel_bias, x.shape[1])
    kv_shared = None
    for l in range(depth):
        if l < n_a:
            o = _dsa_attention(x, toe, g_attn[l], a_w_in[l], a_g_q_lat[l], a_g_kv_lat[l], a_g_k_idx[l],
                               a_w_uq[l], a_w_q_idx[l], a_w_uk[l])
            w_o, w_uv = a_w_o[l], a_w_uv[l]
        else:
            o = _nsa_attention(x, toe, kv_shared, g_attn[l], b_w_in[l - n_a])
            w_o, w_uv = b_w_o[l - n_a], None
        x = _out_mlp(o, x, w_o, g_mlp[l], w_up[l], w_down[l], w_uv, g_final if l == depth - 1 else None)
        if l == n_a - 1:
            kv_shared = _shared_kv(x, g_kv_shared, w_kv_shared, cmp_pos_k, cmp_pos_v,
                                   cmp_w1_k, cmp_w2_k, cmp_w1_v, cmp_w2_v)
    return x
```

```python
import functools
import math

import numpy as np
import jax
import jax.numpy as jnp
from jax import lax
from jax.experimental import pallas as pl
from jax.experimental.pallas import tpu as pltpu

N_HEADS = 8
HEAD_DIM = 128
Q_LORA = 256
KV_LORA = 128
IDX_HEADS = 8
IDX_DIM = 64
IDX_TOPK_MAX = 256
N_KV_GROUPS = 2
HEADS_PER_GROUP = N_HEADS // N_KV_GROUPS
CMP_LEN = 32
CMP_STRIDE = 16
CMP_HID = 256
SLC_BLK = 64
N_SLC_MAX = 16
WIN = 512
REL_BUCKETS = 32
REL_MAX_DIST = 4096
EPS = 1e-6
NEG = -1e30
LOG2E = math.log2(math.e)

LANES = 128
ROWS = 16
KEY_ROWS = 64
INT_MIN = np.int32(-2 ** 31)
INT_MAX = np.int32(2 ** 31 - 1)
FIXED_PROBES = 15
STRAGGLER_PROBES = 40
VMEM_LIMIT = 56 * 1024 * 1024

F32 = jnp.float32
BF16 = jnp.bfloat16
NT_DIMS = (((1,), (1,)), ((), ()))


def _dot(a, b):
    return jnp.dot(a, b, preferred_element_type=F32)


def _dot_nt(a, b):
    return lax.dot_general(a, b, NT_DIMS, preferred_element_type=F32)


def _rms(x, g):
    return x * lax.rsqrt(jnp.mean(x * x, axis=-1, keepdims=True) + EPS) * g


def _to_key(v):
    bits = lax.bitcast_convert_type(v, jnp.int32)
    return bits ^ ((bits >> 31) & jnp.int32(0x7FFFFFFF))


def _const_spec(shape):
    nd = len(shape)
    return pl.BlockSpec(shape, lambda *_: (0,) * nd, pipeline_mode=pl.Buffered(1))


def _params(sem):
    return pltpu.CompilerParams(dimension_semantics=sem, vmem_limit_bytes=VMEM_LIMIT)


def _rel_bucket(dist):
    dist = jnp.maximum(dist, 0)
    exact = REL_BUCKETS // 2
    log_ratio = jnp.log(jnp.maximum(dist, 1).astype(F32) / exact) / math.log(REL_MAX_DIST / exact)
    large = exact + (log_ratio * (REL_BUCKETS - exact)).astype(jnp.int32)
    return jnp.where(dist < exact, dist, jnp.minimum(large, REL_BUCKETS - 1))


def _num_bias_tiles(T):
    exact = REL_BUCKETS // 2
    switch = exact * (REL_MAX_DIST / exact) ** ((REL_BUCKETS - exact - 1) / (REL_BUCKETS - exact))
    far = int(math.ceil(switch)) + 32
    return min(T // LANES, -(-(far + LANES - 1) // LANES) + 1)


def _bias_tiles(rel_bias, T):
    nd = _num_bias_tiles(T)
    assert nd > WIN // LANES
    c = LANES * (nd - 1)
    bd = rel_bias[_rel_bucket(jnp.arange(c + LANES, dtype=jnp.int32))].T.astype(F32) * LOG2E
    p = c + 2 * LANES
    w = jnp.concatenate([bd[:, c::-1], jnp.broadcast_to(bd[:, :1], (N_HEADS, LANES)), bd[:, :c:-1]], axis=1)
    band = jnp.tile(w, (1, LANES))[:, :LANES * (p - 1)].reshape(N_HEADS, LANES, p - 1)[:, :, :c + LANES]
    tiles = band.reshape(N_HEADS, LANES, nd, LANES)[:, :, ::-1]
    tiles = jnp.transpose(tiles, (2, 0, 1, 3))
    i = jnp.arange(LANES)[:, None]
    j = jnp.arange(LANES)[None, :]
    ahead = jnp.full((1,) + tiles.shape[1:], NEG, F32)
    diag = jnp.where(i >= j, tiles[:1], NEG)
    win_edge = jnp.where(i < j, tiles[WIN // LANES:WIN // LANES + 1], NEG)
    return jnp.concatenate([ahead, diag, tiles[1:], win_edge], axis=0)


def _tile_ids(d0, n_sub, nd):
    return [jnp.clip(d0 - k, -1, nd - 1) + 1 for k in range(n_sub)]


def _dsa_proj_kernel(x_ref, g_ref, w_in_ref, w_wt_ref, gq_ref, gkv_ref, gk_ref, w_uq_ref, w_ukt_ref, w_qi_ref,
                     qa_ref, qi_ref, ckv_ref, kidx_ref, wi_ref):
    h = _rms(x_ref[0], g_ref[...]).astype(BF16)
    proj = _dot(h, w_in_ref[...])
    c_q = _rms(proj[:, :Q_LORA], gq_ref[...]).astype(BF16)
    ckv_ref[0] = _rms(proj[:, Q_LORA:Q_LORA + KV_LORA], gkv_ref[...]).astype(BF16)
    o_k = Q_LORA + KV_LORA
    kidx_ref[0] = _rms(proj[:, o_k:o_k + IDX_DIM], gk_ref[...]).astype(BF16)
    wi_ref[0] = _dot_nt(w_wt_ref[...], h) * (IDX_HEADS ** -0.5 * IDX_DIM ** -0.5)
    q = _dot(c_q, w_uq_ref[...])
    for hh in range(N_HEADS):
        qh = q[:, hh * HEAD_DIM:(hh + 1) * HEAD_DIM].astype(BF16)
        qa_ref[0, hh] = (_dot(qh, w_ukt_ref[hh]) * (HEAD_DIM ** -0.5 * LOG2E)).astype(BF16)
        qi_ref[0, hh] = _dot(c_q, w_qi_ref[hh]).astype(BF16)


def _dsa_proj(x, g, w_in, gq, gkv, gk, w_uq, w_uk, w_q_idx, tm=256):
    B, T, D = x.shape
    n_in = Q_LORA + KV_LORA + LANES
    o_k = Q_LORA + KV_LORA
    w_in_p = jnp.zeros((D, n_in), F32).at[:, :o_k + IDX_DIM].set(w_in[:, :o_k + IDX_DIM])
    w_wt = w_in[:, o_k + IDX_DIM:].T.astype(BF16)
    w_ukt = jnp.transpose(w_uk, (1, 2, 0)).astype(BF16)
    w_qi = jnp.transpose(w_q_idx, (1, 0, 2)).astype(BF16)
    tile = lambda b, i: (b, i, 0)
    htile = lambda b, i: (b, 0, i, 0)
    return pl.pallas_call(
        _dsa_proj_kernel,
        grid=(B, T // tm),
        in_specs=[
            pl.BlockSpec((1, tm, D), tile),
            _const_spec((1, D)),
            _const_spec((D, n_in)),
            _const_spec((IDX_HEADS, D)),
            _const_spec((1, Q_LORA)), _const_spec((1, KV_LORA)), _const_spec((1, IDX_DIM)),
            _const_spec((Q_LORA, N_HEADS * HEAD_DIM)),
            _const_spec((N_HEADS, HEAD_DIM, KV_LORA)),
            _const_spec((IDX_HEADS, Q_LORA, IDX_DIM)),
        ],
        out_specs=[
            pl.BlockSpec((1, N_HEADS, tm, KV_LORA), htile),
            pl.BlockSpec((1, IDX_HEADS, tm, IDX_DIM), htile),
            pl.BlockSpec((1, tm, KV_LORA), tile),
            pl.BlockSpec((1, tm, IDX_DIM), tile),
            pl.BlockSpec((1, IDX_HEADS, tm), lambda b, i: (b, 0, i)),
        ],
        out_shape=[
            jax.ShapeDtypeStruct((B, N_HEADS, T, KV_LORA), BF16),
            jax.ShapeDtypeStruct((B, IDX_HEADS, T, IDX_DIM), BF16),
            jax.ShapeDtypeStruct((B, T, KV_LORA), BF16),
            jax.ShapeDtypeStruct((B, T, IDX_DIM), BF16),
            jax.ShapeDtypeStruct((B, IDX_HEADS, T), F32),
        ],
        compiler_params=_params(("parallel", "parallel")),
        name="dsa_proj",
    )(x, g.reshape(1, D), w_in_p.astype(BF16), w_wt, gq.reshape(1, -1), gkv.reshape(1, -1), gk.reshape(1, -1),
      w_uq.reshape(Q_LORA, N_HEADS * HEAD_DIM).astype(BF16), w_ukt, w_qi)


def _tile_lanes(a, n):
    return a if n == 1 else jnp.concatenate([a] * n, axis=1)


def _bias_rows(toe_ref, head, tiles, rows):
    parts = [toe_ref[d, head, rows, :] for d in tiles]
    return parts[0] if len(parts) == 1 else jnp.concatenate(parts, axis=1)


def _flash_head(lg_scr, lrow0, toe_ref, tiles, head, m_scr, acc_scr, p_scr, prow0, tq):
    n_rep = len(tiles)
    tk = n_rep * LANES
    for r in range(tq // ROWS):
        rows = slice(r * ROWS, (r + 1) * ROWS)
        lg = lg_scr[lrow0 + r * ROWS:lrow0 + (r + 1) * ROWS, 0:tk] + _bias_rows(toe_ref, head, tiles, rows)
        m_old = m_scr[head, rows]
        m_new = jnp.maximum(m_old, jnp.max(lg, axis=1, keepdims=True))
        m_scr[head, rows] = m_new
        p_scr[prow0 + r * ROWS:prow0 + (r + 1) * ROWS, 0:tk] = jnp.exp2(lg - _tile_lanes(m_new, n_rep)).astype(BF16)
        arows = slice(head * tq + r * ROWS, head * tq + (r + 1) * ROWS)
        acc_scr[arows, :] = acc_scr[arows, :] * _tile_lanes(jnp.exp2(m_old - m_new), 2)


def _with_ones(v):
    return jnp.concatenate([v, jnp.ones(v.shape, v.dtype)], axis=1)


def _flash_groups(nch, qx_scr, k_aug, v_aug, tile_ids, toe_ref, lg_scrs, p_scrs, m_scr, acc_scr, tq):
    R = HEADS_PER_GROUP
    grows = [slice(g * R * tq, (g + 1) * R * tq) for g in range(N_KV_GROUPS)]

    def logits(c, g):
        k = k_aug(c, g)
        lg_scrs[g][:, 0:k.shape[0]] = _dot_nt(qx_scr[grows[g], :], k)

    def softmax(c, g):
        tiles = tile_ids(c)
        for rr in range(R):
            _flash_head(lg_scrs[g], rr * tq, toe_ref, tiles, g * R + rr, m_scr, acc_scr, p_scrs[g], rr * tq, tq)

    def values(c, g):
        v = v_aug(c, g)
        acc_scr[grows[g], :] = acc_scr[grows[g], :] + _dot(p_scrs[g][:, 0:v.shape[0]], v)

    m_scr[...] = jnp.full(m_scr.shape, NEG, F32)
    acc_scr[...] = jnp.zeros(acc_scr.shape, F32)
    p_scrs[1][...] = jnp.zeros(p_scrs[1].shape, BF16)
    logits(0, 0)

    def chunk(c, carry):
        logits(c, 1)
        softmax(c, 0)
        values(jnp.maximum(c - 1, 0), 1)
        logits(jnp.minimum(c + 1, nch - 1), 0)
        softmax(c, 1)
        values(c, 0)
        return carry

    lax.fori_loop(0, nch, chunk, 0)
    values(nch - 1, 1)


def _fold_rows(a, n, op=jnp.add):
    parts = [a[i:i + n] for i in range(0, a.shape[0], n)]
    while len(parts) > 1:
        parts = [op(parts[i], parts[i + 1]) for i in range(0, len(parts), 2)]
    return parts[0]


def _from_key(k):
    return lax.bitcast_convert_type(k ^ ((k >> 31) & jnp.int32(0x7FFFFFFF)), F32)


def _dsa_attn_kernel(qi_ref, wi_ref, kidx_ref, qa_ref, ckv_ref, toe_ref, o_ref,
                     s_scr, rel0_scr, rel1_scr, lg0_scr, lg1_scr, p0_scr, p1_scr, qx_scr, m_scr, acc_scr,
                     *, tq, tk, top_k, nd, idx_bits):
    t0 = pl.program_id(1) * tq
    nch = (t0 + tq + tk - 1) // tk
    n_sub = tk // LANES
    sub = 8
    key_row = lax.broadcasted_iota(jnp.int32, (tk, tq), 0)

    wi = wi_ref[0]
    qi = qi_ref[0].reshape(IDX_HEADS * tq, IDX_DIM)
    k_slab = lax.broadcasted_iota(jnp.int32, (KEY_ROWS, tq), 0)
    q_slab = lax.broadcasted_iota(jnp.int32, (KEY_ROWS, tq), 1)

    def head_dots(c, rel_scr):
        s0 = pl.multiple_of(jnp.minimum(c, nch - 1) * tk, tk)
        rel_scr[...] = _dot_nt(kidx_ref[0, pl.ds(s0, tk), :], qi)

    def scores(c, rel_scr, top):
        s0 = c * tk
        for r in range(tk // KEY_ROWS):
            rows = slice(r * KEY_ROWS, (r + 1) * KEY_ROWS)
            acc = jnp.zeros((KEY_ROWS, tq), F32)
            for hh in range(IDX_HEADS):
                acc = acc + wi[hh:hh + 1, :] * jnp.maximum(rel_scr[rows, hh * tq:(hh + 1) * tq], 0.0)
            acc = acc + 0.0
            valid = k_slab + (s0 + r * KEY_ROWS) <= q_slab + t0
            s_scr[c, rows, :] = jnp.where(valid, _to_key(acc), INT_MIN)
            top = jnp.maximum(top, _fold_rows(jnp.where(valid, jnp.abs(acc), 0.0), sub, jnp.maximum))
        return top

    def score_pair(j, top):
        head_dots(2 * j + 1, rel1_scr)
        top = scores(2 * j, rel0_scr, top)
        head_dots(2 * j + 2, rel0_scr)
        return scores(2 * j + 1, rel1_scr, top)

    head_dots(0, rel0_scr)
    top = lax.fori_loop(0, (nch + 1) // 2, score_pair, jnp.zeros((sub, tq), F32))
    top = jnp.max(top, axis=0, keepdims=True)

    def count(pred):
        def body(c, cnt):
            return cnt + _fold_rows(jnp.where(pred(s_scr[c], key_row + c * tk), 1.0, 0.0), sub)
        cnt = lax.fori_loop(0, nch, body, jnp.zeros((sub, tq), F32))
        return jnp.sum(cnt, axis=0, keepdims=True)

    kf = jnp.float32(top_k)

    def settled(st):
        lo, hi, n_lo, _ = st
        return (n_lo <= kf) | (hi - 1 <= lo)

    def score_mid(st):
        lo_v, hi_v = _from_key(st[0]), _from_key(st[1])
        return _to_key(lo_v + (hi_v - lo_v) * 0.5 + 0.0)

    def probe(st, want):
        lo, hi, n_lo, n_hi = st
        done = settled(st)
        cand = jnp.where((want > lo) & (want < hi), want, (lo >> 1) + (hi >> 1) + (lo & hi & 1))
        cand = jnp.where(done, lo, cand)
        cnt = count(lambda blk, idx: blk >= cand)
        up = (cnt >= kf) & ~done
        down = ~(up | done)
        return (jnp.where(up, cand, lo), jnp.where(down, cand, hi),
                jnp.where(up, cnt, n_lo), jnp.where(down, cnt, n_hi))

    n0 = (t0 + 1 + lax.broadcasted_iota(jnp.int32, (1, tq), 1)).astype(F32)
    st = (_to_key(-top), _to_key(top) + 1, n0, jnp.zeros((1, tq), F32))
    st = lax.fori_loop(0, FIXED_PROBES, lambda i, s: probe(s, score_mid(s)), st)

    lo, hi, n_lo, n_hi = st

    def inside_range(c, carry):
        blk = s_scr[c]
        inside = (blk >= lo) & (blk < hi)
        return (jnp.maximum(carry[0], _fold_rows(jnp.where(inside, blk, INT_MIN), sub, jnp.maximum)),
                jnp.minimum(carry[1], _fold_rows(jnp.where(inside, blk, INT_MAX), sub, jnp.minimum)))

    big, small = lax.fori_loop(0, nch, inside_range, (jnp.full((sub, tq), INT_MIN, jnp.int32),
                                                      jnp.full((sub, tq), INT_MAX, jnp.int32)))
    big = _fold_rows(big, 1, jnp.maximum)
    small = _fold_rows(small, 1, jnp.minimum)
    flat = (big == small) & ~settled(st)
    st = (jnp.where(flat, big, lo), jnp.where(flat, big + 1, hi), n_lo, n_hi)
    want = jnp.where(kf - n_hi == 1.0, big, jnp.where(kf - n_hi == n_lo - n_hi - 1.0, small + 1, score_mid(st)))
    st = probe(st, want)

    def unsettled(st):
        return jnp.sum(jnp.where(settled(st), 0.0, 1.0))

    def more_cond(c):
        return (c[0] < STRAGGLER_PROBES) & (c[1] > 0.0)

    def more_probe(c):
        it, _, st = c
        st = probe(st, jnp.where(it < 4, score_mid(st), st[0]))
        return it + 1, unsettled(st), st

    _, _, st = lax.while_loop(more_cond, more_probe, (jnp.int32(0), unsettled(st), st))
    thr, n_ge = st[0], st[2]

    @pl.when(jnp.max(n_ge) > kf)
    def _():
        need = kf - count(lambda blk, idx: blk > thr)

        def idx_bit(bi, last):
            cand = last | lax.shift_left(jnp.int32(1), idx_bits - 1 - bi)
            cnt = count(lambda blk, idx: (blk == thr) & (idx < cand))
            return jnp.where(cnt < need, cand, last)

        last = lax.fori_loop(0, idx_bits, idx_bit, jnp.zeros((1, tq), jnp.int32))

        def drop(c, carry):
            blk = s_scr[c]
            s_scr[c] = jnp.where((blk == thr) & (key_row + c * tk > last), INT_MIN, blk)
            return carry

        lax.fori_loop(0, nch, drop, 0)

    eye = (lax.broadcasted_iota(jnp.int32, (tq, tq), 0) == lax.broadcasted_iota(jnp.int32, (tq, tq), 1))
    qx_scr[:, 0:KV_LORA] = qa_ref[0].reshape(N_HEADS * tq, KV_LORA)
    for hh in range(N_HEADS):
        qx_scr[hh * tq:(hh + 1) * tq, KV_LORA:KV_LORA + tq] = jnp.where(eye, 1.0, 0.0).astype(BF16)

    def latents(c):
        return ckv_ref[0, pl.ds(pl.multiple_of(c * tk, tk), tk), :]

    def k_aug(c, g):
        mask_t = jnp.where(s_scr[c] >= thr, 0.0, NEG).astype(BF16)
        return jnp.concatenate([latents(c), mask_t], axis=1)

    _flash_groups(nch, qx_scr, k_aug, lambda c, g: _with_ones(latents(c)),
                  lambda c: _tile_ids((t0 - c * tk) // LANES, n_sub, nd),
                  toe_ref, (lg0_scr, lg1_scr), (p0_scr, p1_scr), m_scr, acc_scr, tq)
    for hh in range(N_HEADS):
        rows = slice(hh * tq, (hh + 1) * tq)
        o = acc_scr[rows, 0:KV_LORA] / acc_scr[rows, KV_LORA:2 * KV_LORA]
        o_ref[0, :, hh * KV_LORA:(hh + 1) * KV_LORA] = o.astype(o_ref.dtype)


def _dsa_attn(qi, wi, kidx, qa, ckv, toe, tq=128, tk=512):
    B, _, T, _ = qa.shape
    tk = min(tk, T)
    assert tq == LANES
    nd = toe.shape[0] - 2
    top_k = min(IDX_TOPK_MAX, T // 4)
    kern = functools.partial(_dsa_attn_kernel, tq=tq, tk=tk, top_k=top_k, nd=nd,
                             idx_bits=max(1, (T - 1).bit_length()))
    return pl.pallas_call(
        kern,
        grid=(B, T // tq),
        in_specs=[
            pl.BlockSpec((1, IDX_HEADS, tq, IDX_DIM), lambda b, i: (b, 0, i, 0)),
            pl.BlockSpec((1, IDX_HEADS, tq), lambda b, i: (b, 0, i)),
            pl.BlockSpec((1, T, IDX_DIM), lambda b, i: (b, 0, 0), pipeline_mode=pl.Buffered(1)),
            pl.BlockSpec((1, N_HEADS, tq, KV_LORA), lambda b, i: (b, 0, i, 0)),
            pl.BlockSpec((1, T, KV_LORA), lambda b, i: (b, 0, 0), pipeline_mode=pl.Buffered(1)),
            _const_spec(toe.shape),
        ],
        out_specs=pl.BlockSpec((1, tq, N_HEADS * KV_LORA), lambda b, i: (b, i, 0)),
        out_shape=jax.ShapeDtypeStruct((B, T, N_HEADS * KV_LORA), BF16),
        scratch_shapes=[
            pltpu.VMEM((T // tk + 1, tk, tq), jnp.int32),
            pltpu.VMEM((tk, IDX_HEADS * tq), F32),
            pltpu.VMEM((tk, IDX_HEADS * tq), F32),
            pltpu.VMEM((HEADS_PER_GROUP * tq, tk), F32),
            pltpu.VMEM((HEADS_PER_GROUP * tq, tk), F32),
            pltpu.VMEM((HEADS_PER_GROUP * tq, tk), BF16),
            pltpu.VMEM((HEADS_PER_GROUP * tq, tk), BF16),
            pltpu.VMEM((N_HEADS * tq, KV_LORA + tq), BF16),
            pltpu.VMEM((N_HEADS, tq, LANES), F32),
            pltpu.VMEM((N_HEADS * tq, 2 * KV_LORA), F32),
        ],
        compiler_params=_params(("parallel", "arbitrary")),
        name="dsa_attn",
    )(qi, wi, kidx, qa, ckv, toe)


def _out_mlp_kernel(o_ref, x_ref, w_o_ref, g_ref, w_up_ref, w_down_ref, *rest, tf, latent, final):
    y_ref = rest[-1]
    o = o_ref[0]
    if latent:
        o = jnp.concatenate([_dot(o[:, hh * KV_LORA:(hh + 1) * KV_LORA], rest[0][hh]).astype(BF16)
                             for hh in range(N_HEADS)], axis=1)
    x = x_ref[0] + _dot(o, w_o_ref[...])
    h = _rms(x, g_ref[...]).astype(BF16)
    acc = x
    for f0 in range(0, w_up_ref.shape[1], tf):
        u = jnp.maximum(_dot(h, w_up_ref[:, f0:f0 + tf]), 0.0)
        acc = acc + _dot((u * u).astype(BF16), w_down_ref[f0:f0 + tf, :])
    y_ref[0] = _rms(acc, rest[-2][...]) if final else acc


def _out_mlp(o, x, w_o, g, w_up, w_down, w_uv=None, g_final=None, tm=512, tf=512):
    B, T, D = x.shape
    F = w_up.shape[1]
    tile = lambda b, i: (b, i, 0)
    latent, final = w_uv is not None, g_final is not None
    in_specs = [pl.BlockSpec((1, tm, o.shape[-1]), tile), pl.BlockSpec((1, tm, D), tile), _const_spec(w_o.shape),
                _const_spec((1, D)), _const_spec((D, F)), _const_spec((F, D))]
    args = [o, x, w_o.astype(BF16), g.reshape(1, D), w_up.astype(BF16), w_down.astype(BF16)]
    if latent:
        in_specs.append(_const_spec((N_HEADS, KV_LORA, HEAD_DIM)))
        args.append(jnp.transpose(w_uv, (1, 0, 2)).astype(BF16))
    if final:
        in_specs.append(_const_spec((1, D)))
        args.append(g_final.reshape(1, D))
    return pl.pallas_call(
        functools.partial(_out_mlp_kernel, tf=tf, latent=latent, final=final),
        grid=(B, T // tm),
        in_specs=in_specs,
        out_specs=pl.BlockSpec((1, tm, D), tile),
        out_shape=jax.ShapeDtypeStruct((B, T, D), F32),
        compiler_params=_params(("parallel", "parallel")),
        name="out_mlp_final" if final else "out_mlp",
    )(*args)


def _kv_proj_kernel(x_ref, g_ref, w_ref, kc_ref, vc_ref, ks_ref, vs_ref, kw_ref, vw_ref):
    h = _rms(x_ref[0], g_ref[...]).astype(BF16)
    kv = _dot(h, w_ref[...])
    for part, ref in enumerate((kc_ref, vc_ref, ks_ref, vs_ref, kw_ref, vw_ref)):
        for g in range(N_KV_GROUPS):
            o = (part * N_KV_GROUPS + g) * HEAD_DIM
            ref[0, g] = kv[:, o:o + HEAD_DIM].astype(ref.dtype)


def _kv_proj(x, g, w_kv, tm=512):
    B, T, D = x.shape
    gtile = lambda b, i: (b, 0, i, 0)
    spec = pl.BlockSpec((1, N_KV_GROUPS, tm, HEAD_DIM), gtile)
    shp = lambda dt: jax.ShapeDtypeStruct((B, N_KV_GROUPS, T, HEAD_DIM), dt)
    return pl.pallas_call(
        _kv_proj_kernel,
        grid=(B, T // tm),
        in_specs=[pl.BlockSpec((1, tm, D), lambda b, i: (b, i, 0)), _const_spec((1, D)),
                  _const_spec(w_kv.shape)],
        out_specs=[spec] * 6,
        out_shape=[shp(F32), shp(F32), shp(BF16), shp(BF16), shp(BF16), shp(BF16)],
        compiler_params=_params(("parallel", "parallel")),
        name="kv_proj",
    )(x, g.reshape(1, D), w_kv.astype(BF16))


def _compress_kernel(raw_ref, pos_ref, w1_ref, w2_ref, o_ref):
    rows = raw_ref[0, 0]
    half = rows.shape[1]
    a = _dot((rows + pos_ref[:, :half]).astype(BF16), w1_ref[:half, :])
    b = _dot((rows + pos_ref[:, half:]).astype(BF16), w1_ref[half:, :])
    pre = a + pltpu.roll(b, rows.shape[0] - 1, 0)
    act = 0.5 * pre * (1.0 + jnp.tanh(math.sqrt(2.0 / math.pi) * (pre + 0.044715 * (pre * pre * pre))))
    o_ref[0, 0] = _dot(act.astype(BF16), w2_ref[...]).astype(o_ref.dtype)


def _compress(raw, pos, w1, w2):
    B, G, T, Dh = raw.shape
    nr = T // CMP_STRIDE
    rows = raw.reshape(B, G, nr, CMP_STRIDE * Dh)
    return pl.pallas_call(
        _compress_kernel,
        grid=(B, G),
        in_specs=[pl.BlockSpec((1, 1, nr, CMP_STRIDE * Dh), lambda b, g: (b, g, 0, 0)),
                  _const_spec((1, CMP_LEN * Dh)), _const_spec(w1.shape), _const_spec(w2.shape)],
        out_specs=pl.BlockSpec((1, 1, nr, Dh), lambda b, g: (b, g, 0, 0)),
        out_shape=jax.ShapeDtypeStruct((B, G, nr, Dh), BF16),
        compiler_params=_params(("parallel", "parallel")),
        name="compress",
    )(rows, pos.reshape(1, CMP_LEN * Dh), w1.astype(BF16), w2.astype(BF16))


def _nsa_proj_kernel(x_ref, g_ref, wq_ref, wg_ref, q_ref, gate_ref):
    h = _rms(x_ref[0], g_ref[...]).astype(BF16)
    q = _dot(h, wq_ref[...]) * (HEAD_DIM ** -0.5 * LOG2E)
    for hh in range(N_HEADS):
        q_ref[0, hh] = q[:, hh * HEAD_DIM:(hh + 1) * HEAD_DIM].astype(BF16)
    gate_ref[0] = jax.nn.sigmoid(_dot(h, wg_ref[...]))[:, :3 * N_HEADS]


def _nsa_proj(x, g, w_in, tm=512):
    B, T, D = x.shape
    hd = N_HEADS * HEAD_DIM
    wq = w_in[:, :hd].astype(BF16)
    wg = jnp.zeros((D, LANES), F32).at[:, :3 * N_HEADS].set(w_in[:, hd:]).astype(BF16)
    return pl.pallas_call(
        _nsa_proj_kernel,
        grid=(B, T // tm),
        in_specs=[pl.BlockSpec((1, tm, D), lambda b, i: (b, i, 0)), _const_spec((1, D)),
                  _const_spec((D, hd)), _const_spec((D, LANES))],
        out_specs=[pl.BlockSpec((1, N_HEADS, tm, HEAD_DIM), lambda b, i: (b, 0, i, 0)),
                   pl.BlockSpec((1, tm, 3 * N_HEADS), lambda b, i: (b, i, 0))],
        out_shape=[jax.ShapeDtypeStruct((B, N_HEADS, T, HEAD_DIM), BF16),
                   jax.ShapeDtypeStruct((B, T, 3 * N_HEADS), F32)],
        compiler_params=_params(("parallel", "parallel")),
        name="nsa_proj",
    )(x, g.reshape(1, D), wq, wg)


def _top_blocks(imp, blk, n_sel, exact_ties):
    sel = jnp.zeros(imp.shape, F32)
    blk_f = blk.astype(F32)
    for _ in range(n_sel):
        if exact_ties:
            best = jnp.max(imp, axis=1, keepdims=True)
            hit = blk_f == jnp.min(jnp.where(imp == best, blk_f, float(LANES)), axis=1, keepdims=True)
        else:
            hit = blk == jnp.argmax(imp, axis=1, keepdims=True).astype(jnp.int32)
        sel = jnp.where(hit, 1.0, sel)
        imp = jnp.where(hit, -jnp.inf, imp)
    return sel


def _nsa_attn_kernel(q_ref, gate_ref, kc_ref, vc_ref, ks_ref, vs_ref, kw_ref, vw_ref, toe_ref,
                     o_ref, sel_scr, imp_scr, lg0_scr, lg1_scr, p0_scr, p1_scr, lw_scr, pw_scr, qx_scr, m_scr, acc_scr,
                     ocw_scr, *, tq, tk, nd, n_slc, n_sel):
    G, R = N_KV_GROUPS, HEADS_PER_GROUP
    lg_scrs, p_scrs = (lg0_scr, lg1_scr), (p0_scr, p1_scr)
    t0 = pl.program_id(1) * tq
    n_cmp = kc_ref.shape[2]
    n_sub = tk // LANES
    wlen = WIN + tq
    gates = gate_ref[0]
    t_col = t0 + lax.broadcasted_iota(jnp.int32, (tq, 1), 0)
    t_rows = t0 + lax.broadcasted_iota(jnp.int32, (ROWS, 1), 0)
    slabs = [slice(r * ROWS, (r + 1) * ROWS) for r in range(tq // ROWS)]

    cmp_end = CMP_STRIDE * lax.broadcasted_iota(jnp.int32, (ROWS, n_cmp), 1) + (CMP_LEN - 1)
    blk = lax.broadcasted_iota(jnp.int32, (tq, LANES), 1)
    cur = t_col // SLC_BLK
    forced = (blk == 0) | (blk == cur) | (blk == cur - 1)
    future = blk * SLC_BLK > t_col
    start = pl.multiple_of(jnp.maximum(t0 - WIN, 0), LANES)
    dw = (t0 - start) // LANES
    wtiles = _tile_ids(dw, wlen // LANES, nd)
    wtiles[0] = jnp.where(dw == WIN // LANES, nd + 1, wtiles[0])
    open_ties = []
    for g in range(G):
        lg_scr, p_scr = lg_scrs[g], p_scrs[g]
        qg = q_ref[0, g * R:(g + 1) * R].reshape(R * tq, HEAD_DIM)
        lg_scr[0:R * tq, 0:n_cmp] = _dot_nt(qg, kc_ref[0, g])
        for r, rows in enumerate(slabs):
            cmask = cmp_end <= t_rows + r * ROWS
            for rr in range(R):
                lc = jnp.where(cmask, lg_scr[rr * tq + r * ROWS:rr * tq + (r + 1) * ROWS, 0:n_cmp], NEG)
                e = jnp.where(cmask, jnp.exp2(lc - jnp.max(lc, axis=1, keepdims=True)), 0.0)
                p_scr[rr * tq + r * ROWS:rr * tq + (r + 1) * ROWS, 0:n_cmp] = e.astype(BF16)
        cmp_all = _dot(p_scr[:, 0:n_cmp], vc_ref[0, g])
        inv = 1.0 / jnp.maximum(cmp_all[:, HEAD_DIM:2 * HEAD_DIM], 1e-30)
        o_cmp = cmp_all[:, 0:HEAD_DIM] * inv
        imp_heads = cmp_all[:, 2 * HEAD_DIM:3 * HEAD_DIM] * inv

        lw_scr[0:R * tq, 0:wlen] = _dot_nt(qg, kw_ref[0, g, pl.ds(start, wlen), :])
        for rr in range(R):
            for r, rows in enumerate(slabs):
                lw = (lw_scr[rr * tq + r * ROWS:rr * tq + (r + 1) * ROWS, 0:wlen]
                      + _bias_rows(toe_ref, g * R + rr, wtiles, rows))
                pw = jnp.exp2(lw - jnp.max(lw, axis=1, keepdims=True))
                pw_scr[rr * tq + r * ROWS:rr * tq + (r + 1) * ROWS, 0:wlen] = pw.astype(BF16)
        o_win = _dot(pw_scr[:, 0:wlen], _with_ones(vw_ref[0, g, pl.ds(start, wlen), :]))
        for rr in range(R):
            hh = g * R + rr
            wrows = slice(rr * tq, (rr + 1) * tq)
            ocw_scr[hh * tq:(hh + 1) * tq, :] = (
                gates[:, 3 * hh:3 * hh + 1] * o_cmp[wrows]
                + gates[:, 3 * hh + 2:3 * hh + 3] * (o_win[wrows, 0:HEAD_DIM] / o_win[wrows, HEAD_DIM:2 * HEAD_DIM]))
        imp = _fold_rows(imp_heads, tq)
        imp = jnp.where(forced, 1e9, imp)
        imp = jnp.where(future, NEG, imp)
        imp = jnp.where(blk < n_slc, imp, -jnp.inf)
        sel = _top_blocks(imp, blk, n_sel, exact_ties=False)
        sel_scr[g] = sel
        imp_scr[g] = imp
        cut = jnp.min(jnp.where(sel > 0.0, imp, jnp.inf), axis=1, keepdims=True)
        open_ties.append(jnp.max(jnp.where((imp == cut) & (sel == 0.0) & ~future, 1.0, 0.0)))
        qx_scr[g * R * tq:(g + 1) * R * tq, 0:HEAD_DIM] = qg

    @pl.when(jnp.maximum(*open_ties) > 0.0)
    def _():
        for g in range(G):
            sel_scr[g] = _top_blocks(imp_scr[g], blk, n_sel, exact_ties=True)

    for g in range(G):
        block_bias = jnp.where((sel_scr[g] > 0.0) & ~future, 0.0, NEG).astype(BF16)
        for rr in range(R):
            qx_scr[(g * R + rr) * tq:(g * R + rr + 1) * tq, HEAD_DIM:HEAD_DIM + LANES] = block_bias

    nch = (t0 + tq + tk - 1) // tk
    lane_minus_block = (lax.broadcasted_iota(jnp.int32, (tk, LANES), 1)
                        - lax.broadcasted_iota(jnp.int32, (tk, LANES), 0) // SLC_BLK)

    def k_aug(c, g):
        s0 = pl.multiple_of(c * tk, tk)
        block_of_key = jnp.where(lane_minus_block == s0 // SLC_BLK, 1.0, 0.0).astype(BF16)
        return jnp.concatenate([ks_ref[0, g, pl.ds(s0, tk), :], block_of_key], axis=1)

    _flash_groups(nch, qx_scr, k_aug, lambda c, g: _with_ones(vs_ref[0, g, pl.ds(pl.multiple_of(c * tk, tk), tk), :]),
                  lambda c: _tile_ids((t0 - c * tk) // LANES, n_sub, nd),
                  toe_ref, lg_scrs, p_scrs, m_scr, acc_scr, tq)

    for hh in range(N_HEADS):
        rows = slice(hh * tq, (hh + 1) * tq)
        o_slc = acc_scr[rows, 0:HEAD_DIM] / acc_scr[rows, HEAD_DIM:2 * HEAD_DIM]
        o = ocw_scr[rows, :] + gates[:, 3 * hh + 1:3 * hh + 2] * o_slc
        o_ref[0, :, hh * HEAD_DIM:(hh + 1) * HEAD_DIM] = o.astype(o_ref.dtype)


def _selection_map(n_cmp, n_slc):
    c0 = CMP_STRIDE * np.arange(n_cmp)[:, None]
    s0 = SLC_BLK * np.arange(n_slc)[None, :]
    ov = np.clip(np.minimum(c0 + CMP_LEN, s0 + SLC_BLK) - np.maximum(c0, s0), 0, None)
    return (ov / CMP_LEN).astype(np.float32)


def _nsa_attn(q, gates, kc, vc, ks, vs, kw, vw, toe, tq=128, tk=512):
    B, _, T, _ = q.shape
    tk = min(tk, T)
    assert tq == LANES
    nd = toe.shape[0] - 2
    n_cmp_pad = kc.shape[2]
    n_cmp = (T - CMP_LEN) // CMP_STRIDE + 1
    n_slc = T // SLC_BLK
    assert n_slc <= LANES and T >= WIN + tq
    selmap = np.zeros((n_cmp_pad, LANES), np.float32)
    selmap[:n_cmp, :n_slc] = _selection_map(n_cmp, n_slc)
    vc_all = jnp.concatenate([vc, jnp.ones_like(vc), jnp.broadcast_to(jnp.asarray(selmap, BF16), vc.shape)], axis=-1)
    kern = functools.partial(_nsa_attn_kernel, tq=tq, tk=tk, nd=nd, n_slc=n_slc, n_sel=min(N_SLC_MAX, n_slc))
    G, R = N_KV_GROUPS, HEADS_PER_GROUP
    width = max(n_cmp_pad, tk, WIN + tq)
    res = lambda n, d=HEAD_DIM: pl.BlockSpec((1, G, n, d), lambda b, i: (b, 0, 0, 0), pipeline_mode=pl.Buffered(1))
    return pl.pallas_call(
        kern,
        grid=(B, T // tq),
        in_specs=[
            pl.BlockSpec((1, N_HEADS, tq, HEAD_DIM), lambda b, i: (b, 0, i, 0)),
            pl.BlockSpec((1, tq, 3 * N_HEADS), lambda b, i: (b, i, 0)),
            res(n_cmp_pad), res(n_cmp_pad, 3 * HEAD_DIM), res(T), res(T), res(T), res(T),
            _const_spec(toe.shape),
        ],
        out_specs=pl.BlockSpec((1, tq, N_HEADS * HEAD_DIM), lambda b, i: (b, i, 0)),
        out_shape=jax.ShapeDtypeStruct((B, T, N_HEADS * HEAD_DIM), BF16),
        scratch_shapes=[
            pltpu.VMEM((G, tq, LANES), F32),
            pltpu.VMEM((G, tq, LANES), F32),
            pltpu.VMEM((R * tq, width), F32),
            pltpu.VMEM((R * tq, width), F32),
            pltpu.VMEM((R * tq, width), BF16),
            pltpu.VMEM((R * tq, width), BF16),
            pltpu.VMEM((R * tq, WIN + tq), F32),
            pltpu.VMEM((R * tq, WIN + tq), BF16),
            pltpu.VMEM((N_HEADS * tq, HEAD_DIM + LANES), BF16),
            pltpu.VMEM((N_HEADS, tq, LANES), F32),
            pltpu.VMEM((N_HEADS * tq, 2 * HEAD_DIM), F32),
            pltpu.VMEM((N_HEADS * tq, HEAD_DIM), F32),
        ],
        compiler_params=_params(("parallel", "arbitrary")),
        name="nsa_attn",
    )(q, gates, kc, vc_all, ks, vs, kw, vw, toe)


def _dsa_attention(x, toe, g_attn, w_in, gq, gkv, gk, w_uq, w_q_idx, w_uk):
    qa, qi, ckv, kidx, wi = _dsa_proj(x, g_attn, w_in, gq, gkv, gk, w_uq, w_uk, w_q_idx)
    return _dsa_attn(qi, wi, kidx, qa, ckv, toe)


def _shared_kv(x, g_kv, w_kv, pos_k, pos_v, w1_k, w2_k, w1_v, w2_v):
    kc_raw, vc_raw, ks, vs, kw, vw = _kv_proj(x, g_kv, w_kv)
    return _compress(kc_raw, pos_k, w1_k, w2_k), _compress(vc_raw, pos_v, w1_v, w2_v), ks, vs, kw, vw


def _nsa_attention(x, toe, kv_shared, g_attn, w_in):
    q, gates = _nsa_proj(x, g_attn, w_in)
    return _nsa_attn(q, gates, *kv_shared, toe)


def kernel(x, g_attn, g_mlp, w_up, w_down, rel_bias, a_w_in, a_g_q_lat, a_g_kv_lat, a_g_k_idx, a_w_uq, a_w_q_idx, a_w_uk, a_w_uv, a_w_o, g_kv_shared, w_kv_shared, cmp_pos_k, cmp_pos_v, cmp_w1_k, cmp_w2_k, cmp_w1_v, cmp_w2_v, b_w_in, b_w_o, g_final):
    depth = g_attn.shape[0]
    n_a = a_w_in.shape[0]
    toe = _bias_tiles(rel_bias, x.shape[1])
    kv_shared = None
    for l in range(depth):
        if l < n_a:
            o = _dsa_attention(x, toe, g_attn[l], a_w_in[l], a_g_q_lat[l], a_g_kv_lat[l], a_g_k_idx[l],
                               a_w_uq[l], a_w_q_idx[l], a_w_uk[l])
            w_o, w_uv = a_w_o[l], a_w_uv[l]
        else:
            o = _nsa_attention(x, toe, kv_shared, g_attn[l], b_w_in[l - n_a])
            w_o, w_uv = b_w_o[l - n_a], None
        x = _out_mlp(o, x, w_o, g_mlp[l], w_up[l], w_down[l], w_uv, g_final if l == depth - 1 else None)
        if l == n_a - 1:
            kv_shared = _shared_kv(x, g_kv_shared, w_kv_shared, cmp_pos_k, cmp_pos_v,
                                   cmp_w1_k, cmp_w2_k, cmp_w1_v, cmp_w2_v)
    return x
```

```python
import functools
import math

import numpy as np
import jax
import jax.numpy as jnp
from jax import lax
from jax.experimental import pallas as pl
from jax.experimental.pallas import tpu as pltpu

N_HEADS = 8
HEAD_DIM = 128
Q_LORA = 256
KV_LORA = 128
IDX_HEADS = 8
IDX_DIM = 64
IDX_TOPK_MAX = 256
N_KV_GROUPS = 2
HEADS_PER_GROUP = N_HEADS // N_KV_GROUPS
CMP_LEN = 32
CMP_STRIDE = 16
CMP_HID = 256
SLC_BLK = 64
N_SLC_MAX = 16
WIN = 512
REL_BUCKETS = 32
REL_MAX_DIST = 4096
EPS = 1e-6
NEG = -1e30
LOG2E = math.log2(math.e)

LANES = 128
ROWS = 16
KEY_ROWS = 64
INT_MIN = np.int32(-2 ** 31)
INT_MAX = np.int32(2 ** 31 - 1)
FIXED_PROBES = 15
STRAGGLER_PROBES = 40
VMEM_LIMIT = 56 * 1024 * 1024

F32 = jnp.float32
BF16 = jnp.bfloat16
NT_DIMS = (((1,), (1,)), ((), ()))


def _dot(a, b):
    return jnp.dot(a, b, preferred_element_type=F32)


def _dot_nt(a, b):
    return lax.dot_general(a, b, NT_DIMS, preferred_element_type=F32)


def _rms(x, g):
    return x * lax.rsqrt(jnp.mean(x * x, axis=-1, keepdims=True) + EPS) * g


def _to_key(v):
    bits = lax.bitcast_convert_type(v, jnp.int32)
    return bits ^ ((bits >> 31) & jnp.int32(0x7FFFFFFF))


def _const_spec(shape):
    nd = len(shape)
    return pl.BlockSpec(shape, lambda *_: (0,) * nd, pipeline_mode=pl.Buffered(1))


def _params(sem):
    return pltpu.CompilerParams(dimension_semantics=sem, vmem_limit_bytes=VMEM_LIMIT)


def _rel_bucket(dist):
    dist = jnp.maximum(dist, 0)
    exact = REL_BUCKETS // 2
    log_ratio = jnp.log(jnp.maximum(dist, 1).astype(F32) / exact) / math.log(REL_MAX_DIST / exact)
    large = exact + (log_ratio * (REL_BUCKETS - exact)).astype(jnp.int32)
    return jnp.where(dist < exact, dist, jnp.minimum(large, REL_BUCKETS - 1))


def _num_bias_tiles(T):
    exact = REL_BUCKETS // 2
    switch = exact * (REL_MAX_DIST / exact) ** ((REL_BUCKETS - exact - 1) / (REL_BUCKETS - exact))
    far = int(math.ceil(switch)) + 32
    return min(T // LANES, -(-(far + LANES - 1) // LANES) + 1)


def _bias_tiles(rel_bias, T):
    nd = _num_bias_tiles(T)
    assert nd > WIN // LANES
    c = LANES * (nd - 1)
    bd = rel_bias[_rel_bucket(jnp.arange(c + LANES, dtype=jnp.int32))].T.astype(F32) * LOG2E
    p = c + 2 * LANES
    w = jnp.concatenate([bd[:, c::-1], jnp.broadcast_to(bd[:, :1], (N_HEADS, LANES)), bd[:, :c:-1]], axis=1)
    band = jnp.tile(w, (1, LANES))[:, :LANES * (p - 1)].reshape(N_HEADS, LANES, p - 1)[:, :, :c + LANES]
    tiles = band.reshape(N_HEADS, LANES, nd, LANES)[:, :, ::-1]
    tiles = jnp.transpose(tiles, (2, 0, 1, 3))
    i = jnp.arange(LANES)[:, None]
    j = jnp.arange(LANES)[None, :]
    ahead = jnp.full((1,) + tiles.shape[1:], NEG, F32)
    diag = jnp.where(i >= j, tiles[:1], NEG)
    win_edge = jnp.where(i < j, tiles[WIN // LANES:WIN // LANES + 1], NEG)
    return jnp.concatenate([ahead, diag, tiles[1:], win_edge], axis=0)


def _tile_ids(d0, n_sub, nd):
    return [jnp.clip(d0 - k, -1, nd - 1) + 1 for k in range(n_sub)]


def _dsa_proj_kernel(x_ref, g_ref, w_in_ref, w_wt_ref, gq_ref, gkv_ref, gk_ref, w_uq_ref, w_ukt_ref, w_qi_ref,
                     qa_ref, qi_ref, ckv_ref, kidx_ref, wi_ref):
    h = _rms(x_ref[0], g_ref[...]).astype(BF16)
    proj = _dot(h, w_in_ref[...])
    c_q = _rms(proj[:, :Q_LORA], gq_ref[...]).astype(BF16)
    ckv_ref[0] = _rms(proj[:, Q_LORA:Q_LORA + KV_LORA], gkv_ref[...]).astype(BF16)
    o_k = Q_LORA + KV_LORA
    kidx_ref[0] = _rms(proj[:, o_k:o_k + IDX_DIM], gk_ref[...]).astype(BF16)
    wi_ref[0] = _dot_nt(w_wt_ref[...], h) * (IDX_HEADS ** -0.5 * IDX_DIM ** -0.5)
    q = _dot(c_q, w_uq_ref[...])
    for hh in range(N_HEADS):
        qh = q[:, hh * HEAD_DIM:(hh + 1) * HEAD_DIM].astype(BF16)
        qa_ref[0, hh] = (_dot(qh, w_ukt_ref[hh]) * (HEAD_DIM ** -0.5 * LOG2E)).astype(BF16)
        qi_ref[0, hh] = _dot(c_q, w_qi_ref[hh]).astype(BF16)


def _dsa_proj(x, g, w_in, gq, gkv, gk, w_uq, w_uk, w_q_idx, tm=256):
    B, T, D = x.shape
    n_in = Q_LORA + KV_LORA + LANES
    o_k = Q_LORA + KV_LORA
    w_in_p = jnp.zeros((D, n_in), F32).at[:, :o_k + IDX_DIM].set(w_in[:, :o_k + IDX_DIM])
    w_wt = w_in[:, o_k + IDX_DIM:].T.astype(BF16)
    w_ukt = jnp.transpose(w_uk, (1, 2, 0)).astype(BF16)
    w_qi = jnp.transpose(w_q_idx, (1, 0, 2)).astype(BF16)
    tile = lambda b, i: (b, i, 0)
    htile = lambda b, i: (b, 0, i, 0)
    return pl.pallas_call(
        _dsa_proj_kernel,
        grid=(B, T // tm),
        in_specs=[
            pl.BlockSpec((1, tm, D), tile),
            _const_spec((1, D)),
            _const_spec((D, n_in)),
            _const_spec((IDX_HEADS, D)),
            _const_spec((1, Q_LORA)), _const_spec((1, KV_LORA)), _const_spec((1, IDX_DIM)),
            _const_spec((Q_LORA, N_HEADS * HEAD_DIM)),
            _const_spec((N_HEADS, HEAD_DIM, KV_LORA)),
            _const_spec((IDX_HEADS, Q_LORA, IDX_DIM)),
        ],
        out_specs=[
            pl.BlockSpec((1, N_HEADS, tm, KV_LORA), htile),
            pl.BlockSpec((1, IDX_HEADS, tm, IDX_DIM), htile),
            pl.BlockSpec((1, tm, KV_LORA), tile),
            pl.BlockSpec((1, tm, IDX_DIM), tile),
            pl.BlockSpec((1, IDX_HEADS, tm), lambda b, i: (b, 0, i)),
        ],
        out_shape=[
            jax.ShapeDtypeStruct((B, N_HEADS, T, KV_LORA), BF16),
            jax.ShapeDtypeStruct((B, IDX_HEADS, T, IDX_DIM), BF16),
            jax.ShapeDtypeStruct((B, T, KV_LORA), BF16),
            jax.ShapeDtypeStruct((B, T, IDX_DIM), BF16),
            jax.ShapeDtypeStruct((B, IDX_HEADS, T), F32),
        ],
        compiler_params=_params(("parallel", "parallel")),
        name="dsa_proj",
    )(x, g.reshape(1, D), w_in_p.astype(BF16), w_wt, gq.reshape(1, -1), gkv.reshape(1, -1), gk.reshape(1, -1),
      w_uq.reshape(Q_LORA, N_HEADS * HEAD_DIM).astype(BF16), w_ukt, w_qi)


def _tile_lanes(a, n):
    return a if n == 1 else jnp.concatenate([a] * n, axis=1)


def _bias_rows(toe_ref, head, tiles, rows):
    parts = [toe_ref[d, head, rows, :] for d in tiles]
    return parts[0] if len(parts) == 1 else jnp.concatenate(parts, axis=1)


def _flash_head(lg_scr, lrow0, toe_ref, tiles, head, m_scr, acc_scr, p_scr, prow0, tq):
    n_rep = len(tiles)
    tk = n_rep * LANES
    for r in range(tq // ROWS):
        rows = slice(r * ROWS, (r + 1) * ROWS)
        lg = lg_scr[lrow0 + r * ROWS:lrow0 + (r + 1) * ROWS, 0:tk] + _bias_rows(toe_ref, head, tiles, rows)
        m_old = m_scr[head, rows]
        m_new = jnp.maximum(m_old, jnp.max(lg, axis=1, keepdims=True))
        m_scr[head, rows] = m_new
        p_scr[prow0 + r * ROWS:prow0 + (r + 1) * ROWS, 0:tk] = jnp.exp2(lg - _tile_lanes(m_new, n_rep)).astype(BF16)
        arows = slice(head * tq + r * ROWS, head * tq + (r + 1) * ROWS)
        acc_scr[arows, :] = acc_scr[arows, :] * _tile_lanes(jnp.exp2(m_old - m_new), 2)


def _with_ones(v):
    return jnp.concatenate([v, jnp.ones(v.shape, v.dtype)], axis=1)


def _flash_groups(nch, qx_scr, k_aug, v_aug, tile_ids, toe_ref, lg_scrs, p_scrs, m_scr, acc_scr, tq):
    R = HEADS_PER_GROUP
    grows = [slice(g * R * tq, (g + 1) * R * tq) for g in range(N_KV_GROUPS)]

    def logits(c, g):
        k = k_aug(c, g)
        lg_scrs[g][:, 0:k.shape[0]] = _dot_nt(qx_scr[grows[g], :], k)

    def softmax(c, g):
        tiles = tile_ids(c)
        for rr in range(R):
            _flash_head(lg_scrs[g], rr * tq, toe_ref, tiles, g * R + rr, m_scr, acc_scr, p_scrs[g], rr * tq, tq)

    def values(c, g):
        v = v_aug(c, g)
        acc_scr[grows[g], :] = acc_scr[grows[g], :] + _dot(p_scrs[g][:, 0:v.shape[0]], v)

    m_scr[...] = jnp.full(m_scr.shape, NEG, F32)
    acc_scr[...] = jnp.zeros(acc_scr.shape, F32)
    p_scrs[1][...] = jnp.zeros(p_scrs[1].shape, BF16)
    logits(0, 0)

    def chunk(c, carry):
        logits(c, 1)
        softmax(c, 0)
        values(jnp.maximum(c - 1, 0), 1)
        logits(jnp.minimum(c + 1, nch - 1), 0)
        softmax(c, 1)
        values(c, 0)
        return carry

    lax.fori_loop(0, nch, chunk, 0)
    values(nch - 1, 1)


def _fold_rows(a, n, op=jnp.add):
    parts = [a[i:i + n] for i in range(0, a.shape[0], n)]
    while len(parts) > 1:
        parts = [op(parts[i], parts[i + 1]) for i in range(0, len(parts), 2)]
    return parts[0]


def _from_key(k):
    return lax.bitcast_convert_type(k ^ ((k >> 31) & jnp.int32(0x7FFFFFFF)), F32)


def _dsa_attn_kernel(qi_ref, wi_ref, kidx_ref, qa_ref, ckv_ref, toe_ref, o_ref,
                     s_scr, rel0_scr, rel1_scr, lg0_scr, lg1_scr, p0_scr, p1_scr, qx_scr, m_scr, acc_scr,
                     *, tq, tk, top_k, nd, idx_bits):
    t0 = pl.program_id(1) * tq
    nch = (t0 + tq + tk - 1) // tk
    n_sub = tk // LANES
    sub = 8
    key_row = lax.broadcasted_iota(jnp.int32, (tk, tq), 0)

    wi = wi_ref[0]
    qi = qi_ref[0].reshape(IDX_HEADS * tq, IDX_DIM)
    k_slab = lax.broadcasted_iota(jnp.int32, (KEY_ROWS, tq), 0)
    q_slab = lax.broadcasted_iota(jnp.int32, (KEY_ROWS, tq), 1)

    def head_dots(c, rel_scr):
        s0 = pl.multiple_of(jnp.minimum(c, nch - 1) * tk, tk)
        rel_scr[...] = _dot_nt(kidx_ref[0, pl.ds(s0, tk), :], qi)

    def scores(c, rel_scr, top):
        s0 = c * tk
        for r in range(tk // KEY_ROWS):
            rows = slice(r * KEY_ROWS, (r + 1) * KEY_ROWS)
            acc = jnp.zeros((KEY_ROWS, tq), F32)
            for hh in range(IDX_HEADS):
                acc = acc + wi[hh:hh + 1, :] * jnp.maximum(rel_scr[rows, hh * tq:(hh + 1) * tq], 0.0)
            acc = acc + 0.0
            valid = k_slab + (s0 + r * KEY_ROWS) <= q_slab + t0
            s_scr[c, rows, :] = jnp.where(valid, _to_key(acc), INT_MIN)
            top = jnp.maximum(top, _fold_rows(jnp.where(valid, jnp.abs(acc), 0.0), sub, jnp.maximum))
        return top

    def score_pair(j, top):
        head_dots(2 * j + 1, rel1_scr)
        top = scores(2 * j, rel0_scr, top)
        head_dots(2 * j + 2, rel0_scr)
        return scores(2 * j + 1, rel1_scr, top)

    head_dots(0, rel0_scr)
    top = lax.fori_loop(0, (nch + 1) // 2, score_pair, jnp.zeros((sub, tq), F32))
    top = jnp.max(top, axis=0, keepdims=True)

    def count(pred):
        def body(j, cnt):
            for c in (2 * j, 2 * j + 1):
                cnt = cnt + _fold_rows(jnp.where(pred(s_scr[c], key_row + c * tk), 1.0, 0.0), sub)
            return cnt
        cnt = lax.fori_loop(0, (nch + 1) // 2, body, jnp.zeros((sub, tq), F32))
        return jnp.sum(cnt, axis=0, keepdims=True)

    kf = jnp.float32(top_k)

    def settled(st):
        lo, hi, n_lo, _ = st
        return (n_lo <= kf) | (hi - 1 <= lo)

    def score_mid(st):
        lo_v, hi_v = _from_key(st[0]), _from_key(st[1])
        return _to_key(lo_v + (hi_v - lo_v) * 0.5 + 0.0)

    def probe(st, want):
        lo, hi, n_lo, n_hi = st
        done = settled(st)
        cand = jnp.where((want > lo) & (want < hi), want, (lo >> 1) + (hi >> 1) + (lo & hi & 1))
        cand = jnp.where(done, lo, cand)
        cnt = count(lambda blk, idx: blk >= cand)
        up = (cnt >= kf) & ~done
        down = ~(up | done)
        return (jnp.where(up, cand, lo), jnp.where(down, cand, hi),
                jnp.where(up, cnt, n_lo), jnp.where(down, cnt, n_hi))

    n0 = (t0 + 1 + lax.broadcasted_iota(jnp.int32, (1, tq), 1)).astype(F32)
    st = (_to_key(-top), _to_key(top) + 1, n0, jnp.zeros((1, tq), F32))
    st = lax.fori_loop(0, FIXED_PROBES, lambda i, s: probe(s, score_mid(s)), st)

    lo, hi, n_lo, n_hi = st

    def inside_range(c, carry):
        blk = s_scr[c]
        inside = (blk >= lo) & (blk < hi)
        return (jnp.maximum(carry[0], _fold_rows(jnp.where(inside, blk, INT_MIN), sub, jnp.maximum)),
                jnp.minimum(carry[1], _fold_rows(jnp.where(inside, blk, INT_MAX), sub, jnp.minimum)))

    big, small = lax.fori_loop(0, nch, inside_range, (jnp.full((sub, tq), INT_MIN, jnp.int32),
                                                      jnp.full((sub, tq), INT_MAX, jnp.int32)))
    big = _fold_rows(big, 1, jnp.maximum)
    small = _fold_rows(small, 1, jnp.minimum)
    flat = (big == small) & ~settled(st)
    st = (jnp.where(flat, big, lo), jnp.where(flat, big + 1, hi), n_lo, n_hi)
    want = jnp.where(kf - n_hi == 1.0, big, jnp.where(kf - n_hi == n_lo - n_hi - 1.0, small + 1, score_mid(st)))
    st = probe(st, want)

    def unsettled(st):
        return jnp.sum(jnp.where(settled(st), 0.0, 1.0))

    def more_cond(c):
        return (c[0] < STRAGGLER_PROBES) & (c[1] > 0.0)

    def more_probe(c):
        it, _, st = c
        st = probe(st, jnp.where(it < 4, score_mid(st), st[0]))
        return it + 1, unsettled(st), st

    _, _, st = lax.while_loop(more_cond, more_probe, (jnp.int32(0), unsettled(st), st))
    thr, n_ge = st[0], st[2]

    @pl.when(jnp.max(n_ge) > kf)
    def _():
        need = kf - count(lambda blk, idx: blk > thr)

        def idx_bit(bi, last):
            cand = last | lax.shift_left(jnp.int32(1), idx_bits - 1 - bi)
            cnt = count(lambda blk, idx: (blk == thr) & (idx < cand))
            return jnp.where(cnt < need, cand, last)

        last = lax.fori_loop(0, idx_bits, idx_bit, jnp.zeros((1, tq), jnp.int32))

        def drop(c, carry):
            blk = s_scr[c]
            s_scr[c] = jnp.where((blk == thr) & (key_row + c * tk > last), INT_MIN, blk)
            return carry

        lax.fori_loop(0, nch, drop, 0)

    eye = (lax.broadcasted_iota(jnp.int32, (tq, tq), 0) == lax.broadcasted_iota(jnp.int32, (tq, tq), 1))
    qx_scr[:, 0:KV_LORA] = qa_ref[0].reshape(N_HEADS * tq, KV_LORA)
    for hh in range(N_HEADS):
        qx_scr[hh * tq:(hh + 1) * tq, KV_LORA:KV_LORA + tq] = jnp.where(eye, 1.0, 0.0).astype(BF16)

    def latents(c):
        return ckv_ref[0, pl.ds(pl.multiple_of(c * tk, tk), tk), :]

    def k_aug(c, g):
        mask_t = jnp.where(s_scr[c] >= thr, 0.0, NEG).astype(BF16)
        return jnp.concatenate([latents(c), mask_t], axis=1)

    _flash_groups(nch, qx_scr, k_aug, lambda c, g: _with_ones(latents(c)),
                  lambda c: _tile_ids((t0 - c * tk) // LANES, n_sub, nd),
                  toe_ref, (lg0_scr, lg1_scr), (p0_scr, p1_scr), m_scr, acc_scr, tq)
    for hh in range(N_HEADS):
        rows = slice(hh * tq, (hh + 1) * tq)
        o = acc_scr[rows, 0:KV_LORA] / acc_scr[rows, KV_LORA:2 * KV_LORA]
        o_ref[0, :, hh * KV_LORA:(hh + 1) * KV_LORA] = o.astype(o_ref.dtype)


def _dsa_attn(qi, wi, kidx, qa, ckv, toe, tq=128, tk=512):
    B, _, T, _ = qa.shape
    tk = min(tk, T)
    assert tq == LANES
    nd = toe.shape[0] - 2
    top_k = min(IDX_TOPK_MAX, T // 4)
    kern = functools.partial(_dsa_attn_kernel, tq=tq, tk=tk, top_k=top_k, nd=nd,
                             idx_bits=max(1, (T - 1).bit_length()))
    return pl.pallas_call(
        kern,
        grid=(B, T // tq),
        in_specs=[
            pl.BlockSpec((1, IDX_HEADS, tq, IDX_DIM), lambda b, i: (b, 0, i, 0)),
            pl.BlockSpec((1, IDX_HEADS, tq), lambda b, i: (b, 0, i)),
            pl.BlockSpec((1, T, IDX_DIM), lambda b, i: (b, 0, 0), pipeline_mode=pl.Buffered(1)),
            pl.BlockSpec((1, N_HEADS, tq, KV_LORA), lambda b, i: (b, 0, i, 0)),
            pl.BlockSpec((1, T, KV_LORA), lambda b, i: (b, 0, 0), pipeline_mode=pl.Buffered(1)),
            _const_spec(toe.shape),
        ],
        out_specs=pl.BlockSpec((1, tq, N_HEADS * KV_LORA), lambda b, i: (b, i, 0)),
        out_shape=jax.ShapeDtypeStruct((B, T, N_HEADS * KV_LORA), BF16),
        scratch_shapes=[
            pltpu.VMEM((T // tk + 1, tk, tq), jnp.int32),
            pltpu.VMEM((tk, IDX_HEADS * tq), F32),
            pltpu.VMEM((tk, IDX_HEADS * tq), F32),
            pltpu.VMEM((HEADS_PER_GROUP * tq, tk), F32),
            pltpu.VMEM((HEADS_PER_GROUP * tq, tk), F32),
            pltpu.VMEM((HEADS_PER_GROUP * tq, tk), BF16),
            pltpu.VMEM((HEADS_PER_GROUP * tq, tk), BF16),
            pltpu.VMEM((N_HEADS * tq, KV_LORA + tq), BF16),
            pltpu.VMEM((N_HEADS, tq, LANES), F32),
            pltpu.VMEM((N_HEADS * tq, 2 * KV_LORA), F32),
        ],
        compiler_params=_params(("parallel", "arbitrary")),
        name="dsa_attn",
    )(qi, wi, kidx, qa, ckv, toe)


def _out_mlp_kernel(o_ref, x_ref, w_o_ref, g_ref, w_up_ref, w_down_ref, *rest, tf, latent, final):
    y_ref = rest[-1]
    o = o_ref[0]
    if latent:
        o = jnp.concatenate([_dot(o[:, hh * KV_LORA:(hh + 1) * KV_LORA], rest[0][hh]).astype(BF16)
                             for hh in range(N_HEADS)], axis=1)
    x = x_ref[0] + _dot(o, w_o_ref[...])
    h = _rms(x, g_ref[...]).astype(BF16)
    acc = x
    for f0 in range(0, w_up_ref.shape[1], tf):
        u = jnp.maximum(_dot(h, w_up_ref[:, f0:f0 + tf]), 0.0)
        acc = acc + _dot((u * u).astype(BF16), w_down_ref[f0:f0 + tf, :])
    y_ref[0] = _rms(acc, rest[-2][...]) if final else acc


def _out_mlp(o, x, w_o, g, w_up, w_down, w_uv=None, g_final=None, tm=512, tf=512):
    B, T, D = x.shape
    F = w_up.shape[1]
    tile = lambda b, i: (b, i, 0)
    latent, final = w_uv is not None, g_final is not None
    in_specs = [pl.BlockSpec((1, tm, o.shape[-1]), tile), pl.BlockSpec((1, tm, D), tile), _const_spec(w_o.shape),
                _const_spec((1, D)), _const_spec((D, F)), _const_spec((F, D))]
    args = [o, x, w_o.astype(BF16), g.reshape(1, D), w_up.astype(BF16), w_down.astype(BF16)]
    if latent:
        in_specs.append(_const_spec((N_HEADS, KV_LORA, HEAD_DIM)))
        args.append(jnp.transpose(w_uv, (1, 0, 2)).astype(BF16))
    if final:
        in_specs.append(_const_spec((1, D)))
        args.append(g_final.reshape(1, D))
    return pl.pallas_call(
        functools.partial(_out_mlp_kernel, tf=tf, latent=latent, final=final),
        grid=(B, T // tm),
        in_specs=in_specs,
        out_specs=pl.BlockSpec((1, tm, D), tile),
        out_shape=jax.ShapeDtypeStruct((B, T, D), F32),
        compiler_params=_params(("parallel", "parallel")),
        name="out_mlp_final" if final else "out_mlp",
    )(*args)


def _kv_proj_kernel(x_ref, g_ref, w_ref, kc_ref, vc_ref, ks_ref, vs_ref, kw_ref, vw_ref):
    h = _rms(x_ref[0], g_ref[...]).astype(BF16)
    kv = _dot(h, w_ref[...])
    for part, ref in enumerate((kc_ref, vc_ref, ks_ref, vs_ref, kw_ref, vw_ref)):
        for g in range(N_KV_GROUPS):
            o = (part * N_KV_GROUPS + g) * HEAD_DIM
            ref[0, g] = kv[:, o:o + HEAD_DIM].astype(ref.dtype)


def _kv_proj(x, g, w_kv, tm=512):
    B, T, D = x.shape
    gtile = lambda b, i: (b, 0, i, 0)
    spec = pl.BlockSpec((1, N_KV_GROUPS, tm, HEAD_DIM), gtile)
    shp = lambda dt: jax.ShapeDtypeStruct((B, N_KV_GROUPS, T, HEAD_DIM), dt)
    return pl.pallas_call(
        _kv_proj_kernel,
        grid=(B, T // tm),
        in_specs=[pl.BlockSpec((1, tm, D), lambda b, i: (b, i, 0)), _const_spec((1, D)),
                  _const_spec(w_kv.shape)],
        out_specs=[spec] * 6,
        out_shape=[shp(F32), shp(F32), shp(BF16), shp(BF16), shp(BF16), shp(BF16)],
        compiler_params=_params(("parallel", "parallel")),
        name="kv_proj",
    )(x, g.reshape(1, D), w_kv.astype(BF16))


def _compress_kernel(raw_ref, pos_ref, w1_ref, w2_ref, o_ref):
    rows = raw_ref[0, 0]
    half = rows.shape[1]
    a = _dot((rows + pos_ref[:, :half]).astype(BF16), w1_ref[:half, :])
    b = _dot((rows + pos_ref[:, half:]).astype(BF16), w1_ref[half:, :])
    pre = a + pltpu.roll(b, rows.shape[0] - 1, 0)
    act = 0.5 * pre * (1.0 + jnp.tanh(math.sqrt(2.0 / math.pi) * (pre + 0.044715 * (pre * pre * pre))))
    o_ref[0, 0] = _dot(act.astype(BF16), w2_ref[...]).astype(o_ref.dtype)


def _compress(raw, pos, w1, w2):
    B, G, T, Dh = raw.shape
    nr = T // CMP_STRIDE
    rows = raw.reshape(B, G, nr, CMP_STRIDE * Dh)
    return pl.pallas_call(
        _compress_kernel,
        grid=(B, G),
        in_specs=[pl.BlockSpec((1, 1, nr, CMP_STRIDE * Dh), lambda b, g: (b, g, 0, 0)),
                  _const_spec((1, CMP_LEN * Dh)), _const_spec(w1.shape), _const_spec(w2.shape)],
        out_specs=pl.BlockSpec((1, 1, nr, Dh), lambda b, g: (b, g, 0, 0)),
        out_shape=jax.ShapeDtypeStruct((B, G, nr, Dh), BF16),
        compiler_params=_params(("parallel", "parallel")),
        name="compress",
    )(rows, pos.reshape(1, CMP_LEN * Dh), w1.astype(BF16), w2.astype(BF16))


def _nsa_proj_kernel(x_ref, g_ref, wq_ref, wg_ref, q_ref, gate_ref):
    h = _rms(x_ref[0], g_ref[...]).astype(BF16)
    q = _dot(h, wq_ref[...]) * (HEAD_DIM ** -0.5 * LOG2E)
    for hh in range(N_HEADS):
        q_ref[0, hh] = q[:, hh * HEAD_DIM:(hh + 1) * HEAD_DIM].astype(BF16)
    gate_ref[0] = jax.nn.sigmoid(_dot(h, wg_ref[...]))[:, :3 * N_HEADS]


def _nsa_proj(x, g, w_in, tm=512):
    B, T, D = x.shape
    hd = N_HEADS * HEAD_DIM
    wq = w_in[:, :hd].astype(BF16)
    wg = jnp.zeros((D, LANES), F32).at[:, :3 * N_HEADS].set(w_in[:, hd:]).astype(BF16)
    return pl.pallas_call(
        _nsa_proj_kernel,
        grid=(B, T // tm),
        in_specs=[pl.BlockSpec((1, tm, D), lambda b, i: (b, i, 0)), _const_spec((1, D)),
                  _const_spec((D, hd)), _const_spec((D, LANES))],
        out_specs=[pl.BlockSpec((1, N_HEADS, tm, HEAD_DIM), lambda b, i: (b, 0, i, 0)),
                   pl.BlockSpec((1, tm, 3 * N_HEADS), lambda b, i: (b, i, 0))],
        out_shape=[jax.ShapeDtypeStruct((B, N_HEADS, T, HEAD_DIM), BF16),
                   jax.ShapeDtypeStruct((B, T, 3 * N_HEADS), F32)],
        compiler_params=_params(("parallel", "parallel")),
        name="nsa_proj",
    )(x, g.reshape(1, D), wq, wg)


def _top_blocks(imp, blk, n_sel, exact_ties):
    sel = jnp.zeros(imp.shape, F32)
    blk_f = blk.astype(F32)
    for _ in range(n_sel):
        if exact_ties:
            best = jnp.max(imp, axis=1, keepdims=True)
            hit = blk_f == jnp.min(jnp.where(imp == best, blk_f, float(LANES)), axis=1, keepdims=True)
        else:
            hit = blk == jnp.argmax(imp, axis=1, keepdims=True).astype(jnp.int32)
        sel = jnp.where(hit, 1.0, sel)
        imp = jnp.where(hit, -jnp.inf, imp)
    return sel


def _nsa_attn_kernel(q_ref, gate_ref, kc_ref, vc_ref, ks_ref, vs_ref, kw_ref, vw_ref, toe_ref,
                     o_ref, sel_scr, imp_scr, lg0_scr, lg1_scr, p0_scr, p1_scr, lw_scr, pw_scr, qx_scr, m_scr, acc_scr,
                     ocw_scr, *, tq, tk, nd, n_slc, n_sel):
    G, R = N_KV_GROUPS, HEADS_PER_GROUP
    lg_scrs, p_scrs = (lg0_scr, lg1_scr), (p0_scr, p1_scr)
    t0 = pl.program_id(1) * tq
    n_cmp = kc_ref.shape[2]
    n_sub = tk // LANES
    wlen = WIN + tq
    gates = gate_ref[0]
    t_col = t0 + lax.broadcasted_iota(jnp.int32, (tq, 1), 0)
    t_rows = t0 + lax.broadcasted_iota(jnp.int32, (ROWS, 1), 0)
    slabs = [slice(r * ROWS, (r + 1) * ROWS) for r in range(tq // ROWS)]

    cmp_end = CMP_STRIDE * lax.broadcasted_iota(jnp.int32, (ROWS, n_cmp), 1) + (CMP_LEN - 1)
    blk = lax.broadcasted_iota(jnp.int32, (tq, LANES), 1)
    cur = t_col // SLC_BLK
    forced = (blk == 0) | (blk == cur) | (blk == cur - 1)
    future = blk * SLC_BLK > t_col
    start = pl.multiple_of(jnp.maximum(t0 - WIN, 0), LANES)
    dw = (t0 - start) // LANES
    wtiles = _tile_ids(dw, wlen // LANES, nd)
    wtiles[0] = jnp.where(dw == WIN // LANES, nd + 1, wtiles[0])
    open_ties = []
    for g in range(G):
        lg_scr, p_scr = lg_scrs[g], p_scrs[g]
        qg = q_ref[0, g * R:(g + 1) * R].reshape(R * tq, HEAD_DIM)
        lg_scr[0:R * tq, 0:n_cmp] = _dot_nt(qg, kc_ref[0, g])
        for r, rows in enumerate(slabs):
            cmask = cmp_end <= t_rows + r * ROWS
            for rr in range(R):
                lc = jnp.where(cmask, lg_scr[rr * tq + r * ROWS:rr * tq + (r + 1) * ROWS, 0:n_cmp], NEG)
                e = jnp.where(cmask, jnp.exp2(lc - jnp.max(lc, axis=1, keepdims=True)), 0.0)
                p_scr[rr * tq + r * ROWS:rr * tq + (r + 1) * ROWS, 0:n_cmp] = e.astype(BF16)
        cmp_all = _dot(p_scr[:, 0:n_cmp], vc_ref[0, g])
        inv = 1.0 / jnp.maximum(cmp_all[:, HEAD_DIM:2 * HEAD_DIM], 1e-30)
        o_cmp = cmp_all[:, 0:HEAD_DIM] * inv
        imp_heads = cmp_all[:, 2 * HEAD_DIM:3 * HEAD_DIM] * inv

        lw_scr[0:R * tq, 0:wlen] = _dot_nt(qg, kw_ref[0, g, pl.ds(start, wlen), :])
        for rr in range(R):
            for r, rows in enumerate(slabs):
                lw = (lw_scr[rr * tq + r * ROWS:rr * tq + (r + 1) * ROWS, 0:wlen]
                      + _bias_rows(toe_ref, g * R + rr, wtiles, rows))
                pw = jnp.exp2(lw - jnp.max(lw, axis=1, keepdims=True))
                pw_scr[rr * tq + r * ROWS:rr * tq + (r + 1) * ROWS, 0:wlen] = pw.astype(BF16)
        o_win = _dot(pw_scr[:, 0:wlen], _with_ones(vw_ref[0, g, pl.ds(start, wlen), :]))
        for rr in range(R):
            hh = g * R + rr
            wrows = slice(rr * tq, (rr + 1) * tq)
            ocw_scr[hh * tq:(hh + 1) * tq, :] = (
                gates[:, 3 * hh:3 * hh + 1] * o_cmp[wrows]
                + gates[:, 3 * hh + 2:3 * hh + 3] * (o_win[wrows, 0:HEAD_DIM] / o_win[wrows, HEAD_DIM:2 * HEAD_DIM]))
        imp = _fold_rows(imp_heads, tq)
        imp = jnp.where(forced, 1e9, imp)
        imp = jnp.where(future, NEG, imp)
        imp = jnp.where(blk < n_slc, imp, -jnp.inf)
        sel = _top_blocks(imp, blk, n_sel, exact_ties=False)
        sel_scr[g] = sel
        imp_scr[g] = imp
        cut = jnp.min(jnp.where(sel > 0.0, imp, jnp.inf), axis=1, keepdims=True)
        open_ties.append(jnp.max(jnp.where((imp == cut) & (sel == 0.0) & ~future, 1.0, 0.0)))
        qx_scr[g * R * tq:(g + 1) * R * tq, 0:HEAD_DIM] = qg

    @pl.when(jnp.maximum(*open_ties) > 0.0)
    def _():
        for g in range(G):
            sel_scr[g] = _top_blocks(imp_scr[g], blk, n_sel, exact_ties=True)

    for g in range(G):
        block_bias = jnp.where((sel_scr[g] > 0.0) & ~future, 0.0, NEG).astype(BF16)
        for rr in range(R):
            qx_scr[(g * R + rr) * tq:(g * R + rr + 1) * tq, HEAD_DIM:HEAD_DIM + LANES] = block_bias

    nch = (t0 + tq + tk - 1) // tk
    lane_minus_block = (lax.broadcasted_iota(jnp.int32, (tk, LANES), 1)
                        - lax.broadcasted_iota(jnp.int32, (tk, LANES), 0) // SLC_BLK)

    def k_aug(c, g):
        s0 = pl.multiple_of(c * tk, tk)
        block_of_key = jnp.where(lane_minus_block == s0 // SLC_BLK, 1.0, 0.0).astype(BF16)
        return jnp.concatenate([ks_ref[0, g, pl.ds(s0, tk), :], block_of_key], axis=1)

    _flash_groups(nch, qx_scr, k_aug, lambda c, g: _with_ones(vs_ref[0, g, pl.ds(pl.multiple_of(c * tk, tk), tk), :]),
                  lambda c: _tile_ids((t0 - c * tk) // LANES, n_sub, nd),
                  toe_ref, lg_scrs, p_scrs, m_scr, acc_scr, tq)

    for hh in range(N_HEADS):
        rows = slice(hh * tq, (hh + 1) * tq)
        o_slc = acc_scr[rows, 0:HEAD_DIM] / acc_scr[rows, HEAD_DIM:2 * HEAD_DIM]
        o = ocw_scr[rows, :] + gates[:, 3 * hh + 1:3 * hh + 2] * o_slc
        o_ref[0, :, hh * HEAD_DIM:(hh + 1) * HEAD_DIM] = o.astype(o_ref.dtype)


def _selection_map(n_cmp, n_slc):
    c0 = CMP_STRIDE * np.arange(n_cmp)[:, None]
    s0 = SLC_BLK * np.arange(n_slc)[None, :]
    ov = np.clip(np.minimum(c0 + CMP_LEN, s0 + SLC_BLK) - np.maximum(c0, s0), 0, None)
    return (ov / CMP_LEN).astype(np.float32)


def _nsa_attn(q, gates, kc, vc, ks, vs, kw, vw, toe, tq=128, tk=512):
    B, _, T, _ = q.shape
    tk = min(tk, T)
    assert tq == LANES
    nd = toe.shape[0] - 2
    n_cmp_pad = kc.shape[2]
    n_cmp = (T - CMP_LEN) // CMP_STRIDE + 1
    n_slc = T // SLC_BLK
    assert n_slc <= LANES and T >= WIN + tq
    selmap = np.zeros((n_cmp_pad, LANES), np.float32)
    selmap[:n_cmp, :n_slc] = _selection_map(n_cmp, n_slc)
    vc_all = jnp.concatenate([vc, jnp.ones_like(vc), jnp.broadcast_to(jnp.asarray(selmap, BF16), vc.shape)], axis=-1)
    kern = functools.partial(_nsa_attn_kernel, tq=tq, tk=tk, nd=nd, n_slc=n_slc, n_sel=min(N_SLC_MAX, n_slc))
    G, R = N_KV_GROUPS, HEADS_PER_GROUP
    width = max(n_cmp_pad, tk, WIN + tq)
    res = lambda n, d=HEAD_DIM: pl.BlockSpec((1, G, n, d), lambda b, i: (b, 0, 0, 0), pipeline_mode=pl.Buffered(1))
    return pl.pallas_call(
        kern,
        grid=(B, T // tq),
        in_specs=[
            pl.BlockSpec((1, N_HEADS, tq, HEAD_DIM), lambda b, i: (b, 0, i, 0)),
            pl.BlockSpec((1, tq, 3 * N_HEADS), lambda b, i: (b, i, 0)),
            res(n_cmp_pad), res(n_cmp_pad, 3 * HEAD_DIM), res(T), res(T), res(T), res(T),
            _const_spec(toe.shape),
        ],
        out_specs=pl.BlockSpec((1, tq, N_HEADS * HEAD_DIM), lambda b, i: (b, i, 0)),
        out_shape=jax.ShapeDtypeStruct((B, T, N_HEADS * HEAD_DIM), BF16),
        scratch_shapes=[
            pltpu.VMEM((G, tq, LANES), F32),
            pltpu.VMEM((G, tq, LANES), F32),
            pltpu.VMEM((R * tq, width), F32),
            pltpu.VMEM((R * tq, width), F32),
            pltpu.VMEM((R * tq, width), BF16),
            pltpu.VMEM((R * tq, width), BF16),
            pltpu.VMEM((R * tq, WIN + tq), F32),
            pltpu.VMEM((R * tq, WIN + tq), BF16),
            pltpu.VMEM((N_HEADS * tq, HEAD_DIM + LANES), BF16),
            pltpu.VMEM((N_HEADS, tq, LANES), F32),
            pltpu.VMEM((N_HEADS * tq, 2 * HEAD_DIM), F32),
            pltpu.VMEM((N_HEADS * tq, HEAD_DIM), F32),
        ],
        compiler_params=_params(("parallel", "arbitrary")),
        name="nsa_attn",
    )(q, gates, kc, vc_all, ks, vs, kw, vw, toe)


def _dsa_attention(x, toe, g_attn, w_in, gq, gkv, gk, w_uq, w_q_idx, w_uk):
    qa, qi, ckv, kidx, wi = _dsa_proj(x, g_attn, w_in, gq, gkv, gk, w_uq, w_uk, w_q_idx)
    return _dsa_attn(qi, wi, kidx, qa, ckv, toe)


def _shared_kv(x, g_kv, w_kv, pos_k, pos_v, w1_k, w2_k, w1_v, w2_v):
    kc_raw, vc_raw, ks, vs, kw, vw = _kv_proj(x, g_kv, w_kv)
    return _compress(kc_raw, pos_k, w1_k, w2_k), _compress(vc_raw, pos_v, w1_v, w2_v), ks, vs, kw, vw


def _nsa_attention(x, toe, kv_shared, g_attn, w_in):
    q, gates = _nsa_proj(x, g_attn, w_in)
    return _nsa_attn(q, gates, *kv_shared, toe)


def kernel(x, g_attn, g_mlp, w_up, w_down, rel_bias, a_w_in, a_g_q_lat, a_g_kv_lat, a_g_k_idx, a_w_uq, a_w_q_idx, a_w_uk, a_w_uv, a_w_o, g_kv_shared, w_kv_shared, cmp_pos_k, cmp_pos_v, cmp_w1_k, cmp_w2_k, cmp_w1_v, cmp_w2_v, b_w_in, b_w_o, g_final):
    depth = g_attn.shape[0]
    n_a = a_w_in.shape[0]
    toe = _bias_tiles(rel_bias, x.shape[1])
    kv_shared = None
    for l in range(depth):
        if l < n_a:
            o = _dsa_attention(x, toe, g_attn[l], a_w_in[l], a_g_q_lat[l], a_g_kv_lat[l], a_g_k_idx[l],
                               a_w_uq[l], a_w_q_idx[l], a_w_uk[l])
            w_o, w_uv = a_w_o[l], a_w_uv[l]
        else:
            o = _nsa_attention(x, toe, kv_shared, g_attn[l], b_w_in[l - n_a])
            w_o, w_uv = b_w_o[l - n_a], None
        x = _out_mlp(o, x, w_o, g_mlp[l], w_up[l], w_down[l], w_uv, g_final if l == depth - 1 else None)
        if l == n_a - 1:
            kv_shared = _shared_kv(x, g_kv_shared, w_kv_shared, cmp_pos_k, cmp_pos_v,
                                   cmp_w1_k, cmp_w2_k, cmp_w1_v, cmp_w2_v)
    return x
```

```python
import functools
import math

import numpy as np
import jax
import jax.numpy as jnp
from jax import lax
from jax.experimental import pallas as pl
from jax.experimental.pallas import tpu as pltpu

N_HEADS = 8
HEAD_DIM = 128
Q_LORA = 256
KV_LORA = 128
IDX_HEADS = 8
IDX_DIM = 64
IDX_TOPK_MAX = 256
N_KV_GROUPS = 2
HEADS_PER_GROUP = N_HEADS // N_KV_GROUPS
CMP_LEN = 32
CMP_STRIDE = 16
CMP_HID = 256
SLC_BLK = 64
N_SLC_MAX = 16
WIN = 512
REL_BUCKETS = 32
REL_MAX_DIST = 4096
EPS = 1e-6
NEG = -1e30
LOG2E = math.log2(math.e)

LANES = 128
ROWS = 16
KEY_ROWS = 64
INT_MIN = np.int32(-2 ** 31)
INT_MAX = np.int32(2 ** 31 - 1)
FIXED_PROBES = 14
STRAGGLER_PROBES = 40
VMEM_LIMIT = 56 * 1024 * 1024

F32 = jnp.float32
BF16 = jnp.bfloat16
NT_DIMS = (((1,), (1,)), ((), ()))


def _dot(a, b):
    return jnp.dot(a, b, preferred_element_type=F32)


def _dot_nt(a, b):
    return lax.dot_general(a, b, NT_DIMS, preferred_element_type=F32)


def _rms(x, g):
    return x * lax.rsqrt(jnp.mean(x * x, axis=-1, keepdims=True) + EPS) * g


def _to_key(v):
    bits = lax.bitcast_convert_type(v, jnp.int32)
    return bits ^ ((bits >> 31) & jnp.int32(0x7FFFFFFF))


def _const_spec(shape):
    nd = len(shape)
    return pl.BlockSpec(shape, lambda *_: (0,) * nd, pipeline_mode=pl.Buffered(1))


def _params(sem):
    return pltpu.CompilerParams(dimension_semantics=sem, vmem_limit_bytes=VMEM_LIMIT)


def _rel_bucket(dist):
    dist = jnp.maximum(dist, 0)
    exact = REL_BUCKETS // 2
    log_ratio = jnp.log(jnp.maximum(dist, 1).astype(F32) / exact) / math.log(REL_MAX_DIST / exact)
    large = exact + (log_ratio * (REL_BUCKETS - exact)).astype(jnp.int32)
    return jnp.where(dist < exact, dist, jnp.minimum(large, REL_BUCKETS - 1))


def _num_bias_tiles(T):
    exact = REL_BUCKETS // 2
    switch = exact * (REL_MAX_DIST / exact) ** ((REL_BUCKETS - exact - 1) / (REL_BUCKETS - exact))
    far = int(math.ceil(switch)) + 32
    return min(T // LANES, -(-(far + LANES - 1) // LANES) + 1)


def _bias_tiles(rel_bias, T):
    nd = _num_bias_tiles(T)
    assert nd > WIN // LANES
    c = LANES * (nd - 1)
    bd = rel_bias[_rel_bucket(jnp.arange(c + LANES, dtype=jnp.int32))].T.astype(F32) * LOG2E
    p = c + 2 * LANES
    w = jnp.concatenate([bd[:, c::-1], jnp.broadcast_to(bd[:, :1], (N_HEADS, LANES)), bd[:, :c:-1]], axis=1)
    band = jnp.tile(w, (1, LANES))[:, :LANES * (p - 1)].reshape(N_HEADS, LANES, p - 1)[:, :, :c + LANES]
    tiles = band.reshape(N_HEADS, LANES, nd, LANES)[:, :, ::-1]
    tiles = jnp.transpose(tiles, (2, 0, 1, 3))
    i = jnp.arange(LANES)[:, None]
    j = jnp.arange(LANES)[None, :]
    ahead = jnp.full((1,) + tiles.shape[1:], NEG, F32)
    diag = jnp.where(i >= j, tiles[:1], NEG)
    win_edge = jnp.where(i < j, tiles[WIN // LANES:WIN // LANES + 1], NEG)
    return jnp.concatenate([ahead, diag, tiles[1:], win_edge], axis=0)


def _tile_ids(d0, n_sub, nd):
    return [jnp.clip(d0 - k, -1, nd - 1) + 1 for k in range(n_sub)]


def _dsa_proj_kernel(x_ref, g_ref, w_in_ref, w_wt_ref, gq_ref, gkv_ref, gk_ref, w_uq_ref, w_ukt_ref, w_qi_ref,
                     qa_ref, qi_ref, ckv_ref, kidx_ref, wi_ref):
    h = _rms(x_ref[0], g_ref[...]).astype(BF16)
    proj = _dot(h, w_in_ref[...])
    c_q = _rms(proj[:, :Q_LORA], gq_ref[...]).astype(BF16)
    ckv_ref[0] = _rms(proj[:, Q_LORA:Q_LORA + KV_LORA], gkv_ref[...]).astype(BF16)
    o_k = Q_LORA + KV_LORA
    kidx_ref[0] = _rms(proj[:, o_k:o_k + IDX_DIM], gk_ref[...]).astype(BF16)
    wi_ref[0] = _dot_nt(w_wt_ref[...], h) * (IDX_HEADS ** -0.5 * IDX_DIM ** -0.5)
    q = _dot(c_q, w_uq_ref[...])
    for hh in range(N_HEADS):
        qh = q[:, hh * HEAD_DIM:(hh + 1) * HEAD_DIM].astype(BF16)
        qa_ref[0, hh] = (_dot(qh, w_ukt_ref[hh]) * (HEAD_DIM ** -0.5 * LOG2E)).astype(BF16)
        qi_ref[0, hh] = _dot(c_q, w_qi_ref[hh]).astype(BF16)


def _dsa_proj(x, g, w_in, gq, gkv, gk, w_uq, w_uk, w_q_idx, tm=512):
    B, T, D = x.shape
    n_in = Q_LORA + KV_LORA + LANES
    o_k = Q_LORA + KV_LORA
    w_in_p = jnp.zeros((D, n_in), F32).at[:, :o_k + IDX_DIM].set(w_in[:, :o_k + IDX_DIM])
    w_wt = w_in[:, o_k + IDX_DIM:].T.astype(BF16)
    w_ukt = jnp.transpose(w_uk, (1, 2, 0)).astype(BF16)
    w_qi = jnp.transpose(w_q_idx, (1, 0, 2)).astype(BF16)
    tile = lambda b, i: (b, i, 0)
    htile = lambda b, i: (b, 0, i, 0)
    return pl.pallas_call(
        _dsa_proj_kernel,
        grid=(B, T // tm),
        in_specs=[
            pl.BlockSpec((1, tm, D), tile),
            _const_spec((1, D)),
            _const_spec((D, n_in)),
            _const_spec((IDX_HEADS, D)),
            _const_spec((1, Q_LORA)), _const_spec((1, KV_LORA)), _const_spec((1, IDX_DIM)),
            _const_spec((Q_LORA, N_HEADS * HEAD_DIM)),
            _const_spec((N_HEADS, HEAD_DIM, KV_LORA)),
            _const_spec((IDX_HEADS, Q_LORA, IDX_DIM)),
        ],
        out_specs=[
            pl.BlockSpec((1, N_HEADS, tm, KV_LORA), htile),
            pl.BlockSpec((1, IDX_HEADS, tm, IDX_DIM), htile),
            pl.BlockSpec((1, tm, KV_LORA), tile),
            pl.BlockSpec((1, tm, IDX_DIM), tile),
            pl.BlockSpec((1, IDX_HEADS, tm), lambda b, i: (b, 0, i)),
        ],
        out_shape=[
            jax.ShapeDtypeStruct((B, N_HEADS, T, KV_LORA), BF16),
            jax.ShapeDtypeStruct((B, IDX_HEADS, T, IDX_DIM), BF16),
            jax.ShapeDtypeStruct((B, T, KV_LORA), BF16),
            jax.ShapeDtypeStruct((B, T, IDX_DIM), BF16),
            jax.ShapeDtypeStruct((B, IDX_HEADS, T), F32),
        ],
        compiler_params=_params(("parallel", "parallel")),
        name="dsa_proj",
    )(x, g.reshape(1, D), w_in_p.astype(BF16), w_wt, gq.reshape(1, -1), gkv.reshape(1, -1), gk.reshape(1, -1),
      w_uq.reshape(Q_LORA, N_HEADS * HEAD_DIM).astype(BF16), w_ukt, w_qi)


def _tile_lanes(a, n):
    return a if n == 1 else jnp.concatenate([a] * n, axis=1)


def _bias_rows(toe_ref, head, tiles, rows):
    parts = [toe_ref[d, head, rows, :] for d in tiles]
    return parts[0] if len(parts) == 1 else jnp.concatenate(parts, axis=1)


def _flash_head(lg_scr, lrow0, toe_ref, tiles, head, m_scr, acc_scr, p_scr, prow0, tq):
    n_rep = len(tiles)
    tk = n_rep * LANES
    for r in range(tq // ROWS):
        rows = slice(r * ROWS, (r + 1) * ROWS)
        lg = lg_scr[lrow0 + r * ROWS:lrow0 + (r + 1) * ROWS, 0:tk] + _bias_rows(toe_ref, head, tiles, rows)
        m_old = m_scr[head, rows]
        m_new = jnp.maximum(m_old, jnp.max(lg, axis=1, keepdims=True))
        m_scr[head, rows] = m_new
        p_scr[prow0 + r * ROWS:prow0 + (r + 1) * ROWS, 0:tk] = jnp.exp2(lg - _tile_lanes(m_new, n_rep)).astype(BF16)
        arows = slice(head * tq + r * ROWS, head * tq + (r + 1) * ROWS)
        acc_scr[arows, :] = acc_scr[arows, :] * _tile_lanes(jnp.exp2(m_old - m_new), 2)


def _with_ones(v):
    return jnp.concatenate([v, jnp.ones(v.shape, v.dtype)], axis=1)


def _flash_groups(nch, qx_scr, k_aug, v_aug, tile_ids, toe_ref, lg_scrs, p_scrs, m_scr, acc_scr, tq):
    R = HEADS_PER_GROUP
    grows = [slice(g * R * tq, (g + 1) * R * tq) for g in range(N_KV_GROUPS)]

    def logits(c, g):
        k = k_aug(c, g)
        lg_scrs[g][:, 0:k.shape[0]] = _dot_nt(qx_scr[grows[g], :], k)

    def softmax(c, g):
        tiles = tile_ids(c)
        for rr in range(R):
            _flash_head(lg_scrs[g], rr * tq, toe_ref, tiles, g * R + rr, m_scr, acc_scr, p_scrs[g], rr * tq, tq)

    def values(c, g):
        v = v_aug(c, g)
        acc_scr[grows[g], :] = acc_scr[grows[g], :] + _dot(p_scrs[g][:, 0:v.shape[0]], v)

    m_scr[...] = jnp.full(m_scr.shape, NEG, F32)
    acc_scr[...] = jnp.zeros(acc_scr.shape, F32)
    p_scrs[1][...] = jnp.zeros(p_scrs[1].shape, BF16)
    logits(0, 0)

    def chunk(c, carry):
        logits(c, 1)
        softmax(c, 0)
        values(jnp.maximum(c - 1, 0), 1)
        logits(jnp.minimum(c + 1, nch - 1), 0)
        softmax(c, 1)
        values(c, 0)
        return carry

    lax.fori_loop(0, nch, chunk, 0)
    values(nch - 1, 1)


def _fold_rows(a, n, op=jnp.add):
    parts = [a[i:i + n] for i in range(0, a.shape[0], n)]
    while len(parts) > 1:
        parts = [op(parts[i], parts[i + 1]) for i in range(0, len(parts), 2)]
    return parts[0]


def _from_key(k):
    return lax.bitcast_convert_type(k ^ ((k >> 31) & jnp.int32(0x7FFFFFFF)), F32)


def _dsa_attn_kernel(qi_ref, wi_ref, kidx_ref, qa_ref, ckv_ref, toe_ref, o_ref,
                     s_scr, rel0_scr, rel1_scr, lg0_scr, lg1_scr, p0_scr, p1_scr, qx_scr, m_scr, acc_scr,
                     *, tq, tk, top_k, nd, idx_bits):
    t0 = pl.program_id(1) * tq
    nch = (t0 + tq + tk - 1) // tk
    n_sub = tk // LANES
    sub = 8
    key_row = lax.broadcasted_iota(jnp.int32, (tk, tq), 0)

    wi = wi_ref[0]
    qi = qi_ref[0].reshape(IDX_HEADS * tq, IDX_DIM)
    k_slab = lax.broadcasted_iota(jnp.int32, (KEY_ROWS, tq), 0)
    q_slab = lax.broadcasted_iota(jnp.int32, (KEY_ROWS, tq), 1)

    def head_dots(c, rel_scr):
        s0 = pl.multiple_of(jnp.minimum(c, nch - 1) * tk, tk)
        rel_scr[...] = _dot_nt(kidx_ref[0, pl.ds(s0, tk), :], qi)

    def scores(c, rel_scr, top):
        s0 = c * tk
        for r in range(tk // KEY_ROWS):
            rows = slice(r * KEY_ROWS, (r + 1) * KEY_ROWS)
            acc = jnp.zeros((KEY_ROWS, tq), F32)
            for hh in range(IDX_HEADS):
                acc = acc + wi[hh:hh + 1, :] * jnp.maximum(rel_scr[rows, hh * tq:(hh + 1) * tq], 0.0)
            acc = acc + 0.0
            valid = k_slab + (s0 + r * KEY_ROWS) <= q_slab + t0
            s_scr[c, rows, :] = jnp.where(valid, _to_key(acc), INT_MIN)
            top = jnp.maximum(top, _fold_rows(jnp.where(valid, jnp.abs(acc), 0.0), sub, jnp.maximum))
        return top

    def score_pair(j, top):
        head_dots(2 * j + 1, rel1_scr)
        top = scores(2 * j, rel0_scr, top)
        head_dots(2 * j + 2, rel0_scr)
        return scores(2 * j + 1, rel1_scr, top)

    head_dots(0, rel0_scr)
    top = lax.fori_loop(0, (nch + 1) // 2, score_pair, jnp.zeros((sub, tq), F32))
    top = jnp.max(top, axis=0, keepdims=True)

    def count(pred):
        def body(j, cnt):
            for c in (2 * j, 2 * j + 1):
                cnt = cnt + _fold_rows(jnp.where(pred(s_scr[c], key_row + c * tk), 1.0, 0.0), sub)
            return cnt
        cnt = lax.fori_loop(0, (nch + 1) // 2, body, jnp.zeros((sub, tq), F32))
        return jnp.sum(cnt, axis=0, keepdims=True)

    kf = jnp.float32(top_k)

    def settled(st):
        lo, hi, n_lo, _ = st
        return (n_lo <= kf) | (hi - 1 <= lo)

    def score_mid(st):
        lo_v, hi_v = _from_key(st[0]), _from_key(st[1])
        return _to_key(lo_v + (hi_v - lo_v) * 0.5 + 0.0)

    def probe(st, want):
        lo, hi, n_lo, n_hi = st
        done = settled(st)
        cand = jnp.where((want > lo) & (want < hi), want, (lo >> 1) + (hi >> 1) + (lo & hi & 1))
        cand = jnp.where(done, lo, cand)
        cnt = count(lambda blk, idx: blk >= cand)
        up = (cnt >= kf) & ~done
        down = ~(up | done)
        return (jnp.where(up, cand, lo), jnp.where(down, cand, hi),
                jnp.where(up, cnt, n_lo), jnp.where(down, cnt, n_hi))

    n0 = (t0 + 1 + lax.broadcasted_iota(jnp.int32, (1, tq), 1)).astype(F32)
    st = (_to_key(-top), _to_key(top) + 1, n0, jnp.zeros((1, tq), F32))
    st = lax.fori_loop(0, FIXED_PROBES, lambda i, s: probe(s, score_mid(s)), st)

    lo, hi, n_lo, n_hi = st

    def inside_range(c, carry):
        blk = s_scr[c]
        inside = (blk >= lo) & (blk < hi)
        return (jnp.maximum(carry[0], _fold_rows(jnp.where(inside, blk, INT_MIN), sub, jnp.maximum)),
                jnp.minimum(carry[1], _fold_rows(jnp.where(inside, blk, INT_MAX), sub, jnp.minimum)))

    big, small = lax.fori_loop(0, nch, inside_range, (jnp.full((sub, tq), INT_MIN, jnp.int32),
                                                      jnp.full((sub, tq), INT_MAX, jnp.int32)))
    big = _fold_rows(big, 1, jnp.maximum)
    small = _fold_rows(small, 1, jnp.minimum)
    flat = (big == small) & ~settled(st)
    st = (jnp.where(flat, big, lo), jnp.where(flat, big + 1, hi), n_lo, n_hi)
    want = jnp.where(kf - n_hi == 1.0, big, jnp.where(kf - n_hi == n_lo - n_hi - 1.0, small + 1, score_mid(st)))
    st = probe(st, want)

    def unsettled(st):
        return jnp.sum(jnp.where(settled(st), 0.0, 1.0))

    def more_cond(c):
        return (c[0] < STRAGGLER_PROBES) & (c[1] > 0.0)

    def more_probe(c):
        it, _, st = c
        st = probe(st, jnp.where(it < 4, score_mid(st), st[0]))
        return it + 1, unsettled(st), st

    _, _, st = lax.while_loop(more_cond, more_probe, (jnp.int32(0), unsettled(st), st))
    thr, n_ge = st[0], st[2]

    @pl.when(jnp.max(n_ge) > kf)
    def _():
        need = kf - count(lambda blk, idx: blk > thr)

        def idx_bit(bi, last):
            cand = last | lax.shift_left(jnp.int32(1), idx_bits - 1 - bi)
            cnt = count(lambda blk, idx: (blk == thr) & (idx < cand))
            return jnp.where(cnt < need, cand, last)

        last = lax.fori_loop(0, idx_bits, idx_bit, jnp.zeros((1, tq), jnp.int32))

        def drop(c, carry):
            blk = s_scr[c]
            s_scr[c] = jnp.where((blk == thr) & (key_row + c * tk > last), INT_MIN, blk)
            return carry

        lax.fori_loop(0, nch, drop, 0)

    eye = (lax.broadcasted_iota(jnp.int32, (tq, tq), 0) == lax.broadcasted_iota(jnp.int32, (tq, tq), 1))
    qx_scr[:, 0:KV_LORA] = qa_ref[0].reshape(N_HEADS * tq, KV_LORA)
    for hh in range(N_HEADS):
        qx_scr[hh * tq:(hh + 1) * tq, KV_LORA:KV_LORA + tq] = jnp.where(eye, 1.0, 0.0).astype(BF16)

    def latents(c):
        return ckv_ref[0, pl.ds(pl.multiple_of(c * tk, tk), tk), :]

    def k_aug(c, g):
        mask_t = jnp.where(s_scr[c] >= thr, 0.0, NEG).astype(BF16)
        return jnp.concatenate([latents(c), mask_t], axis=1)

    _flash_groups(nch, qx_scr, k_aug, lambda c, g: _with_ones(latents(c)),
                  lambda c: _tile_ids((t0 - c * tk) // LANES, n_sub, nd),
                  toe_ref, (lg0_scr, lg1_scr), (p0_scr, p1_scr), m_scr, acc_scr, tq)
    for hh in range(N_HEADS):
        rows = slice(hh * tq, (hh + 1) * tq)
        o = acc_scr[rows, 0:KV_LORA] / acc_scr[rows, KV_LORA:2 * KV_LORA]
        o_ref[0, :, hh * KV_LORA:(hh + 1) * KV_LORA] = o.astype(o_ref.dtype)


def _dsa_attn(qi, wi, kidx, qa, ckv, toe, tq=128, tk=512):
    B, _, T, _ = qa.shape
    tk = min(tk, T)
    assert tq == LANES
    nd = toe.shape[0] - 2
    top_k = min(IDX_TOPK_MAX, T // 4)
    kern = functools.partial(_dsa_attn_kernel, tq=tq, tk=tk, top_k=top_k, nd=nd,
                             idx_bits=max(1, (T - 1).bit_length()))
    return pl.pallas_call(
        kern,
        grid=(B, T // tq),
        in_specs=[
            pl.BlockSpec((1, IDX_HEADS, tq, IDX_DIM), lambda b, i: (b, 0, i, 0)),
            pl.BlockSpec((1, IDX_HEADS, tq), lambda b, i: (b, 0, i)),
            pl.BlockSpec((1, T, IDX_DIM), lambda b, i: (b, 0, 0), pipeline_mode=pl.Buffered(1)),
            pl.BlockSpec((1, N_HEADS, tq, KV_LORA), lambda b, i: (b, 0, i, 0)),
            pl.BlockSpec((1, T, KV_LORA), lambda b, i: (b, 0, 0), pipeline_mode=pl.Buffered(1)),
            _const_spec(toe.shape),
        ],
        out_specs=pl.BlockSpec((1, tq, N_HEADS * KV_LORA), lambda b, i: (b, i, 0)),
        out_shape=jax.ShapeDtypeStruct((B, T, N_HEADS * KV_LORA), BF16),
        scratch_shapes=[
            pltpu.VMEM((T // tk + 1, tk, tq), jnp.int32),
            pltpu.VMEM((tk, IDX_HEADS * tq), F32),
            pltpu.VMEM((tk, IDX_HEADS * tq), F32),
            pltpu.VMEM((HEADS_PER_GROUP * tq, tk), F32),
            pltpu.VMEM((HEADS_PER_GROUP * tq, tk), F32),
            pltpu.VMEM((HEADS_PER_GROUP * tq, tk), BF16),
            pltpu.VMEM((HEADS_PER_GROUP * tq, tk), BF16),
            pltpu.VMEM((N_HEADS * tq, KV_LORA + tq), BF16),
            pltpu.VMEM((N_HEADS, tq, LANES), F32),
            pltpu.VMEM((N_HEADS * tq, 2 * KV_LORA), F32),
        ],
        compiler_params=_params(("parallel", "arbitrary")),
        name="dsa_attn",
    )(qi, wi, kidx, qa, ckv, toe)


def _out_mlp_kernel(o_ref, x_ref, w_o_ref, g_ref, w_up_ref, w_down_ref, *rest, tf, latent, final):
    y_ref = rest[-1]
    o = o_ref[0]
    if latent:
        o = jnp.concatenate([_dot(o[:, hh * KV_LORA:(hh + 1) * KV_LORA], rest[0][hh]).astype(BF16)
                             for hh in range(N_HEADS)], axis=1)
    x = x_ref[0] + _dot(o, w_o_ref[...])
    h = _rms(x, g_ref[...]).astype(BF16)
    acc = x
    for f0 in range(0, w_up_ref.shape[1], tf):
        u = jnp.maximum(_dot(h, w_up_ref[:, f0:f0 + tf]), 0.0)
        acc = acc + _dot((u * u).astype(BF16), w_down_ref[f0:f0 + tf, :])
    y_ref[0] = _rms(acc, rest[-2][...]) if final else acc


def _out_mlp(o, x, w_o, g, w_up, w_down, w_uv=None, g_final=None, tm=512, tf=512):
    B, T, D = x.shape
    F = w_up.shape[1]
    tile = lambda b, i: (b, i, 0)
    latent, final = w_uv is not None, g_final is not None
    in_specs = [pl.BlockSpec((1, tm, o.shape[-1]), tile), pl.BlockSpec((1, tm, D), tile), _const_spec(w_o.shape),
                _const_spec((1, D)), _const_spec((D, F)), _const_spec((F, D))]
    args = [o, x, w_o.astype(BF16), g.reshape(1, D), w_up.astype(BF16), w_down.astype(BF16)]
    if latent:
        in_specs.append(_const_spec((N_HEADS, KV_LORA, HEAD_DIM)))
        args.append(jnp.transpose(w_uv, (1, 0, 2)).astype(BF16))
    if final:
        in_specs.append(_const_spec((1, D)))
        args.append(g_final.reshape(1, D))
    return pl.pallas_call(
        functools.partial(_out_mlp_kernel, tf=tf, latent=latent, final=final),
        grid=(B, T // tm),
        in_specs=in_specs,
        out_specs=pl.BlockSpec((1, tm, D), tile),
        out_shape=jax.ShapeDtypeStruct((B, T, D), F32),
        compiler_params=_params(("parallel", "parallel")),
        name="out_mlp_final" if final else "out_mlp",
    )(*args)


def _kv_proj_kernel(x_ref, g_ref, w_ref, kc_ref, vc_ref, ks_ref, vs_ref, kw_ref, vw_ref):
    h = _rms(x_ref[0], g_ref[...]).astype(BF16)
    kv = _dot(h, w_ref[...])
    for part, ref in enumerate((kc_ref, vc_ref, ks_ref, vs_ref, kw_ref, vw_ref)):
        for g in range(N_KV_GROUPS):
            o = (part * N_KV_GROUPS + g) * HEAD_DIM
            ref[0, g] = kv[:, o:o + HEAD_DIM].astype(ref.dtype)


def _kv_proj(x, g, w_kv, tm=512):
    B, T, D = x.shape
    gtile = lambda b, i: (b, 0, i, 0)
    spec = pl.BlockSpec((1, N_KV_GROUPS, tm, HEAD_DIM), gtile)
    shp = lambda dt: jax.ShapeDtypeStruct((B, N_KV_GROUPS, T, HEAD_DIM), dt)
    return pl.pallas_call(
        _kv_proj_kernel,
        grid=(B, T // tm),
        in_specs=[pl.BlockSpec((1, tm, D), lambda b, i: (b, i, 0)), _const_spec((1, D)),
                  _const_spec(w_kv.shape)],
        out_specs=[spec] * 6,
        out_shape=[shp(F32), shp(F32), shp(BF16), shp(BF16), shp(BF16), shp(BF16)],
        compiler_params=_params(("parallel", "parallel")),
        name="kv_proj",
    )(x, g.reshape(1, D), w_kv.astype(BF16))


def _compress_kernel(raw_ref, pos_ref, w1_ref, w2_ref, o_ref):
    rows = raw_ref[0, 0]
    half = rows.shape[1]
    a = _dot((rows + pos_ref[:, :half]).astype(BF16), w1_ref[:half, :])
    b = _dot((rows + pos_ref[:, half:]).astype(BF16), w1_ref[half:, :])
    pre = a + pltpu.roll(b, rows.shape[0] - 1, 0)
    act = 0.5 * pre * (1.0 + jnp.tanh(math.sqrt(2.0 / math.pi) * (pre + 0.044715 * (pre * pre * pre))))
    o_ref[0, 0] = _dot(act.astype(BF16), w2_ref[...]).astype(o_ref.dtype)


def _compress(raw, pos, w1, w2):
    B, G, T, Dh = raw.shape
    nr = T // CMP_STRIDE
    rows = raw.reshape(B, G, nr, CMP_STRIDE * Dh)
    return pl.pallas_call(
        _compress_kernel,
        grid=(B, G),
        in_specs=[pl.BlockSpec((1, 1, nr, CMP_STRIDE * Dh), lambda b, g: (b, g, 0, 0)),
                  _const_spec((1, CMP_LEN * Dh)), _const_spec(w1.shape), _const_spec(w2.shape)],
        out_specs=pl.BlockSpec((1, 1, nr, Dh), lambda b, g: (b, g, 0, 0)),
        out_shape=jax.ShapeDtypeStruct((B, G, nr, Dh), BF16),
        compiler_params=_params(("parallel", "parallel")),
        name="compress",
    )(rows, pos.reshape(1, CMP_LEN * Dh), w1.astype(BF16), w2.astype(BF16))


def _nsa_proj_kernel(x_ref, g_ref, wq_ref, wg_ref, q_ref, gate_ref):
    h = _rms(x_ref[0], g_ref[...]).astype(BF16)
    q = _dot(h, wq_ref[...]) * (HEAD_DIM ** -0.5 * LOG2E)
    for hh in range(N_HEADS):
        q_ref[0, hh] = q[:, hh * HEAD_DIM:(hh + 1) * HEAD_DIM].astype(BF16)
    gate_ref[0] = jax.nn.sigmoid(_dot(h, wg_ref[...]))[:, :3 * N_HEADS]


def _nsa_proj(x, g, w_in, tm=512):
    B, T, D = x.shape
    hd = N_HEADS * HEAD_DIM
    wq = w_in[:, :hd].astype(BF16)
    wg = jnp.zeros((D, LANES), F32).at[:, :3 * N_HEADS].set(w_in[:, hd:]).astype(BF16)
    return pl.pallas_call(
        _nsa_proj_kernel,
        grid=(B, T // tm),
        in_specs=[pl.BlockSpec((1, tm, D), lambda b, i: (b, i, 0)), _const_spec((1, D)),
                  _const_spec((D, hd)), _const_spec((D, LANES))],
        out_specs=[pl.BlockSpec((1, N_HEADS, tm, HEAD_DIM), lambda b, i: (b, 0, i, 0)),
                   pl.BlockSpec((1, tm, 3 * N_HEADS), lambda b, i: (b, i, 0))],
        out_shape=[jax.ShapeDtypeStruct((B, N_HEADS, T, HEAD_DIM), BF16),
                   jax.ShapeDtypeStruct((B, T, 3 * N_HEADS), F32)],
        compiler_params=_params(("parallel", "parallel")),
        name="nsa_proj",
    )(x, g.reshape(1, D), wq, wg)


def _top_blocks(imp, blk, n_sel, exact_ties):
    sel = jnp.zeros(imp.shape, F32)
    blk_f = blk.astype(F32)
    for _ in range(n_sel):
        if exact_ties:
            best = jnp.max(imp, axis=1, keepdims=True)
            hit = blk_f == jnp.min(jnp.where(imp == best, blk_f, float(LANES)), axis=1, keepdims=True)
        else:
            hit = blk == jnp.argmax(imp, axis=1, keepdims=True).astype(jnp.int32)
        sel = jnp.where(hit, 1.0, sel)
        imp = jnp.where(hit, -jnp.inf, imp)
    return sel


def _nsa_attn_kernel(q_ref, gate_ref, kc_ref, vc_ref, ks_ref, vs_ref, kw_ref, vw_ref, toe_ref,
                     o_ref, sel_scr, imp_scr, lg0_scr, lg1_scr, p0_scr, p1_scr, lw_scr, pw_scr, qx_scr, m_scr, acc_scr,
                     ocw_scr, *, tq, tk, nd, n_slc, n_sel):
    G, R = N_KV_GROUPS, HEADS_PER_GROUP
    lg_scrs, p_scrs = (lg0_scr, lg1_scr), (p0_scr, p1_scr)
    t0 = pl.program_id(1) * tq
    n_cmp = kc_ref.shape[2]
    n_sub = tk // LANES
    wlen = WIN + tq
    gates = gate_ref[0]
    t_col = t0 + lax.broadcasted_iota(jnp.int32, (tq, 1), 0)
    t_rows = t0 + lax.broadcasted_iota(jnp.int32, (ROWS, 1), 0)
    slabs = [slice(r * ROWS, (r + 1) * ROWS) for r in range(tq // ROWS)]

    cmp_end = CMP_STRIDE * lax.broadcasted_iota(jnp.int32, (ROWS, n_cmp), 1) + (CMP_LEN - 1)
    blk = lax.broadcasted_iota(jnp.int32, (tq, LANES), 1)
    cur = t_col // SLC_BLK
    forced = (blk == 0) | (blk == cur) | (blk == cur - 1)
    future = blk * SLC_BLK > t_col
    start = pl.multiple_of(jnp.maximum(t0 - WIN, 0), LANES)
    dw = (t0 - start) // LANES
    wtiles = _tile_ids(dw, wlen // LANES, nd)
    wtiles[0] = jnp.where(dw == WIN // LANES, nd + 1, wtiles[0])
    open_ties = []
    for g in range(G):
        lg_scr, p_scr = lg_scrs[g], p_scrs[g]
        qg = q_ref[0, g * R:(g + 1) * R].reshape(R * tq, HEAD_DIM)
        lg_scr[0:R * tq, 0:n_cmp] = _dot_nt(qg, kc_ref[0, g])
        for r, rows in enumerate(slabs):
            cmask = cmp_end <= t_rows + r * ROWS
            for rr in range(R):
                lc = jnp.where(cmask, lg_scr[rr * tq + r * ROWS:rr * tq + (r + 1) * ROWS, 0:n_cmp], NEG)
                e = jnp.where(cmask, jnp.exp2(lc - jnp.max(lc, axis=1, keepdims=True)), 0.0)
                p_scr[rr * tq + r * ROWS:rr * tq + (r + 1) * ROWS, 0:n_cmp] = e.astype(BF16)
        cmp_all = _dot(p_scr[:, 0:n_cmp], vc_ref[0, g])
        inv = 1.0 / jnp.maximum(cmp_all[:, HEAD_DIM:2 * HEAD_DIM], 1e-30)
        o_cmp = cmp_all[:, 0:HEAD_DIM] * inv
        imp_heads = cmp_all[:, 2 * HEAD_DIM:3 * HEAD_DIM] * inv

        lw_scr[0:R * tq, 0:wlen] = _dot_nt(qg, kw_ref[0, g, pl.ds(start, wlen), :])
        for rr in range(R):
            for r, rows in enumerate(slabs):
                lw = (lw_scr[rr * tq + r * ROWS:rr * tq + (r + 1) * ROWS, 0:wlen]
                      + _bias_rows(toe_ref, g * R + rr, wtiles, rows))
                pw = jnp.exp2(lw - jnp.max(lw, axis=1, keepdims=True))
                pw_scr[rr * tq + r * ROWS:rr * tq + (r + 1) * ROWS, 0:wlen] = pw.astype(BF16)
        o_win = _dot(pw_scr[:, 0:wlen], _with_ones(vw_ref[0, g, pl.ds(start, wlen), :]))
        for rr in range(R):
            hh = g * R + rr
            wrows = slice(rr * tq, (rr + 1) * tq)
            ocw_scr[hh * tq:(hh + 1) * tq, :] = (
                gates[:, 3 * hh:3 * hh + 1] * o_cmp[wrows]
                + gates[:, 3 * hh + 2:3 * hh + 3] * (o_win[wrows, 0:HEAD_DIM] / o_win[wrows, HEAD_DIM:2 * HEAD_DIM]))
        imp = _fold_rows(imp_heads, tq)
        imp = jnp.where(forced, 1e9, imp)
        imp = jnp.where(future, NEG, imp)
        imp = jnp.where(blk < n_slc, imp, -jnp.inf)
        sel = _top_blocks(imp, blk, n_sel, exact_ties=False)
        sel_scr[g] = sel
        imp_scr[g] = imp
        cut = jnp.min(jnp.where(sel > 0.0, imp, jnp.inf), axis=1, keepdims=True)
        open_ties.append(jnp.max(jnp.where((imp == cut) & (sel == 0.0) & ~future, 1.0, 0.0)))
        qx_scr[g * R * tq:(g + 1) * R * tq, 0:HEAD_DIM] = qg

    @pl.when(jnp.maximum(*open_ties) > 0.0)
    def _():
        for g in range(G):
            sel_scr[g] = _top_blocks(imp_scr[g], blk, n_sel, exact_ties=True)

    for g in range(G):
        block_bias = jnp.where((sel_scr[g] > 0.0) & ~future, 0.0, NEG).astype(BF16)
        for rr in range(R):
            qx_scr[(g * R + rr) * tq:(g * R + rr + 1) * tq, HEAD_DIM:HEAD_DIM + LANES] = block_bias

    nch = (t0 + tq + tk - 1) // tk
    lane_minus_block = (lax.broadcasted_iota(jnp.int32, (tk, LANES), 1)
                        - lax.broadcasted_iota(jnp.int32, (tk, LANES), 0) // SLC_BLK)

    def k_aug(c, g):
        s0 = pl.multiple_of(c * tk, tk)
        block_of_key = jnp.where(lane_minus_block == s0 // SLC_BLK, 1.0, 0.0).astype(BF16)
        return jnp.concatenate([ks_ref[0, g, pl.ds(s0, tk), :], block_of_key], axis=1)

    _flash_groups(nch, qx_scr, k_aug, lambda c, g: _with_ones(vs_ref[0, g, pl.ds(pl.multiple_of(c * tk, tk), tk), :]),
                  lambda c: _tile_ids((t0 - c * tk) // LANES, n_sub, nd),
                  toe_ref, lg_scrs, p_scrs, m_scr, acc_scr, tq)

    for hh in range(N_HEADS):
        rows = slice(hh * tq, (hh + 1) * tq)
        o_slc = acc_scr[rows, 0:HEAD_DIM] / acc_scr[rows, HEAD_DIM:2 * HEAD_DIM]
        o = ocw_scr[rows, :] + gates[:, 3 * hh + 1:3 * hh + 2] * o_slc
        o_ref[0, :, hh * HEAD_DIM:(hh + 1) * HEAD_DIM] = o.astype(o_ref.dtype)


def _selection_map(n_cmp, n_slc):
    c0 = CMP_STRIDE * np.arange(n_cmp)[:, None]
    s0 = SLC_BLK * np.arange(n_slc)[None, :]
    ov = np.clip(np.minimum(c0 + CMP_LEN, s0 + SLC_BLK) - np.maximum(c0, s0), 0, None)
    return (ov / CMP_LEN).astype(np.float32)


def _nsa_attn(q, gates, kc, vc, ks, vs, kw, vw, toe, tq=128, tk=512):
    B, _, T, _ = q.shape
    tk = min(tk, T)
    assert tq == LANES
    nd = toe.shape[0] - 2
    n_cmp_pad = kc.shape[2]
    n_cmp = (T - CMP_LEN) // CMP_STRIDE + 1
    n_slc = T // SLC_BLK
    assert n_slc <= LANES and T >= WIN + tq
    selmap = np.zeros((n_cmp_pad, LANES), np.float32)
    selmap[:n_cmp, :n_slc] = _selection_map(n_cmp, n_slc)
    vc_all = jnp.concatenate([vc, jnp.ones_like(vc), jnp.broadcast_to(jnp.asarray(selmap, BF16), vc.shape)], axis=-1)
    kern = functools.partial(_nsa_attn_kernel, tq=tq, tk=tk, nd=nd, n_slc=n_slc, n_sel=min(N_SLC_MAX, n_slc))
    G, R = N_KV_GROUPS, HEADS_PER_GROUP
    width = max(n_cmp_pad, tk, WIN + tq)
    res = lambda n, d=HEAD_DIM: pl.BlockSpec((1, G, n, d), lambda b, i: (b, 0, 0, 0), pipeline_mode=pl.Buffered(1))
    return pl.pallas_call(
        kern,
        grid=(B, T // tq),
        in_specs=[
            pl.BlockSpec((1, N_HEADS, tq, HEAD_DIM), lambda b, i: (b, 0, i, 0)),
            pl.BlockSpec((1, tq, 3 * N_HEADS), lambda b, i: (b, i, 0)),
            res(n_cmp_pad), res(n_cmp_pad, 3 * HEAD_DIM), res(T), res(T), res(T), res(T),
            _const_spec(toe.shape),
        ],
        out_specs=pl.BlockSpec((1, tq, N_HEADS * HEAD_DIM), lambda b, i: (b, i, 0)),
        out_shape=jax.ShapeDtypeStruct((B, T, N_HEADS * HEAD_DIM), BF16),
        scratch_shapes=[
            pltpu.VMEM((G, tq, LANES), F32),
            pltpu.VMEM((G, tq, LANES), F32),
            pltpu.VMEM((R * tq, width), F32),
            pltpu.VMEM((R * tq, width), F32),
            pltpu.VMEM((R * tq, width), BF16),
            pltpu.VMEM((R * tq, width), BF16),
            pltpu.VMEM((R * tq, WIN + tq), F32),
            pltpu.VMEM((R * tq, WIN + tq), BF16),
            pltpu.VMEM((N_HEADS * tq, HEAD_DIM + LANES), BF16),
            pltpu.VMEM((N_HEADS, tq, LANES), F32),
            pltpu.VMEM((N_HEADS * tq, 2 * HEAD_DIM), F32),
            pltpu.VMEM((N_HEADS * tq, HEAD_DIM), F32),
        ],
        compiler_params=_params(("parallel", "arbitrary")),
        name="nsa_attn",
    )(q, gates, kc, vc_all, ks, vs, kw, vw, toe)


def _dsa_attention(x, toe, g_attn, w_in, gq, gkv, gk, w_uq, w_q_idx, w_uk):
    qa, qi, ckv, kidx, wi = _dsa_proj(x, g_attn, w_in, gq, gkv, gk, w_uq, w_uk, w_q_idx)
    return _dsa_attn(qi, wi, kidx, qa, ckv, toe)


def _shared_kv(x, g_kv, w_kv, pos_k, pos_v, w1_k, w2_k, w1_v, w2_v):
    kc_raw, vc_raw, ks, vs, kw, vw = _kv_proj(x, g_kv, w_kv)
    return _compress(kc_raw, pos_k, w1_k, w2_k), _compress(vc_raw, pos_v, w1_v, w2_v), ks, vs, kw, vw


def _nsa_attention(x, toe, kv_shared, g_attn, w_in):
    q, gates = _nsa_proj(x, g_attn, w_in)
    return _nsa_attn(q, gates, *kv_shared, toe)


def kernel(x, g_attn, g_mlp, w_up, w_down, rel_bias, a_w_in, a_g_q_lat, a_g_kv_lat, a_g_k_idx, a_w_uq, a_w_q_idx, a_w_uk, a_w_uv, a_w_o, g_kv_shared, w_kv_shared, cmp_pos_k, cmp_pos_v, cmp_w1_k, cmp_w2_k, cmp_w1_v, cmp_w2_v, b_w_in, b_w_o, g_final):
    depth = g_attn.shape[0]
    n_a = a_w_in.shape[0]
    toe = _bias_tiles(rel_bias, x.shape[1])
    kv_shared = None
    for l in range(depth):
        if l < n_a:
            o = _dsa_attention(x, toe, g_attn[l], a_w_in[l], a_g_q_lat[l], a_g_kv_lat[l], a_g_k_idx[l],
                               a_w_uq[l], a_w_q_idx[l], a_w_uk[l])
            w_o, w_uv = a_w_o[l], a_w_uv[l]
        else:
            o = _nsa_attention(x, toe, kv_shared, g_attn[l], b_w_in[l - n_a])
            w_o, w_uv = b_w_o[l - n_a], None
        x = _out_mlp(o, x, w_o, g_mlp[l], w_up[l], w_down[l], w_uv, g_final if l == depth - 1 else None)
        if l == n_a - 1:
            kv_shared = _shared_kv(x, g_kv_shared, w_kv_shared, cmp_pos_k, cmp_pos_v,
                                   cmp_w1_k, cmp_w2_k, cmp_w1_v, cmp_w2_v)
    return x
```

```python
import functools
import math

import numpy as np
import jax
import jax.numpy as jnp
from jax import lax
from jax.experimental import pallas as pl
from jax.experimental.pallas import tpu as pltpu

N_HEADS = 8
HEAD_DIM = 128
Q_LORA = 256
KV_LORA = 128
IDX_HEADS = 8
IDX_DIM = 64
IDX_TOPK_MAX = 256
N_KV_GROUPS = 2
HEADS_PER_GROUP = N_HEADS // N_KV_GROUPS
CMP_LEN = 32
CMP_STRIDE = 16
CMP_HID = 256
SLC_BLK = 64
N_SLC_MAX = 16
WIN = 512
REL_BUCKETS = 32
REL_MAX_DIST = 4096
EPS = 1e-6
NEG = -1e30
LOG2E = math.log2(math.e)

LANES = 128
ROWS = 16
KEY_ROWS = 64
INT_MIN = np.int32(-2 ** 31)
INT_MAX = np.int32(2 ** 31 - 1)
FIXED_PROBES = 15
STRAGGLER_PROBES = 40
VMEM_LIMIT = 56 * 1024 * 1024

F32 = jnp.float32
BF16 = jnp.bfloat16
NT_DIMS = (((1,), (1,)), ((), ()))


def _dot(a, b):
    return jnp.dot(a, b, preferred_element_type=F32)


def _dot_nt(a, b):
    return lax.dot_general(a, b, NT_DIMS, preferred_element_type=F32)


def _rms(x, g):
    return x * lax.rsqrt(jnp.mean(x * x, axis=-1, keepdims=True) + EPS) * g


def _to_key(v):
    bits = lax.bitcast_convert_type(v, jnp.int32)
    return bits ^ ((bits >> 31) & jnp.int32(0x7FFFFFFF))


def _const_spec(shape):
    nd = len(shape)
    return pl.BlockSpec(shape, lambda *_: (0,) * nd, pipeline_mode=pl.Buffered(1))


def _params(sem):
    return pltpu.CompilerParams(dimension_semantics=sem, vmem_limit_bytes=VMEM_LIMIT)


def _rel_bucket(dist):
    dist = jnp.maximum(dist, 0)
    exact = REL_BUCKETS // 2
    log_ratio = jnp.log(jnp.maximum(dist, 1).astype(F32) / exact) / math.log(REL_MAX_DIST / exact)
    large = exact + (log_ratio * (REL_BUCKETS - exact)).astype(jnp.int32)
    return jnp.where(dist < exact, dist, jnp.minimum(large, REL_BUCKETS - 1))


def _num_bias_tiles(T):
    exact = REL_BUCKETS // 2
    switch = exact * (REL_MAX_DIST / exact) ** ((REL_BUCKETS - exact - 1) / (REL_BUCKETS - exact))
    far = int(math.ceil(switch)) + 32
    return min(T // LANES, -(-(far + LANES - 1) // LANES) + 1)


def _bias_tiles(rel_bias, T):
    nd = _num_bias_tiles(T)
    assert nd > WIN // LANES
    c = LANES * (nd - 1)
    bd = rel_bias[_rel_bucket(jnp.arange(c + LANES, dtype=jnp.int32))].T.astype(F32) * LOG2E
    p = c + 2 * LANES
    w = jnp.concatenate([bd[:, c::-1], jnp.broadcast_to(bd[:, :1], (N_HEADS, LANES)), bd[:, :c:-1]], axis=1)
    band = jnp.tile(w, (1, LANES))[:, :LANES * (p - 1)].reshape(N_HEADS, LANES, p - 1)[:, :, :c + LANES]
    tiles = band.reshape(N_HEADS, LANES, nd, LANES)[:, :, ::-1]
    tiles = jnp.transpose(tiles, (2, 0, 1, 3))
    i = jnp.arange(LANES)[:, None]
    j = jnp.arange(LANES)[None, :]
    ahead = jnp.full((1,) + tiles.shape[1:], NEG, F32)
    diag = jnp.where(i >= j, tiles[:1], NEG)
    win_edge = jnp.where(i < j, tiles[WIN // LANES:WIN // LANES + 1], NEG)
    return jnp.concatenate([ahead, diag, tiles[1:], win_edge], axis=0)


def _tile_ids(d0, n_sub, nd):
    return [jnp.clip(d0 - k, -1, nd - 1) + 1 for k in range(n_sub)]


def _dsa_proj_kernel(x_ref, g_ref, w_in_ref, w_wt_ref, gq_ref, gkv_ref, gk_ref, w_uq_ref, w_ukt_ref, w_qi_ref,
                     qa_ref, qi_ref, ckv_ref, kidx_ref, wi_ref):
    h = _rms(x_ref[0], g_ref[...]).astype(BF16)
    proj = _dot(h, w_in_ref[...])
    c_q = _rms(proj[:, :Q_LORA], gq_ref[...]).astype(BF16)
    ckv_ref[0] = _rms(proj[:, Q_LORA:Q_LORA + KV_LORA], gkv_ref[...]).astype(BF16)
    o_k = Q_LORA + KV_LORA
    kidx_ref[0] = _rms(proj[:, o_k:o_k + IDX_DIM], gk_ref[...]).astype(BF16)
    wi_ref[0] = _dot_nt(w_wt_ref[...], h) * (IDX_HEADS ** -0.5 * IDX_DIM ** -0.5)
    q = _dot(c_q, w_uq_ref[...])
    for hh in range(N_HEADS):
        qh = q[:, hh * HEAD_DIM:(hh + 1) * HEAD_DIM].astype(BF16)
        qa_ref[0, hh] = (_dot(qh, w_ukt_ref[hh]) * (HEAD_DIM ** -0.5 * LOG2E)).astype(BF16)
        qi_ref[0, hh] = _dot(c_q, w_qi_ref[hh]).astype(BF16)


def _dsa_proj(x, g, w_in, gq, gkv, gk, w_uq, w_uk, w_q_idx, tm=512):
    B, T, D = x.shape
    n_in = Q_LORA + KV_LORA + LANES
    o_k = Q_LORA + KV_LORA
    w_in_p = jnp.zeros((D, n_in), F32).at[:, :o_k + IDX_DIM].set(w_in[:, :o_k + IDX_DIM])
    w_wt = w_in[:, o_k + IDX_DIM:].T.astype(BF16)
    w_ukt = jnp.transpose(w_uk, (1, 2, 0)).astype(BF16)
    w_qi = jnp.transpose(w_q_idx, (1, 0, 2)).astype(BF16)
    tile = lambda b, i: (b, i, 0)
    htile = lambda b, i: (b, 0, i, 0)
    return pl.pallas_call(
        _dsa_proj_kernel,
        grid=(B, T // tm),
        in_specs=[
            pl.BlockSpec((1, tm, D), tile),
            _const_spec((1, D)),
            _const_spec((D, n_in)),
            _const_spec((IDX_HEADS, D)),
            _const_spec((1, Q_LORA)), _const_spec((1, KV_LORA)), _const_spec((1, IDX_DIM)),
            _const_spec((Q_LORA, N_HEADS * HEAD_DIM)),
            _const_spec((N_HEADS, HEAD_DIM, KV_LORA)),
            _const_spec((IDX_HEADS, Q_LORA, IDX_DIM)),
        ],
        out_specs=[
            pl.BlockSpec((1, N_HEADS, tm, KV_LORA), htile),
            pl.BlockSpec((1, IDX_HEADS, tm, IDX_DIM), htile),
            pl.BlockSpec((1, tm, KV_LORA), tile),
            pl.BlockSpec((1, tm, IDX_DIM), tile),
            pl.BlockSpec((1, IDX_HEADS, tm), lambda b, i: (b, 0, i)),
        ],
        out_shape=[
            jax.ShapeDtypeStruct((B, N_HEADS, T, KV_LORA), BF16),
            jax.ShapeDtypeStruct((B, IDX_HEADS, T, IDX_DIM), BF16),
            jax.ShapeDtypeStruct((B, T, KV_LORA), BF16),
            jax.ShapeDtypeStruct((B, T, IDX_DIM), BF16),
            jax.ShapeDtypeStruct((B, IDX_HEADS, T), F32),
        ],
        compiler_params=_params(("parallel", "parallel")),
        name="dsa_proj",
    )(x, g.reshape(1, D), w_in_p.astype(BF16), w_wt, gq.reshape(1, -1), gkv.reshape(1, -1), gk.reshape(1, -1),
      w_uq.reshape(Q_LORA, N_HEADS * HEAD_DIM).astype(BF16), w_ukt, w_qi)


def _tile_lanes(a, n):
    return a if n == 1 else jnp.concatenate([a] * n, axis=1)


def _bias_rows(toe_ref, head, tiles, rows):
    parts = [toe_ref[d, head, rows, :] for d in tiles]
    return parts[0] if len(parts) == 1 else jnp.concatenate(parts, axis=1)


def _flash_head(lg_scr, lrow0, toe_ref, tiles, head, m_scr, acc_scr, p_scr, prow0, tq):
    n_rep = len(tiles)
    tk = n_rep * LANES
    for r in range(tq // ROWS):
        rows = slice(r * ROWS, (r + 1) * ROWS)
        lg = lg_scr[lrow0 + r * ROWS:lrow0 + (r + 1) * ROWS, 0:tk] + _bias_rows(toe_ref, head, tiles, rows)
        m_old = m_scr[head, rows]
        m_new = jnp.maximum(m_old, jnp.max(lg, axis=1, keepdims=True))
        m_scr[head, rows] = m_new
        p_scr[prow0 + r * ROWS:prow0 + (r + 1) * ROWS, 0:tk] = jnp.exp2(lg - _tile_lanes(m_new, n_rep)).astype(BF16)
        arows = slice(head * tq + r * ROWS, head * tq + (r + 1) * ROWS)
        acc_scr[arows, :] = acc_scr[arows, :] * _tile_lanes(jnp.exp2(m_old - m_new), 2)


def _with_ones(v):
    return jnp.concatenate([v, jnp.ones(v.shape, v.dtype)], axis=1)


def _flash_groups(nch, qx_scr, k_aug, v_aug, tile_ids, toe_ref, lg_scrs, p_scrs, m_scr, acc_scr, tq):
    R = HEADS_PER_GROUP
    grows = [slice(g * R * tq, (g + 1) * R * tq) for g in range(N_KV_GROUPS)]

    def logits(c, g):
        k = k_aug(c, g)
        lg_scrs[g][:, 0:k.shape[0]] = _dot_nt(qx_scr[grows[g], :], k)

    def softmax(c, g):
        tiles = tile_ids(c)
        for rr in range(R):
            _flash_head(lg_scrs[g], rr * tq, toe_ref, tiles, g * R + rr, m_scr, acc_scr, p_scrs[g], rr * tq, tq)

    def values(c, g):
        v = v_aug(c, g)
        acc_scr[grows[g], :] = acc_scr[grows[g], :] + _dot(p_scrs[g][:, 0:v.shape[0]], v)

    m_scr[...] = jnp.full(m_scr.shape, NEG, F32)
    acc_scr[...] = jnp.zeros(acc_scr.shape, F32)
    p_scrs[1][...] = jnp.zeros(p_scrs[1].shape, BF16)
    logits(0, 0)

    def chunk(c, carry):
        logits(c, 1)
        softmax(c, 0)
        values(jnp.maximum(c - 1, 0), 1)
        logits(jnp.minimum(c + 1, nch - 1), 0)
        softmax(c, 1)
        values(c, 0)
        return carry

    lax.fori_loop(0, nch, chunk, 0)
    values(nch - 1, 1)


def _fold_rows(a, n, op=jnp.add):
    parts = [a[i:i + n] for i in range(0, a.shape[0], n)]
    while len(parts) > 1:
        parts = [op(parts[i], parts[i + 1]) for i in range(0, len(parts), 2)]
    return parts[0]


def _from_key(k):
    return lax.bitcast_convert_type(k ^ ((k >> 31) & jnp.int32(0x7FFFFFFF)), F32)


def _dsa_attn_kernel(qi_ref, wi_ref, kidx_ref, qa_ref, ckv_ref, toe_ref, o_ref,
                     s_scr, rel0_scr, rel1_scr, lg0_scr, lg1_scr, p0_scr, p1_scr, qx_scr, m_scr, acc_scr,
                     *, tq, tk, top_k, nd, idx_bits):
    t0 = pl.program_id(1) * tq
    nch = (t0 + tq + tk - 1) // tk
    n_sub = tk // LANES
    sub = 8
    key_row = lax.broadcasted_iota(jnp.int32, (tk, tq), 0)

    wi = wi_ref[0]
    qi = qi_ref[0].reshape(IDX_HEADS * tq, IDX_DIM)
    k_slab = lax.broadcasted_iota(jnp.int32, (KEY_ROWS, tq), 0)
    q_slab = lax.broadcasted_iota(jnp.int32, (KEY_ROWS, tq), 1)

    def head_dots(c, rel_scr):
        s0 = pl.multiple_of(jnp.minimum(c, nch - 1) * tk, tk)
        rel_scr[...] = _dot_nt(kidx_ref[0, pl.ds(s0, tk), :], qi)

    def scores(c, rel_scr, top):
        s0 = c * tk
        for r in range(tk // KEY_ROWS):
            rows = slice(r * KEY_ROWS, (r + 1) * KEY_ROWS)
            acc = jnp.zeros((KEY_ROWS, tq), F32)
            for hh in range(IDX_HEADS):
                acc = acc + wi[hh:hh + 1, :] * jnp.maximum(rel_scr[rows, hh * tq:(hh + 1) * tq], 0.0)
            acc = acc + 0.0
            valid = k_slab + (s0 + r * KEY_ROWS) <= q_slab + t0
            s_scr[c, rows, :] = jnp.where(valid, _to_key(acc), INT_MIN)
            top = jnp.maximum(top, _fold_rows(jnp.where(valid, jnp.abs(acc), 0.0), sub, jnp.maximum))
        return top

    def score_pair(j, top):
        head_dots(2 * j + 1, rel1_scr)
        top = scores(2 * j, rel0_scr, top)
        head_dots(2 * j + 2, rel0_scr)
        return scores(2 * j + 1, rel1_scr, top)

    head_dots(0, rel0_scr)
    top = lax.fori_loop(0, (nch + 1) // 2, score_pair, jnp.zeros((sub, tq), F32))
    top = jnp.max(top, axis=0, keepdims=True)

    def count(pred):
        def body(j, cnt):
            for c in (2 * j, 2 * j + 1):
                cnt = cnt + _fold_rows(jnp.where(pred(s_scr[c], key_row + c * tk), 1.0, 0.0), sub)
            return cnt
        cnt = lax.fori_loop(0, (nch + 1) // 2, body, jnp.zeros((sub, tq), F32))
        return jnp.sum(cnt, axis=0, keepdims=True)

    kf = jnp.float32(top_k)

    def settled(st):
        lo, hi, n_lo, _ = st
        return (n_lo <= kf) | (hi - 1 <= lo)

    def score_mid(st):
        lo_v, hi_v = _from_key(st[0]), _from_key(st[1])
        return _to_key(lo_v + (hi_v - lo_v) * 0.5 + 0.0)

    def probe(st, want):
        lo, hi, n_lo, n_hi = st
        done = settled(st)
        cand = jnp.where((want > lo) & (want < hi), want, (lo >> 1) + (hi >> 1) + (lo & hi & 1))
        cand = jnp.where(done, lo, cand)
        cnt = count(lambda blk, idx: blk >= cand)
        up = (cnt >= kf) & ~done
        down = ~(up | done)
        return (jnp.where(up, cand, lo), jnp.where(down, cand, hi),
                jnp.where(up, cnt, n_lo), jnp.where(down, cnt, n_hi))

    n0 = (t0 + 1 + lax.broadcasted_iota(jnp.int32, (1, tq), 1)).astype(F32)
    st = (_to_key(-top), _to_key(top) + 1, n0, jnp.zeros((1, tq), F32))
    st = lax.fori_loop(0, FIXED_PROBES, lambda i, s: probe(s, score_mid(s)), st)

    lo, hi, n_lo, n_hi = st

    def inside_range(c, carry):
        blk = s_scr[c]
        inside = (blk >= lo) & (blk < hi)
        return (jnp.maximum(carry[0], _fold_rows(jnp.where(inside, blk, INT_MIN), sub, jnp.maximum)),
                jnp.minimum(carry[1], _fold_rows(jnp.where(inside, blk, INT_MAX), sub, jnp.minimum)))

    big, small = lax.fori_loop(0, nch, inside_range, (jnp.full((sub, tq), INT_MIN, jnp.int32),
                                                      jnp.full((sub, tq), INT_MAX, jnp.int32)))
    big = _fold_rows(big, 1, jnp.maximum)
    small = _fold_rows(small, 1, jnp.minimum)
    flat = (big == small) & ~settled(st)
    st = (jnp.where(flat, big, lo), jnp.where(flat, big + 1, hi), n_lo, n_hi)
    want = jnp.where(kf - n_hi == 1.0, big, jnp.where(kf - n_hi == n_lo - n_hi - 1.0, small + 1, score_mid(st)))
    st = probe(st, want)

    def unsettled(st):
        return jnp.sum(jnp.where(settled(st), 0.0, 1.0))

    def more_cond(c):
        return (c[0] < STRAGGLER_PROBES) & (c[1] > 0.0)

    def more_probe(c):
        it, _, st = c
        st = probe(st, jnp.where(it < 4, score_mid(st), st[0]))
        return it + 1, unsettled(st), st

    _, _, st = lax.while_loop(more_cond, more_probe, (jnp.int32(0), unsettled(st), st))
    thr, n_ge = st[0], st[2]

    @pl.when(jnp.max(n_ge) > kf)
    def _():
        need = kf - count(lambda blk, idx: blk > thr)

        def idx_bit(bi, last):
            cand = last | lax.shift_left(jnp.int32(1), idx_bits - 1 - bi)
            cnt = count(lambda blk, idx: (blk == thr) & (idx < cand))
            return jnp.where(cnt < need, cand, last)

        last = lax.fori_loop(0, idx_bits, idx_bit, jnp.zeros((1, tq), jnp.int32))

        def drop(c, carry):
            blk = s_scr[c]
            s_scr[c] = jnp.where((blk == thr) & (key_row + c * tk > last), INT_MIN, blk)
            return carry

        lax.fori_loop(0, nch, drop, 0)

    eye = (lax.broadcasted_iota(jnp.int32, (tq, tq), 0) == lax.broadcasted_iota(jnp.int32, (tq, tq), 1))
    qx_scr[:, 0:KV_LORA] = qa_ref[0].reshape(N_HEADS * tq, KV_LORA)
    for hh in range(N_HEADS):
        qx_scr[hh * tq:(hh + 1) * tq, KV_LORA:KV_LORA + tq] = jnp.where(eye, 1.0, 0.0).astype(BF16)

    def latents(c):
        return ckv_ref[0, pl.ds(pl.multiple_of(c * tk, tk), tk), :]

    def k_aug(c, g):
        mask_t = jnp.where(s_scr[c] >= thr, 0.0, NEG).astype(BF16)
        return jnp.concatenate([latents(c), mask_t], axis=1)

    _flash_groups(nch, qx_scr, k_aug, lambda c, g: _with_ones(latents(c)),
                  lambda c: _tile_ids((t0 - c * tk) // LANES, n_sub, nd),
                  toe_ref, (lg0_scr, lg1_scr), (p0_scr, p1_scr), m_scr, acc_scr, tq)
    for hh in range(N_HEADS):
        rows = slice(hh * tq, (hh + 1) * tq)
        o = acc_scr[rows, 0:KV_LORA] / acc_scr[rows, KV_LORA:2 * KV_LORA]
        o_ref[0, :, hh * KV_LORA:(hh + 1) * KV_LORA] = o.astype(o_ref.dtype)


def _dsa_attn(qi, wi, kidx, qa, ckv, toe, tq=128, tk=512):
    B, _, T, _ = qa.shape
    tk = min(tk, T)
    assert tq == LANES
    nd = toe.shape[0] - 2
    top_k = min(IDX_TOPK_MAX, T // 4)
    kern = functools.partial(_dsa_attn_kernel, tq=tq, tk=tk, top_k=top_k, nd=nd,
                             idx_bits=max(1, (T - 1).bit_length()))
    return pl.pallas_call(
        kern,
        grid=(B, T // tq),
        in_specs=[
            pl.BlockSpec((1, IDX_HEADS, tq, IDX_DIM), lambda b, i: (b, 0, i, 0)),
            pl.BlockSpec((1, IDX_HEADS, tq), lambda b, i: (b, 0, i)),
            pl.BlockSpec((1, T, IDX_DIM), lambda b, i: (b, 0, 0), pipeline_mode=pl.Buffered(1)),
            pl.BlockSpec((1, N_HEADS, tq, KV_LORA), lambda b, i: (b, 0, i, 0)),
            pl.BlockSpec((1, T, KV_LORA), lambda b, i: (b, 0, 0), pipeline_mode=pl.Buffered(1)),
            _const_spec(toe.shape),
        ],
        out_specs=pl.BlockSpec((1, tq, N_HEADS * KV_LORA), lambda b, i: (b, i, 0)),
        out_shape=jax.ShapeDtypeStruct((B, T, N_HEADS * KV_LORA), BF16),
        scratch_shapes=[
            pltpu.VMEM((T // tk + 1, tk, tq), jnp.int32),
            pltpu.VMEM((tk, IDX_HEADS * tq), F32),
            pltpu.VMEM((tk, IDX_HEADS * tq), F32),
            pltpu.VMEM((HEADS_PER_GROUP * tq, tk), F32),
            pltpu.VMEM((HEADS_PER_GROUP * tq, tk), F32),
            pltpu.VMEM((HEADS_PER_GROUP * tq, tk), BF16),
            pltpu.VMEM((HEADS_PER_GROUP * tq, tk), BF16),
            pltpu.VMEM((N_HEADS * tq, KV_LORA + tq), BF16),
            pltpu.VMEM((N_HEADS, tq, LANES), F32),
            pltpu.VMEM((N_HEADS * tq, 2 * KV_LORA), F32),
        ],
        compiler_params=_params(("parallel", "arbitrary")),
        name="dsa_attn",
    )(qi, wi, kidx, qa, ckv, toe)


def _out_mlp_kernel(o_ref, x_ref, w_o_ref, g_ref, w_up_ref, w_down_ref, *rest, tf, latent, final):
    y_ref = rest[-1]
    o = o_ref[0]
    if latent:
        o = jnp.concatenate([_dot(o[:, hh * KV_LORA:(hh + 1) * KV_LORA], rest[0][hh]).astype(BF16)
                             for hh in range(N_HEADS)], axis=1)
    x = x_ref[0] + _dot(o, w_o_ref[...])
    h = _rms(x, g_ref[...]).astype(BF16)
    acc = x
    for f0 in range(0, w_up_ref.shape[1], tf):
        u = jnp.maximum(_dot(h, w_up_ref[:, f0:f0 + tf]), 0.0)
        acc = acc + _dot((u * u).astype(BF16), w_down_ref[f0:f0 + tf, :])
    y_ref[0] = _rms(acc, rest[-2][...]) if final else acc


def _out_mlp(o, x, w_o, g, w_up, w_down, w_uv=None, g_final=None, tm=512, tf=512):
    B, T, D = x.shape
    F = w_up.shape[1]
    tile = lambda b, i: (b, i, 0)
    latent, final = w_uv is not None, g_final is not None
    in_specs = [pl.BlockSpec((1, tm, o.shape[-1]), tile), pl.BlockSpec((1, tm, D), tile), _const_spec(w_o.shape),
                _const_spec((1, D)), _const_spec((D, F)), _const_spec((F, D))]
    args = [o, x, w_o.astype(BF16), g.reshape(1, D), w_up.astype(BF16), w_down.astype(BF16)]
    if latent:
        in_specs.append(_const_spec((N_HEADS, KV_LORA, HEAD_DIM)))
        args.append(jnp.transpose(w_uv, (1, 0, 2)).astype(BF16))
    if final:
        in_specs.append(_const_spec((1, D)))
        args.append(g_final.reshape(1, D))
    return pl.pallas_call(
        functools.partial(_out_mlp_kernel, tf=tf, latent=latent, final=final),
        grid=(B, T // tm),
        in_specs=in_specs,
        out_specs=pl.BlockSpec((1, tm, D), tile),
        out_shape=jax.ShapeDtypeStruct((B, T, D), F32),
        compiler_params=_params(("parallel", "parallel")),
        name="out_mlp_final" if final else "out_mlp",
    )(*args)


def _kv_proj_kernel(x_ref, g_ref, w_ref, kc_ref, vc_ref, ks_ref, vs_ref, kw_ref, vw_ref):
    h = _rms(x_ref[0], g_ref[...]).astype(BF16)
    kv = _dot(h, w_ref[...])
    for part, ref in enumerate((kc_ref, vc_ref, ks_ref, vs_ref, kw_ref, vw_ref)):
        for g in range(N_KV_GROUPS):
            o = (part * N_KV_GROUPS + g) * HEAD_DIM
            ref[0, g] = kv[:, o:o + HEAD_DIM].astype(ref.dtype)


def _kv_proj(x, g, w_kv, tm=512):
    B, T, D = x.shape
    gtile = lambda b, i: (b, 0, i, 0)
    spec = pl.BlockSpec((1, N_KV_GROUPS, tm, HEAD_DIM), gtile)
    shp = lambda dt: jax.ShapeDtypeStruct((B, N_KV_GROUPS, T, HEAD_DIM), dt)
    return pl.pallas_call(
        _kv_proj_kernel,
        grid=(B, T // tm),
        in_specs=[pl.BlockSpec((1, tm, D), lambda b, i: (b, i, 0)), _const_spec((1, D)),
                  _const_spec(w_kv.shape)],
        out_specs=[spec] * 6,
        out_shape=[shp(F32), shp(F32), shp(BF16), shp(BF16), shp(BF16), shp(BF16)],
        compiler_params=_params(("parallel", "parallel")),
        name="kv_proj",
    )(x, g.reshape(1, D), w_kv.astype(BF16))


def _compress_kernel(raw_ref, pos_ref, w1_ref, w2_ref, o_ref):
    rows = raw_ref[0, 0]
    half = rows.shape[1]
    a = _dot((rows + pos_ref[:, :half]).astype(BF16), w1_ref[:half, :])
    b = _dot((rows + pos_ref[:, half:]).astype(BF16), w1_ref[half:, :])
    pre = a + pltpu.roll(b, rows.shape[0] - 1, 0)
    act = 0.5 * pre * (1.0 + jnp.tanh(math.sqrt(2.0 / math.pi) * (pre + 0.044715 * (pre * pre * pre))))
    o_ref[0, 0] = _dot(act.astype(BF16), w2_ref[...]).astype(o_ref.dtype)


def _compress(raw, pos, w1, w2):
    B, G, T, Dh = raw.shape
    nr = T // CMP_STRIDE
    rows = raw.reshape(B, G, nr, CMP_STRIDE * Dh)
    return pl.pallas_call(
        _compress_kernel,
        grid=(B, G),
        in_specs=[pl.BlockSpec((1, 1, nr, CMP_STRIDE * Dh), lambda b, g: (b, g, 0, 0)),
                  _const_spec((1, CMP_LEN * Dh)), _const_spec(w1.shape), _const_spec(w2.shape)],
        out_specs=pl.BlockSpec((1, 1, nr, Dh), lambda b, g: (b, g, 0, 0)),
        out_shape=jax.ShapeDtypeStruct((B, G, nr, Dh), BF16),
        compiler_params=_params(("parallel", "parallel")),
        name="compress",
    )(rows, pos.reshape(1, CMP_LEN * Dh), w1.astype(BF16), w2.astype(BF16))


def _nsa_proj_kernel(x_ref, g_ref, wq_ref, wg_ref, q_ref, gate_ref):
    h = _rms(x_ref[0], g_ref[...]).astype(BF16)
    q = _dot(h, wq_ref[...]) * (HEAD_DIM ** -0.5 * LOG2E)
    for hh in range(N_HEADS):
        q_ref[0, hh] = q[:, hh * HEAD_DIM:(hh + 1) * HEAD_DIM].astype(BF16)
    gate_ref[0] = jax.nn.sigmoid(_dot(h, wg_ref[...]))[:, :3 * N_HEADS]


def _nsa_proj(x, g, w_in, tm=512):
    B, T, D = x.shape
    hd = N_HEADS * HEAD_DIM
    wq = w_in[:, :hd].astype(BF16)
    wg = jnp.zeros((D, LANES), F32).at[:, :3 * N_HEADS].set(w_in[:, hd:]).astype(BF16)
    return pl.pallas_call(
        _nsa_proj_kernel,
        grid=(B, T // tm),
        in_specs=[pl.BlockSpec((1, tm, D), lambda b, i: (b, i, 0)), _const_spec((1, D)),
                  _const_spec((D, hd)), _const_spec((D, LANES))],
        out_specs=[pl.BlockSpec((1, N_HEADS, tm, HEAD_DIM), lambda b, i: (b, 0, i, 0)),
                   pl.BlockSpec((1, tm, 3 * N_HEADS), lambda b, i: (b, i, 0))],
        out_shape=[jax.ShapeDtypeStruct((B, N_HEADS, T, HEAD_DIM), BF16),
                   jax.ShapeDtypeStruct((B, T, 3 * N_HEADS), F32)],
        compiler_params=_params(("parallel", "parallel")),
        name="nsa_proj",
    )(x, g.reshape(1, D), wq, wg)


def _top_blocks(imp, blk, n_sel, exact_ties):
    sel = jnp.zeros(imp.shape, F32)
    blk_f = blk.astype(F32)
    for _ in range(n_sel):
        if exact_ties:
            best = jnp.max(imp, axis=1, keepdims=True)
            hit = blk_f == jnp.min(jnp.where(imp == best, blk_f, float(LANES)), axis=1, keepdims=True)
        else:
            hit = blk == jnp.argmax(imp, axis=1, keepdims=True).astype(jnp.int32)
        sel = jnp.where(hit, 1.0, sel)
        imp = jnp.where(hit, -jnp.inf, imp)
    return sel


def _nsa_attn_kernel(q_ref, gate_ref, kc_ref, vc_ref, ks_ref, vs_ref, kw_ref, vw_ref, toe_ref,
                     o_ref, sel_scr, imp_scr, lg0_scr, lg1_scr, p0_scr, p1_scr, lw_scr, pw_scr, qx_scr, m_scr, acc_scr,
                     ocw_scr, *, tq, tk, nd, n_slc, n_sel):
    G, R = N_KV_GROUPS, HEADS_PER_GROUP
    lg_scrs, p_scrs = (lg0_scr, lg1_scr), (p0_scr, p1_scr)
    t0 = pl.program_id(1) * tq
    n_cmp = kc_ref.shape[2]
    n_sub = tk // LANES
    wlen = WIN + tq
    gates = gate_ref[0]
    t_col = t0 + lax.broadcasted_iota(jnp.int32, (tq, 1), 0)
    t_rows = t0 + lax.broadcasted_iota(jnp.int32, (ROWS, 1), 0)
    slabs = [slice(r * ROWS, (r + 1) * ROWS) for r in range(tq // ROWS)]

    cmp_end = CMP_STRIDE * lax.broadcasted_iota(jnp.int32, (ROWS, n_cmp), 1) + (CMP_LEN - 1)
    blk = lax.broadcasted_iota(jnp.int32, (tq, LANES), 1)
    cur = t_col // SLC_BLK
    forced = (blk == 0) | (blk == cur) | (blk == cur - 1)
    future = blk * SLC_BLK > t_col
    start = pl.multiple_of(jnp.maximum(t0 - WIN, 0), LANES)
    dw = (t0 - start) // LANES
    wtiles = _tile_ids(dw, wlen // LANES, nd)
    wtiles[0] = jnp.where(dw == WIN // LANES, nd + 1, wtiles[0])
    open_ties = []
    for g in range(G):
        lg_scr, p_scr = lg_scrs[g], p_scrs[g]
        qg = q_ref[0, g * R:(g + 1) * R].reshape(R * tq, HEAD_DIM)
        lg_scr[0:R * tq, 0:n_cmp] = _dot_nt(qg, kc_ref[0, g])
        for r, rows in enumerate(slabs):
            cmask = cmp_end <= t_rows + r * ROWS
            for rr in range(R):
                lc = jnp.where(cmask, lg_scr[rr * tq + r * ROWS:rr * tq + (r + 1) * ROWS, 0:n_cmp], NEG)
                e = jnp.where(cmask, jnp.exp2(lc - jnp.max(lc, axis=1, keepdims=True)), 0.0)
                p_scr[rr * tq + r * ROWS:rr * tq + (r + 1) * ROWS, 0:n_cmp] = e.astype(BF16)
        cmp_all = _dot(p_scr[:, 0:n_cmp], vc_ref[0, g])
        inv = 1.0 / jnp.maximum(cmp_all[:, HEAD_DIM:2 * HEAD_DIM], 1e-30)
        o_cmp = cmp_all[:, 0:HEAD_DIM] * inv
        imp_heads = cmp_all[:, 2 * HEAD_DIM:3 * HEAD_DIM] * inv

        lw_scr[0:R * tq, 0:wlen] = _dot_nt(qg, kw_ref[0, g, pl.ds(start, wlen), :])
        for rr in range(R):
            for r, rows in enumerate(slabs):
                lw = (lw_scr[rr * tq + r * ROWS:rr * tq + (r + 1) * ROWS, 0:wlen]
                      + _bias_rows(toe_ref, g * R + rr, wtiles, rows))
                pw = jnp.exp2(lw - jnp.max(lw, axis=1, keepdims=True))
                pw_scr[rr * tq + r * ROWS:rr * tq + (r + 1) * ROWS, 0:wlen] = pw.astype(BF16)
        o_win = _dot(pw_scr[:, 0:wlen], _with_ones(vw_ref[0, g, pl.ds(start, wlen), :]))
        for rr in range(R):
            hh = g * R + rr
            wrows = slice(rr * tq, (rr + 1) * tq)
            ocw_scr[hh * tq:(hh + 1) * tq, :] = (
                gates[:, 3 * hh:3 * hh + 1] * o_cmp[wrows]
                + gates[:, 3 * hh + 2:3 * hh + 3] * (o_win[wrows, 0:HEAD_DIM] / o_win[wrows, HEAD_DIM:2 * HEAD_DIM]))
        imp = _fold_rows(imp_heads, tq)
        imp = jnp.where(forced, 1e9, imp)
        imp = jnp.where(future, NEG, imp)
        imp = jnp.where(blk < n_slc, imp, -jnp.inf)
        sel = _top_blocks(imp, blk, n_sel, exact_ties=False)
        sel_scr[g] = sel
        imp_scr[g] = imp
        cut = jnp.min(jnp.where(sel > 0.0, imp, jnp.inf), axis=1, keepdims=True)
        open_ties.append(jnp.max(jnp.where((imp == cut) & (sel == 0.0) & ~future, 1.0, 0.0)))
        qx_scr[g * R * tq:(g + 1) * R * tq, 0:HEAD_DIM] = qg

    @pl.when(jnp.maximum(*open_ties) > 0.0)
    def _():
        for g in range(G):
            sel_scr[g] = _top_blocks(imp_scr[g], blk, n_sel, exact_ties=True)

    for g in range(G):
        block_bias = jnp.where((sel_scr[g] > 0.0) & ~future, 0.0, NEG).astype(BF16)
        for rr in range(R):
            qx_scr[(g * R + rr) * tq:(g * R + rr + 1) * tq, HEAD_DIM:HEAD_DIM + LANES] = block_bias

    nch = (t0 + tq + tk - 1) // tk
    lane_minus_block = (lax.broadcasted_iota(jnp.int32, (tk, LANES), 1)
                        - lax.broadcasted_iota(jnp.int32, (tk, LANES), 0) // SLC_BLK)

    def k_aug(c, g):
        s0 = pl.multiple_of(c * tk, tk)
        block_of_key = jnp.where(lane_minus_block == s0 // SLC_BLK, 1.0, 0.0).astype(BF16)
        return jnp.concatenate([ks_ref[0, g, pl.ds(s0, tk), :], block_of_key], axis=1)

    _flash_groups(nch, qx_scr, k_aug, lambda c, g: _with_ones(vs_ref[0, g, pl.ds(pl.multiple_of(c * tk, tk), tk), :]),
                  lambda c: _tile_ids((t0 - c * tk) // LANES, n_sub, nd),
                  toe_ref, lg_scrs, p_scrs, m_scr, acc_scr, tq)

    for hh in range(N_HEADS):
        rows = slice(hh * tq, (hh + 1) * tq)
        o_slc = acc_scr[rows, 0:HEAD_DIM] / acc_scr[rows, HEAD_DIM:2 * HEAD_DIM]
        o = ocw_scr[rows, :] + gates[:, 3 * hh + 1:3 * hh + 2] * o_slc
        o_ref[0, :, hh * HEAD_DIM:(hh + 1) * HEAD_DIM] = o.astype(o_ref.dtype)


def _selection_map(n_cmp, n_slc):
    c0 = CMP_STRIDE * np.arange(n_cmp)[:, None]
    s0 = SLC_BLK * np.arange(n_slc)[None, :]
    ov = np.clip(np.minimum(c0 + CMP_LEN, s0 + SLC_BLK) - np.maximum(c0, s0), 0, None)
    return (ov / CMP_LEN).astype(np.float32)


def _nsa_attn(q, gates, kc, vc, ks, vs, kw, vw, toe, tq=128, tk=512):
    B, _, T, _ = q.shape
    tk = min(tk, T)
    assert tq == LANES
    nd = toe.shape[0] - 2
    n_cmp_pad = kc.shape[2]
    n_cmp = (T - CMP_LEN) // CMP_STRIDE + 1
    n_slc = T // SLC_BLK
    assert n_slc <= LANES and T >= WIN + tq
    selmap = np.zeros((n_cmp_pad, LANES), np.float32)
    selmap[:n_cmp, :n_slc] = _selection_map(n_cmp, n_slc)
    vc_all = jnp.concatenate([vc, jnp.ones_like(vc), jnp.broadcast_to(jnp.asarray(selmap, BF16), vc.shape)], axis=-1)
    kern = functools.partial(_nsa_attn_kernel, tq=tq, tk=tk, nd=nd, n_slc=n_slc, n_sel=min(N_SLC_MAX, n_slc))
    G, R = N_KV_GROUPS, HEADS_PER_GROUP
    width = max(n_cmp_pad, tk, WIN + tq)
    res = lambda n, d=HEAD_DIM: pl.BlockSpec((1, G, n, d), lambda b, i: (b, 0, 0, 0), pipeline_mode=pl.Buffered(1))
    return pl.pallas_call(
        kern,
        grid=(B, T // tq),
        in_specs=[
            pl.BlockSpec((1, N_HEADS, tq, HEAD_DIM), lambda b, i: (b, 0, i, 0)),
            pl.BlockSpec((1, tq, 3 * N_HEADS), lambda b, i: (b, i, 0)),
            res(n_cmp_pad), res(n_cmp_pad, 3 * HEAD_DIM), res(T), res(T), res(T), res(T),
            _const_spec(toe.shape),
        ],
        out_specs=pl.BlockSpec((1, tq, N_HEADS * HEAD_DIM), lambda b, i: (b, i, 0)),
        out_shape=jax.ShapeDtypeStruct((B, T, N_HEADS * HEAD_DIM), BF16),
        scratch_shapes=[
            pltpu.VMEM((G, tq, LANES), F32),
            pltpu.VMEM((G, tq, LANES), F32),
            pltpu.VMEM((R * tq, width), F32),
            pltpu.VMEM((R * tq, width), F32),
            pltpu.VMEM((R * tq, width), BF16),
            pltpu.VMEM((R * tq, width), BF16),
            pltpu.VMEM((R * tq, WIN + tq), F32),
            pltpu.VMEM((R * tq, WIN + tq), BF16),
            pltpu.VMEM((N_HEADS * tq, HEAD_DIM + LANES), BF16),
            pltpu.VMEM((N_HEADS, tq, LANES), F32),
            pltpu.VMEM((N_HEADS * tq, 2 * HEAD_DIM), F32),
            pltpu.VMEM((N_HEADS * tq, HEAD_DIM), F32),
        ],
        compiler_params=_params(("parallel", "arbitrary")),
        name="nsa_attn",
    )(q, gates, kc, vc_all, ks, vs, kw, vw, toe)


def _dsa_attention(x, toe, g_attn, w_in, gq, gkv, gk, w_uq, w_q_idx, w_uk):
    qa, qi, ckv, kidx, wi = _dsa_proj(x, g_attn, w_in, gq, gkv, gk, w_uq, w_uk, w_q_idx)
    return _dsa_attn(qi, wi, kidx, qa, ckv, toe)


def _shared_kv(x, g_kv, w_kv, pos_k, pos_v, w1_k, w2_k, w1_v, w2_v):
    kc_raw, vc_raw, ks, vs, kw, vw = _kv_proj(x, g_kv, w_kv)
    return _compress(kc_raw, pos_k, w1_k, w2_k), _compress(vc_raw, pos_v, w1_v, w2_v), ks, vs, kw, vw


def _nsa_attention(x, toe, kv_shared, g_attn, w_in):
    q, gates = _nsa_proj(x, g_attn, w_in)
    return _nsa_attn(q, gates, *kv_shared, toe)


def kernel(x, g_attn, g_mlp, w_up, w_down, rel_bias, a_w_in, a_g_q_lat, a_g_kv_lat, a_g_k_idx, a_w_uq, a_w_q_idx, a_w_uk, a_w_uv, a_w_o, g_kv_shared, w_kv_shared, cmp_pos_k, cmp_pos_v, cmp_w1_k, cmp_w2_k, cmp_w1_v, cmp_w2_v, b_w_in, b_w_o, g_final):
    depth = g_attn.shape[0]
    n_a = a_w_in.shape[0]
    toe = _bias_tiles(rel_bias, x.shape[1])
    kv_shared = None
    for l in range(depth):
        if l < n_a:
            o = _dsa_attention(x, toe, g_attn[l], a_w_in[l], a_g_q_lat[l], a_g_kv_lat[l], a_g_k_idx[l],
                               a_w_uq[l], a_w_q_idx[l], a_w_uk[l])
            w_o, w_uv = a_w_o[l], a_w_uv[l]
        else:
            o = _nsa_attention(x, toe, kv_shared, g_attn[l], b_w_in[l - n_a])
            w_o, w_uv = b_w_o[l - n_a], None
        x = _out_mlp(o, x, w_o, g_mlp[l], w_up[l], w_down[l], w_uv, g_final if l == depth - 1 else None)
        if l == n_a - 1:
            kv_shared = _shared_kv(x, g_kv_shared, w_kv_shared, cmp_pos_k, cmp_pos_v,
                                   cmp_w1_k, cmp_w2_k, cmp_w1_v, cmp_w2_v)
    return x
```
